```python
import math
import jax, jax.numpy as jnp
from jax import lax
import numpy as np

D_MODEL = 1024
BATCH = 16
SEQ = 4096
DEPTH = 4

GLA_HEADS = 4
GLA_DK = 128
GLA_DV = 256
GLA_RANK = 16
GLA_TAU = 16.0
GLA_CHUNK = 64
DIL_PATTERN = ((128, 1), (512, 4), (2048, 16))
DIL_HEADS = 8
DIL_HD = 64
ROPE_THETA = 10000.0
N_EXPERTS = 16
N_GROUPS = 4
EXPERTS_PER_GROUP = N_EXPERTS // N_GROUPS
TOP_K = 2
D_FF = 1024
LN_EPS = 1e-5
RMS_EPS = 1e-6
DN_ALPHA = (2 * DEPTH) ** 0.25
DN_BETA = (8 * DEPTH) ** -0.25
NEG_INF = -1e30

GLA_QK_W = GLA_HEADS * GLA_DK
GLA_V_W = GLA_HEADS * GLA_DV
N_DIL = len(DIL_PATTERN)
DIL_GROUP_W = DIL_HEADS * DIL_HD
N_BRANCH = 2
SPLITS = (GLA_QK_W, GLA_QK_W, GLA_V_W, GLA_V_W, GLA_RANK,
          N_DIL * DIL_GROUP_W, N_DIL * DIL_GROUP_W, N_DIL * DIL_GROUP_W, N_BRANCH * D_MODEL)
V_BLOCKS = (2, 7)
D_IN = sum(SPLITS)
SPLIT_POINTS = tuple(sum(SPLITS[:i + 1]) for i in range(len(SPLITS) - 1))

kernel_name = "hybrid_gla_dilated_gated_moe_deepnorm"


def layer_norm(x, g, b):
    xf = x.astype(jnp.float32)
    mu = jnp.mean(xf, axis=-1, keepdims=True)
    var = jnp.mean(jnp.square(xf - mu), axis=-1, keepdims=True)
    y = (xf - mu) * lax.rsqrt(var + LN_EPS) * g.astype(jnp.float32) + b.astype(jnp.float32)
    return y.astype(x.dtype)


def rotary(x, positions):
    e = x.shape[-1]
    half = e // 2
    inv_freq = jnp.power(jnp.float32(ROPE_THETA), -jnp.arange(0, e, 2, dtype=jnp.float32) / e)
    ang = positions.astype(jnp.float32)[..., None] * inv_freq
    cos = jnp.cos(ang)[:, :, None, :]
    sin = jnp.sin(ang)[:, :, None, :]
    xf = x.astype(jnp.float32)
    x1, x2 = xf[..., :half], xf[..., half:]
    return jnp.concatenate([x1 * cos - x2 * sin, x2 * cos + x1 * sin], axis=-1).astype(x.dtype)


def gla_chunked(q, k, v, log_a):
    B, S, H, DK = q.shape
    DV = v.shape[-1]
    C = GLA_CHUNK
    N = S // C

    def chunks(t):
        return t.astype(jnp.float32).reshape(B, N, C, H, t.shape[-1]).transpose(0, 3, 1, 2, 4)

    qc = chunks(q) * (DK ** -0.5)
    kc = chunks(k)
    vc = chunks(v)
    b = jnp.cumsum(chunks(log_a), axis=3)
    b_end = b[:, :, :, -1:, :]
    q_t = qc * jnp.exp(b)
    k_t = kc * jnp.exp(-b)
    k_s = kc * jnp.exp(b_end - b)
    causal = jnp.tril(jnp.ones((C, C), dtype=bool))
    att = jnp.where(causal, jnp.einsum('bhnid,bhnjd->bhnij', q_t, k_t), 0.0)
    o_intra = jnp.einsum('bhnij,bhnjv->bhniv', att, vc)
    decay = jnp.exp(b_end[:, :, :, 0, :])

    def step(state, inp):
        q_n, k_n, v_n, d_n = inp
        o_n = jnp.einsum('bhcd,bhdv->bhcv', q_n, state)
        state = d_n[..., None] * state + jnp.einsum('bhcd,bhcv->bhdv', k_n, v_n)
        return state, o_n

    init = jnp.zeros((B, H, DK, DV), jnp.float32)
    xs = (jnp.moveaxis(q_t, 2, 0), jnp.moveaxis(k_s, 2, 0), jnp.moveaxis(vc, 2, 0), jnp.moveaxis(decay, 2, 0))
    _, o_inter = lax.scan(step, init, xs)
    o = o_intra + jnp.moveaxis(o_inter, 0, 2)
    return o.transpose(0, 2, 3, 1, 4).reshape(B, S, H, DV)


def dilated_group(q, k, v, window, dilation):
    B, S, H, E = q.shape
    steps = window // dilation
    L = S // dilation
    nb = -(-L // steps)
    Lp = nb * steps

    def to_sub(t):
        t = t.reshape(B, L, dilation, H, E).transpose(0, 2, 3, 1, 4)
        t = jnp.pad(t, ((0, 0), (0, 0), (0, 0), (0, Lp - L), (0, 0)))
        return t.reshape(B, dilation, H, nb, steps, E)

    qb, kb, vb = to_sub(q), to_sub(k), to_sub(v)

    def with_prev(t):
        prev = jnp.concatenate([jnp.zeros_like(t[:, :, :, :1]), t[:, :, :, :-1]], axis=3)
        return jnp.concatenate([prev, t], axis=4)

    kw, vw = with_prev(kb), with_prev(vb)
    s = jnp.einsum('bdhnqe,bdhnke->bdhnqk', qb, kw).astype(jnp.float32) * (E ** -0.5)
    qi = jnp.arange(steps)[:, None]
    kj = jnp.arange(2 * steps)[None, :]
    rel = steps + qi - kj
    band = (rel >= 0) & (rel <= steps)
    blk = jnp.arange(nb)[:, None, None]
    key_exists = (blk * steps - steps + kj[None]) >= 0
    mask = band[None] & key_exists
    s = jnp.where(mask, s, NEG_INF)
    m = jnp.max(s, axis=-1, keepdims=True)
    p = jnp.exp(s - m)
    l = jnp.sum(p, axis=-1, keepdims=True)
    o = jnp.einsum('bdhnqk,bdhnke->bdhnqe', p, vw.astype(jnp.float32)) / l
    lse = (m + jnp.log(l))[..., 0]
    o = o.reshape(B, dilation, H, Lp, E)[:, :, :, :L].transpose(0, 3, 1, 2, 4).reshape(B, S, H, E)
    lse = lse.reshape(B, dilation, H, Lp)[..., :L].transpose(0, 3, 1, 2).reshape(B, S, H)
    return o, lse


def dilated_branch(q, k, v, positions):
    B, S, _ = q.shape
    q = rotary(q.reshape(B, S, N_DIL * DIL_HEADS, DIL_HD), positions)
    k = rotary(k.reshape(B, S, N_DIL * DIL_HEADS, DIL_HD), positions)
    v = v.reshape(B, S, N_DIL * DIL_HEADS, DIL_HD)
    outs, lses = [], []
    for g, (window, dilation) in enumerate(DIL_PATTERN):
        sl = slice(g * DIL_HEADS, (g + 1) * DIL_HEADS)
        o, lse = dilated_group(q[:, :, sl], k[:, :, sl], v[:, :, sl], window, dilation)
        outs.append(o)
        lses.append(lse)
    w = jax.nn.softmax(jnp.stack(lses, axis=0), axis=0)
    o = jnp.sum(w[..., None] * jnp.stack(outs, axis=0), axis=0)
    return o.reshape(B, S, DIL_GROUP_W)


def grouped_moe(x, w_router, b_router, w_gate, w_up, w_down):
    B, S, _ = x.shape
    logits = jnp.einsum('bsd,de->bse', x, w_router).astype(jnp.float32) + b_router.astype(jnp.float32)
    probs = jax.nn.softmax(logits, axis=-1)
    pg = probs.reshape(B, S, N_GROUPS, EXPERTS_PER_GROUP)
    group_score = lax.top_k(pg, TOP_K)[0].sum(-1)
    g_sel = jnp.argmax(group_score, axis=-1)
    in_group = jnp.arange(N_GROUPS) == g_sel[..., None]
    cand = jnp.where(in_group[..., None], pg, -1.0).reshape(B, S, N_EXPERTS)
    vals, idx = lax.top_k(cand, TOP_K)
    gates = vals / jnp.sum(vals, axis=-1, keepdims=True)
    combine = jnp.sum(jax.nn.one_hot(idx, N_EXPERTS, dtype=jnp.float32) * gates[..., None], axis=-2)
    y = jnp.zeros((B, S, x.shape[-1]), jnp.float32)
    for e in range(N_EXPERTS):
        h = jax.nn.silu(x @ w_gate[e]) * (x @ w_up[e])
        y = y + combine[..., e:e + 1] * (h @ w_down[e]).astype(jnp.float32)
    return y.astype(x.dtype)


def setup_inputs(seed: int = 0) -> dict:
    key = jax.random.key(seed)
    ks = jax.random.split(key, 20)
    f32 = jnp.float32

    def nrm(k, shape, scale):
        return jax.random.normal(k, shape, f32) * scale

    x = nrm(ks[0], (BATCH, SEQ, D_MODEL), 1.0)
    offset = jax.random.randint(ks[1], (BATCH, 1), 0, SEQ)
    positions = (offset + jnp.arange(SEQ)[None, :]).astype(jnp.int32)
    col_scale = jnp.concatenate([jnp.full((n,), DN_BETA if i in V_BLOCKS else 1.0, f32)
                                 for i, n in enumerate(SPLITS)])
    w_in = nrm(ks[2], (DEPTH, D_MODEL, D_IN), D_MODEL ** -0.5) * col_scale
    w_alpha_up = nrm(ks[3], (DEPTH, GLA_RANK, GLA_QK_W), GLA_RANK ** -0.5)
    b_alpha = nrm(ks[4], (DEPTH, GLA_QK_W), 0.1)
    gla_norm = 1.0 + nrm(ks[5], (DEPTH, GLA_V_W), 0.02)
    w_gla_o = nrm(ks[6], (DEPTH, GLA_V_W, D_MODEL), GLA_V_W ** -0.5)
    w_att_o = nrm(ks[7], (DEPTH, DIL_GROUP_W, D_MODEL), DIL_GROUP_W ** -0.5)
    w_out = nrm(ks[8], (DEPTH, D_MODEL, D_MODEL), D_MODEL ** -0.5 * DN_BETA)
    ln1_g = 1.0 + nrm(ks[9], (DEPTH, D_MODEL), 0.02)
    ln1_b = nrm(ks[10], (DEPTH, D_MODEL), 0.02)
    w_router = nrm(ks[11], (D_MODEL, N_EXPERTS), D_MODEL ** -0.5)
    b_router = nrm(ks[12], (N_EXPERTS,), 0.01)
    w_e_gate = nrm(ks[13], (DEPTH, N_EXPERTS, D_MODEL, D_FF), D_MODEL ** -0.5)
    w_e_up = nrm(ks[14], (DEPTH, N_EXPERTS, D_MODEL, D_FF), D_MODEL ** -0.5)
    w_e_down = nrm(ks[15], (DEPTH, N_EXPERTS, D_FF, D_MODEL), D_FF ** -0.5 * DN_BETA)
    ln2_g = 1.0 + nrm(ks[16], (DEPTH, D_MODEL), 0.02)
    ln2_b = nrm(ks[17], (DEPTH, D_MODEL), 0.02)
    return {"x": x, "positions": positions, "w_in": w_in, "w_alpha_up": w_alpha_up,
            "b_alpha": b_alpha, "gla_norm": gla_norm, "w_gla_o": w_gla_o, "w_att_o": w_att_o,
            "w_out": w_out, "ln1_g": ln1_g, "ln1_b": ln1_b, "w_router": w_router,
            "b_router": b_router, "w_e_gate": w_e_gate, "w_e_up": w_e_up, "w_e_down": w_e_down,
            "ln2_g": ln2_g, "ln2_b": ln2_b}


def reference(x, positions, w_in, w_alpha_up, b_alpha, gla_norm, w_gla_o, w_att_o, w_out,
              ln1_g, ln1_b, w_router, b_router, w_e_gate, w_e_up, w_e_down, ln2_g, ln2_b):
    B, S, _ = x.shape
    for i in range(DEPTH):
        h = x @ w_in[i]
        q_g, k_g, v_g, r_g, a_g, q_d, k_d, v_d, gate_in = jnp.split(h, SPLIT_POINTS, axis=-1)
        log_a = jax.nn.log_sigmoid((a_g @ w_alpha_up[i]).astype(jnp.float32)
                                   + b_alpha[i].astype(jnp.float32)) / GLA_TAU
        o = gla_chunked(q_g.reshape(B, S, GLA_HEADS, GLA_DK), k_g.reshape(B, S, GLA_HEADS, GLA_DK),
                        v_g.reshape(B, S, GLA_HEADS, GLA_DV), log_a.reshape(B, S, GLA_HEADS, GLA_DK))
        o = o * lax.rsqrt(jnp.mean(jnp.square(o), axis=-1, keepdims=True) + RMS_EPS)
        o = o * gla_norm[i].astype(jnp.float32).reshape(GLA_HEADS, GLA_DV)
        y_gla = (o.reshape(B, S, GLA_V_W).astype(x.dtype) * jax.nn.silu(r_g)) @ w_gla_o[i]
        y_att = dilated_branch(q_d, k_d, v_d, positions).astype(x.dtype) @ w_att_o[i]
        g_gla, g_att = jnp.split(jax.nn.sigmoid(gate_in), N_BRANCH, axis=-1)
        mix = (g_gla * y_gla + g_att * y_att) @ w_out[i]
        x = layer_norm(DN_ALPHA * x + mix, ln1_g[i], ln1_b[i])
        y_moe = grouped_moe(x, w_router, b_router, w_e_gate[i], w_e_up[i], w_e_down[i])
        x = layer_norm(DN_ALPHA * x + y_moe, ln2_g[i], ln2_b[i])
    return x
```

```python
import functools

import jax
import jax.numpy as jnp
from jax import lax
from jax.experimental import pallas as pl
from jax.experimental.pallas import tpu as pltpu

F32 = jnp.float32
BF16 = jnp.bfloat16
I32 = jnp.int32

D_MODEL = 1024
N_LAYERS = 4
GLA_HEADS = 4
GLA_DK = 128
GLA_DV = 256
GLA_RANK = 16
GLA_TAU = 16.0
GLA_CHUNK = 64
DIL_PATTERN = ((128, 1), (512, 4), (2048, 16))
DIL_HEADS = 8
DIL_HD = 64
DIL_STEPS = 128
ROPE_THETA = 10000.0
N_EXPERTS = 16
N_GROUPS = 4
EXPERTS_PER_GROUP = 4
D_FF = 1024
LN_EPS = 1e-5
RMS_EPS = 1e-6
DN_ALPHA = (2 * N_LAYERS) ** 0.25
NEG_INF = -1e30

GLA_QK_W = GLA_HEADS * GLA_DK
GLA_V_W = GLA_HEADS * GLA_DV
DIL_GROUP_W = DIL_HEADS * DIL_HD
N_DIL = len(DIL_PATTERN)
MAIN_W = 2 * GLA_QK_W + 2 * GLA_V_W + 2 * D_MODEL
GATE_COL = 2 * GLA_QK_W + 2 * GLA_V_W

LANES = 128
PAIRS = ((0, 1), (0, 2), (0, 3), (1, 2), (1, 3), (2, 3))
N_BUCKETS = N_GROUPS * len(PAIRS)
BUCKET_ROWS = 32

PROJ_TM = 1024
PROJ_TN = 512
GLA_T = 512
MIX_TM = 512
SORT_TS = 1024
MOE_T = 256
ROW_TM = 512
VMEM_LIMIT = 56 * 1024 * 1024


def _cp(*sem):
    return pltpu.CompilerParams(dimension_semantics=sem, vmem_limit_bytes=VMEM_LIMIT)


def _rope_kernel(pos_ref, invf_ref, cos_ref, sin_ref):
    ang = pos_ref[...].astype(F32) * invf_ref[...]
    lane = lax.broadcasted_iota(I32, ang.shape, 1)
    sign = jnp.where((lane & (DIL_HD // 2)) == 0, -1.0, 1.0)
    cos_ref[...] = jnp.cos(ang)
    sin_ref[...] = jnp.sin(ang) * sign


def _rope_tables(positions):
    n = positions.size
    tm = 2048
    half = DIL_HD // 2
    inv_freq = jnp.power(jnp.float32(ROPE_THETA), -jnp.arange(0, DIL_HD, 2, dtype=F32) / DIL_HD)
    invf = jnp.tile(inv_freq, LANES // half).reshape(1, LANES)
    return pl.pallas_call(
        _rope_kernel,
        grid=(n // tm,),
        in_specs=[pl.BlockSpec((tm, 1), lambda i: (i, 0)),
                  pl.BlockSpec((1, LANES), lambda i: (0, 0))],
        out_specs=[pl.BlockSpec((tm, LANES), lambda i: (i, 0)),
                   pl.BlockSpec((tm, LANES), lambda i: (i, 0))],
        out_shape=[jax.ShapeDtypeStruct((n, LANES), F32)] * 2,
        compiler_params=_cp("parallel"),
        name="rope_tables",
    )(positions.reshape(n, 1), invf)


def _proj_kernel(x_ref, w_ref, o_ref, xb_ref):
    @pl.when(pl.program_id(1) == 0)
    def _():
        xb_ref[...] = x_ref[...].astype(BF16)

    o_ref[...] = jnp.dot(xb_ref[...], w_ref[...], preferred_element_type=F32).astype(o_ref.dtype)


def _proj(x, w, out_dtype, tn):
    n, k = x.shape
    ncol = w.shape[1]
    tm = min(PROJ_TM, n)
    return pl.pallas_call(
        _proj_kernel,
        grid=(n // tm, ncol // tn),
        in_specs=[pl.BlockSpec((tm, k), lambda i, j: (i, 0)),
                  pl.BlockSpec((k, tn), lambda i, j: (0, j))],
        out_specs=pl.BlockSpec((tm, tn), lambda i, j: (i, j)),
        out_shape=jax.ShapeDtypeStruct((n, ncol), out_dtype),
        scratch_shapes=[pltpu.VMEM((tm, k), BF16)],
        compiler_params=_cp("parallel", "arbitrary"),
        name="proj",
    )(x, w)


def _proj_dil_kernel(x_ref, w_ref, cos_ref, sin_ref, o_ref, xb_ref, acc_ref, *, dil, tm):
    j = pl.program_id(1)

    @pl.when(j == 0)
    def _():
        xb_ref[...] = x_ref[...].astype(BF16)

    acc = jnp.dot(xb_ref[...], w_ref[...], preferred_element_type=F32)

    def store_chunks(val):
        for c in range(DIL_GROUP_W // LANES):
            acc_ref[c] = val[:, c * LANES:(c + 1) * LANES]

    @pl.when(j < 2)
    def _():
        reps = DIL_GROUP_W // LANES
        cos_t = jnp.concatenate([cos_ref[...]] * reps, axis=1)
        sin_t = jnp.concatenate([sin_ref[...]] * reps, axis=1)
        lane = lax.broadcasted_iota(I32, acc.shape, 1)
        half = DIL_HD // 2
        partner = jnp.where((lane & half) == 0,
                            pltpu.roll(acc, DIL_GROUP_W - half, 1),
                            pltpu.roll(acc, half, 1))
        scale = jnp.where(j == 0, DIL_HD ** -0.5, 1.0)
        store_chunks((acc * cos_t + partner * sin_t) * scale)

    @pl.when(j == 2)
    def _():
        store_chunks(acc)

    rows = tm // dil
    for r in range(dil):
        for c in range(DIL_GROUP_W // LANES):
            cs = slice(c * LANES, (c + 1) * LANES)
            if dil == 1:
                o_ref[0, 0, r, :, cs] = acc_ref[c].astype(BF16)
            else:
                o_ref[0, 0, r, :, cs] = acc_ref[c, pl.ds(r, rows, stride=dil), :].astype(BF16)


def _proj_dil(x, w, cos_t, sin_t, batch, seq, dil):
    n, k = x.shape
    tm = min(PROJ_TM, seq)
    tiles_per_seq = seq // tm
    sub = seq // dil
    kern = functools.partial(_proj_dil_kernel, dil=dil, tm=tm)
    return pl.pallas_call(
        kern,
        grid=(n // tm, 3),
        in_specs=[pl.BlockSpec((tm, k), lambda i, j: (i, 0)),
                  pl.BlockSpec((k, DIL_GROUP_W), lambda i, j: (0, j)),
                  pl.BlockSpec((tm, LANES), lambda i, j: (i, 0)),
                  pl.BlockSpec((tm, LANES), lambda i, j: (i, 0))],
        out_specs=pl.BlockSpec((1, 1, dil, tm // dil, DIL_GROUP_W),
                               lambda i, j: (j, i // tiles_per_seq, 0, i % tiles_per_seq, 0)),
        out_shape=jax.ShapeDtypeStruct((3, batch, dil, sub, DIL_GROUP_W), BF16),
        scratch_shapes=[pltpu.VMEM((tm, k), BF16), pltpu.VMEM((DIL_GROUP_W // LANES, tm, LANES), F32)],
        compiler_params=_cp("parallel", "arbitrary"),
        name=f"proj_dil{dil}",
    )(x, w, cos_t, sin_t)


def _split_dot(mat_bf16, val_f32):
    hi = val_f32.astype(BF16)
    lo = (val_f32 - hi.astype(F32)).astype(BF16)
    return (jnp.dot(mat_bf16, hi, preferred_element_type=F32)
            + jnp.dot(mat_bf16, lo, preferred_element_type=F32))


def _gla_kernel(q_ref, k_ref, v_ref, r_ref, a_ref, wup_ref, balpha_ref, gnorm_ref, tri_ref, ups_ref,
                o_ref, state_ref, sprev_ref):
    t = pl.program_id(1)
    n_chunks = GLA_T // GLA_CHUNK

    @pl.when(t == 0)
    def _():
        state_ref[...] = jnp.zeros_like(state_ref)

    z = jnp.dot(a_ref[0].astype(BF16), wup_ref[...], preferred_element_type=F32) + balpha_ref[...]
    log_a = (jnp.minimum(z, 0.0) - jnp.log1p(jnp.exp(-jnp.abs(z)))) / GLA_TAU
    tri = tri_ref[...]
    b = _split_dot(tri, log_a)
    c = _split_dot(ups_ref[...], log_a)
    eb = jnp.exp(b)
    kf = k_ref[0].astype(F32)
    qt = (q_ref[0].astype(F32) * eb * (GLA_DK ** -0.5)).astype(BF16)
    kt = (kf * jnp.exp(-b)).astype(BF16)
    ks = (kf * jnp.exp(c)).astype(BF16)
    causal = tri > 0

    for h in range(GLA_HEADS):
        ck = slice(h * GLA_DK, (h + 1) * GLA_DK)
        cv = slice(h * GLA_DV, (h + 1) * GLA_DV)
        qt_h, kt_h, ks_h = qt[:, ck], kt[:, ck], ks[:, ck]
        v_h = v_ref[0, :, cv]
        att = lax.dot_general(qt_h, kt_h, (((1,), (1,)), ((), ())), preferred_element_type=F32)
        att = jnp.where(causal, att, 0.0).astype(BF16)
        o_h = jnp.dot(att, v_h, preferred_element_type=F32)

        state = state_ref[h]
        for n in range(n_chunks):
            rs = slice(n * GLA_CHUNK, (n + 1) * GLA_CHUNK)
            contrib = lax.dot_general(v_h[rs], ks_h[rs], (((0,), (0,)), ((), ())),
                                      preferred_element_type=F32)
            sprev_ref[h, n] = state.astype(BF16)
            decay = eb[(n + 1) * GLA_CHUNK - 1:(n + 1) * GLA_CHUNK, ck]
            state = state * decay + contrib
        state_ref[h] = state

        inter = [lax.dot_general(qt_h[n * GLA_CHUNK:(n + 1) * GLA_CHUNK], sprev_ref[h, n],
                                 (((1,), (1,)), ((), ())), preferred_element_type=F32)
                 for n in range(n_chunks)]
        o_h = o_h + jnp.concatenate(inter, axis=0)

        o_h = o_h * lax.rsqrt(jnp.mean(jnp.square(o_h), axis=-1, keepdims=True) + RMS_EPS)
        o_h = o_h * gnorm_ref[:, cv]
        r_h = r_ref[0, :, cv].astype(F32)
        o_ref[0, :, cv] = (o_h * (r_h * jax.nn.sigmoid(r_h))).astype(o_ref.dtype)


def _gla(h_main, a_proj, wup, balpha, gnorm, tri, ups, batch, seq):
    h3 = h_main.reshape(batch, seq, MAIN_W)
    a3 = a_proj.reshape(batch, seq, LANES)
    n_chunks = GLA_T // GLA_CHUNK
    return pl.pallas_call(
        _gla_kernel,
        grid=(batch, seq // GLA_T),
        in_specs=[pl.BlockSpec((1, GLA_T, GLA_QK_W), lambda b, t: (b, t, 0)),
                  pl.BlockSpec((1, GLA_T, GLA_QK_W), lambda b, t: (b, t, 1)),
                  pl.BlockSpec((1, GLA_T, GLA_V_W), lambda b, t: (b, t, 1)),
                  pl.BlockSpec((1, GLA_T, GLA_V_W), lambda b, t: (b, t, 2)),
                  pl.BlockSpec((1, GLA_T, LANES), lambda b, t: (b, t, 0)),
                  pl.BlockSpec((LANES, GLA_QK_W), lambda b, t: (0, 0)),
                  pl.BlockSpec((1, GLA_QK_W), lambda b, t: (0, 0)),
                  pl.BlockSpec((1, GLA_V_W), lambda b, t: (0, 0)),
                  pl.BlockSpec((GLA_T, GLA_T), lambda b, t: (0, 0)),
                  pl.BlockSpec((GLA_T, GLA_T), lambda b, t: (0, 0))],
        out_specs=pl.BlockSpec((1, GLA_T, GLA_V_W), lambda b, t: (b, t, 0)),
        out_shape=jax.ShapeDtypeStruct((batch, seq, GLA_V_W), BF16),
        scratch_shapes=[pltpu.VMEM((GLA_HEADS, GLA_DV, GLA_DK), F32),
                        pltpu.VMEM((GLA_HEADS, n_chunks, GLA_DV, GLA_DK), BF16)],
        compiler_params=_cp("parallel", "arbitrary"),
        name="gla",
    )(h3, h3, h3, h3, a3, wup, balpha, gnorm, tri, ups)


def _gla_masks():
    i = jnp.arange(GLA_T)[:, None]
    j = jnp.arange(GLA_T)[None, :]
    same = (i // GLA_CHUNK) == (j // GLA_CHUNK)
    tri = (same & (j <= i)).astype(BF16)
    ups = (same & (j > i)).astype(BF16)
    return tri, ups


def _dil_attn_kernel(q_ref, kc_ref, kp_ref, vc_ref, vp_ref, o_ref, lse_ref):
    n = pl.program_id(2)
    st = DIL_STEPS
    qi = lax.broadcasted_iota(I32, (st, st), 0)
    kj = lax.broadcasted_iota(I32, (st, st), 1)
    mask_prev = (kj >= qi) & (n > 0)
    mask_cur = kj <= qi
    lane = lax.broadcasted_iota(I32, (st, LANES), 1)
    lse_all = jnp.zeros((st, LANES), F32)
    outs = []
    nt = (((1,), (1,)), ((), ()))
    for h in range(DIL_HEADS):
        cs = slice(h * DIL_HD, (h + 1) * DIL_HD)
        q = q_ref[0, 0, 0, :, cs]
        s_p = lax.dot_general(q, kp_ref[0, 0, 0, :, cs], nt, preferred_element_type=F32)
        s_c = lax.dot_general(q, kc_ref[0, 0, 0, :, cs], nt, preferred_element_type=F32)
        s_p = jnp.where(mask_prev, s_p, NEG_INF)
        s_c = jnp.where(mask_cur, s_c, NEG_INF)
        m = jnp.maximum(jnp.max(s_p, axis=-1, keepdims=True), jnp.max(s_c, axis=-1, keepdims=True))
        p_p = jnp.exp(s_p - m)
        p_c = jnp.exp(s_c - m)
        l = jnp.sum(p_p, axis=-1, keepdims=True) + jnp.sum(p_c, axis=-1, keepdims=True)
        o = (jnp.dot(p_p.astype(BF16), vp_ref[0, 0, 0, :, cs], preferred_element_type=F32)
             + jnp.dot(p_c.astype(BF16), vc_ref[0, 0, 0, :, cs], preferred_element_type=F32))
        outs.append(o / l)
        lse_all = jnp.where(lane == h, m + jnp.log(l), lse_all)
    o_ref[0, 0] = jnp.concatenate(outs, axis=1).astype(o_ref.dtype)
    lse_ref[0, 0] = lse_all


def _dil_attn(qkv, batch, seq, dil):
    sub = seq // dil
    nb = sub // DIL_STEPS
    blk = (1, 1, 1, DIL_STEPS, DIL_GROUP_W)

    def cur(which):
        return pl.BlockSpec(blk, lambda b, r, n: (which, b, r, n, 0))

    def prev(which):
        return pl.BlockSpec(blk, lambda b, r, n: (which, b, r, jnp.maximum(n - 1, 0), 0))

    return pl.pallas_call(
        _dil_attn_kernel,
        grid=(batch, dil, nb),
        in_specs=[cur(0), cur(1), prev(1), cur(2), prev(2)],
        out_specs=[pl.BlockSpec((1, 1, DIL_STEPS, DIL_GROUP_W), lambda b, r, n: (b, r, n, 0)),
                   pl.BlockSpec((1, 1, DIL_STEPS, LANES), lambda b, r, n: (b, r, n, 0))],
        out_shape=[jax.ShapeDtypeStruct((batch, dil, sub, DIL_GROUP_W), BF16),
                   jax.ShapeDtypeStruct((batch, dil, sub, LANES), F32)],
        compiler_params=_cp("parallel", "parallel", "arbitrary"),
        name=f"dil_attn{dil}",
    )(qkv, qkv, qkv, qkv, qkv)


def _layer_norm(x, g, b):
    mu = jnp.mean(x, axis=-1, keepdims=True)
    var = jnp.mean(jnp.square(x - mu), axis=-1, keepdims=True)
    return (x - mu) * lax.rsqrt(var + LN_EPS) * g + b


def _mix_kernel(og_ref, g1_ref, g2_ref, o0_ref, o1_ref, o2_ref, l0_ref, l1_ref, l2_ref, x_ref,
                wgo_ref, wao_ref, wout_ref, lng_ref, lnb_ref, wrt_ref, br_ref, expand_ref,
                x1_ref, bucket_ref, osc_ref, lsc_ref):
    tm = x_ref.shape[0]

    def interleaved(o_ref, l_ref, dil, slot):
        if dil == 1:
            return o_ref[0, 0].astype(F32), l_ref[0, 0]
        rows = tm // dil
        chunks = DIL_GROUP_W // LANES
        for r in range(dil):
            for c in range(chunks):
                osc_ref[slot, c, pl.ds(r, rows, stride=dil), :] = (
                    o_ref[0, r, :, c * LANES:(c + 1) * LANES].astype(F32))
            lsc_ref[slot, pl.ds(r, rows, stride=dil), :] = l_ref[0, r]
        return jnp.concatenate([osc_ref[slot, c] for c in range(chunks)], axis=1), lsc_ref[slot]

    groups = [interleaved(o_ref, l_ref, dil, slot)
              for slot, (o_ref, l_ref, (_, dil)) in enumerate(
                  zip((o0_ref, o1_ref, o2_ref), (l0_ref, l1_ref, l2_ref), DIL_PATTERN))]
    lses = [g[1] for g in groups]
    m = jnp.maximum(jnp.maximum(lses[0], lses[1]), lses[2])
    es = [jnp.exp(l - m) for l in lses]
    inv = 1.0 / (es[0] + es[1] + es[2])
    att = jnp.zeros((tm, DIL_GROUP_W), F32)
    for (o_g, _), e in zip(groups, es):
        att = att + _split_dot_rhs(e * inv, expand_ref[...]) * o_g

    y_att = jnp.dot(att.astype(BF16), wao_ref[...], preferred_element_type=F32)
    y_gla = jnp.dot(og_ref[...], wgo_ref[...], preferred_element_type=F32)
    mixed = (jax.nn.sigmoid(g1_ref[...].astype(F32)) * y_gla
             + jax.nn.sigmoid(g2_ref[...].astype(F32)) * y_att)
    mix = jnp.dot(mixed.astype(BF16), wout_ref[...], preferred_element_type=F32)
    x1 = _layer_norm(DN_ALPHA * x_ref[...] + mix, lng_ref[...], lnb_ref[...])
    x1_ref[...] = x1

    logits = lax.dot_general(wrt_ref[...], x1.astype(BF16), (((1,), (1,)), ((), ())),
                             preferred_element_type=F32) + br_ref[:, 0:1]
    e = jnp.exp(logits - jnp.max(logits, axis=0, keepdims=True))
    probs = e / jnp.sum(e, axis=0, keepdims=True)
    best = None
    for g in range(N_GROUPS):
        for pi, (ea, eb_) in enumerate(PAIRS):
            cand = probs[4 * g + ea:4 * g + ea + 1] + probs[4 * g + eb_:4 * g + eb_ + 1]
            idx = g * len(PAIRS) + pi
            if best is None:
                best, bidx = cand, jnp.zeros(cand.shape, I32)
            else:
                better = cand > best
                best = jnp.where(better, cand, best)
                bidx = jnp.where(better, idx, bidx)
    bucket_ref[0] = bidx


def _split_dot_rhs(val_f32, mat_bf16):
    hi = val_f32.astype(BF16)
    lo = (val_f32 - hi.astype(F32)).astype(BF16)
    return (jnp.dot(hi, mat_bf16, preferred_element_type=F32)
            + jnp.dot(lo, mat_bf16, preferred_element_type=F32))


def _mix(og, h_main, attn, x, wgo, wao, wout, lng, lnb, wrt, br, expand, batch, seq):
    n = batch * seq
    tm = min(MIX_TM, seq)
    tps = seq // tm
    row = lambda i: (i, 0)
    const = lambda i: (0, 0)
    o_specs, l_specs, o_args, l_args = [], [], [], []
    for (o_g, lse_g), (_, dil) in zip(attn, DIL_PATTERN):
        o_specs.append(pl.BlockSpec((1, dil, tm // dil, DIL_GROUP_W), lambda i: (i // tps, 0, i % tps, 0)))
        l_specs.append(pl.BlockSpec((1, dil, tm // dil, LANES), lambda i: (i // tps, 0, i % tps, 0)))
        o_args.append(o_g)
        l_args.append(lse_g)
    gate_blk = GATE_COL // D_MODEL
    return pl.pallas_call(
        _mix_kernel,
        grid=(n // tm,),
        in_specs=[pl.BlockSpec((tm, GLA_V_W), row),
                  pl.BlockSpec((tm, D_MODEL), lambda i: (i, gate_blk)),
                  pl.BlockSpec((tm, D_MODEL), lambda i: (i, gate_blk + 1)),
                  *o_specs, *l_specs,
                  pl.BlockSpec((tm, D_MODEL), row),
                  pl.BlockSpec((GLA_V_W, D_MODEL), const),
                  pl.BlockSpec((DIL_GROUP_W, D_MODEL), const),
                  pl.BlockSpec((D_MODEL, D_MODEL), const),
                  pl.BlockSpec((1, D_MODEL), const),
                  pl.BlockSpec((1, D_MODEL), const),
                  pl.BlockSpec((N_EXPERTS, D_MODEL), const),
                  pl.BlockSpec((N_EXPERTS, LANES), const),
                  pl.BlockSpec((LANES, DIL_GROUP_W), const)],
        out_specs=[pl.BlockSpec((tm, D_MODEL), row),
                   pl.BlockSpec((1, 1, tm), lambda i: (i, 0, 0))],
        out_shape=[jax.ShapeDtypeStruct((n, D_MODEL), F32),
                   jax.ShapeDtypeStruct((n // tm, 1, tm), I32)],
        scratch_shapes=[pltpu.VMEM((N_DIL, DIL_GROUP_W // LANES, tm, LANES), F32),
                        pltpu.VMEM((N_DIL, tm, LANES), F32)],
        compiler_params=_cp("parallel"),
        name="mix",
    )(og, h_main, h_main, *o_args, *l_args, x, wgo, wao, wout, lng, lnb, wrt, br, expand)


def _rank_kernel(bucket_ref, upper_ref, rank_ref, counts_ref, carry_ref):
    @pl.when(pl.program_id(0) == 0)
    def _():
        carry_ref[...] = jnp.zeros_like(carry_ref)

    b = bucket_ref[0]
    ts = b.shape[1]
    onehot = lax.broadcasted_iota(I32, (BUCKET_ROWS, ts), 0) == b
    oh_f = jnp.where(onehot, 1.0, 0.0)
    before = jnp.dot(oh_f.astype(BF16), upper_ref[...], preferred_element_type=F32)
    carry = carry_ref[...]
    rank = jnp.sum(oh_f * (before + carry[:, 0:1]), axis=0, keepdims=True)
    rank_ref[0] = rank.astype(I32)
    carry = carry + jnp.sum(oh_f, axis=1, keepdims=True)
    carry_ref[...] = carry
    counts_ref[...] = carry.astype(I32)


def _bucket_ranks(bucket, n):
    ts = min(SORT_TS, n)
    b3 = bucket.reshape(n // ts, 1, ts)
    j = jnp.arange(ts)
    upper = (j[:, None] < j[None, :]).astype(BF16)
    rank, counts = pl.pallas_call(
        _rank_kernel,
        grid=(n // ts,),
        in_specs=[pl.BlockSpec((1, 1, ts), lambda i: (i, 0, 0)),
                  pl.BlockSpec((ts, ts), lambda i: (0, 0))],
        out_specs=[pl.BlockSpec((1, 1, ts), lambda i: (i, 0, 0)),
                   pl.BlockSpec((BUCKET_ROWS, LANES), lambda i: (0, 0))],
        out_shape=[jax.ShapeDtypeStruct((n // ts, 1, ts), I32),
                   jax.ShapeDtypeStruct((BUCKET_ROWS, LANES), I32)],
        scratch_shapes=[pltpu.VMEM((BUCKET_ROWS, LANES), F32)],
        compiler_params=_cp("arbitrary"),
        name="bucket_rank",
    )(b3, upper)
    return rank.reshape(n), counts[:N_BUCKETS, 0]


def _sort_plan(bucket, rank, counts, n):
    tiles = (counts + MOE_T - 1) // MOE_T
    tile_end = jnp.cumsum(tiles)
    offsets = (tile_end - tiles) * MOE_T
    dest = offsets[bucket] + rank
    n_tiles = n // MOE_T + N_BUCKETS
    tile_ids = jnp.arange(n_tiles, dtype=I32)
    total = tile_end[-1]
    valid = tile_ids < total
    tb = jnp.searchsorted(tile_end, jnp.minimum(tile_ids, total - 1), side="right").astype(I32)
    tb = jnp.minimum(tb, N_BUCKETS - 1)
    pair = jnp.array(PAIRS, dtype=I32)
    e_lo = (tb // len(PAIRS)) * EXPERTS_PER_GROUP + pair[tb % len(PAIRS), 0]
    e_hi = (tb // len(PAIRS)) * EXPERTS_PER_GROUP + pair[tb % len(PAIRS), 1]
    return dest.astype(I32), tb, e_lo, e_hi, valid.astype(I32)


def _scatter_rows_kernel(dest_ref, x_ref, xs_in_ref, xs_ref, sem):
    del xs_in_ref
    tm = x_ref.shape[0]

    def row_copy(s):
        return pltpu.make_async_copy(x_ref.at[pl.ds(s, 1)], xs_ref.at[pl.ds(dest_ref[0, 0, s], 1)], sem)

    def start(s, carry):
        row_copy(s).start()
        return carry

    def wait(s, carry):
        row_copy(s).wait()
        return carry

    lax.fori_loop(0, tm, start, 0)
    lax.fori_loop(0, tm, wait, 0)


def _scatter_rows(x1, dest, n_slots):
    n = x1.shape[0]
    tm = min(ROW_TM, n)
    xs0 = jnp.zeros((n_slots, D_MODEL), F32)
    return pl.pallas_call(
        _scatter_rows_kernel,
        grid=(n // tm,),
        in_specs=[pl.BlockSpec((1, 1, tm), lambda i: (i, 0, 0), memory_space=pltpu.SMEM),
                  pl.BlockSpec((tm, D_MODEL), lambda i: (i, 0)),
                  pl.BlockSpec(memory_space=pl.ANY)],
        out_specs=pl.BlockSpec(memory_space=pl.ANY),
        out_shape=jax.ShapeDtypeStruct((n_slots, D_MODEL), F32),
        scratch_shapes=[pltpu.SemaphoreType.DMA(())],
        input_output_aliases={2: 0},
        compiler_params=_cp("arbitrary"),
        name="scatter_rows",
    )(dest.reshape(n // tm, 1, tm), x1, xs0)


def _gather_rows_kernel(dest_ref, ys_ref, x_ref, xb_ref, sem):
    tm = x_ref.shape[0]

    def row_copy(s):
        return pltpu.make_async_copy(ys_ref.at[pl.ds(dest_ref[0, 0, s], 1)], x_ref.at[pl.ds(s, 1)], sem)

    def start(s, carry):
        row_copy(s).start()
        return carry

    def wait(s, carry):
        row_copy(s).wait()
        return carry

    lax.fori_loop(0, tm, start, 0)
    lax.fori_loop(0, tm, wait, 0)
    xb_ref[...] = x_ref[...].astype(BF16)


def _gather_rows(ys, dest, n):
    tm = min(ROW_TM, n)
    return pl.pallas_call(
        _gather_rows_kernel,
        grid=(n // tm,),
        in_specs=[pl.BlockSpec((1, 1, tm), lambda i: (i, 0, 0), memory_space=pltpu.SMEM),
                  pl.BlockSpec(memory_space=pl.ANY)],
        out_specs=[pl.BlockSpec((tm, D_MODEL), lambda i: (i, 0)),
                   pl.BlockSpec((tm, D_MODEL), lambda i: (i, 0))],
        out_shape=[jax.ShapeDtypeStruct((n, D_MODEL), F32),
                   jax.ShapeDtypeStruct((n, D_MODEL), BF16)],
        scratch_shapes=[pltpu.SemaphoreType.DMA(())],
        compiler_params=_cp("arbitrary"),
        name="gather_rows",
    )(dest.reshape(n // tm, 1, tm), ys)


def _moe_kernel(tb_ref, elo_ref, ehi_ref, valid_ref, xs_ref, wr2_ref, br2_ref,
                wg_lo_ref, wu_lo_ref, wd_lo_ref, wg_hi_ref, wu_hi_ref, wd_hi_ref, lng_ref, lnb_ref, o_ref):
    del tb_ref, elo_ref, ehi_ref
    t = pl.program_id(0)

    @pl.when(valid_ref[t] == 0)
    def _():
        o_ref[...] = jnp.zeros_like(o_ref)

    @pl.when(valid_ref[t] != 0)
    def _():
        x = xs_ref[...]
        xb = x.astype(BF16)
        l2 = jnp.dot(xb, wr2_ref[0], preferred_element_type=F32) + br2_ref[0]
        g_lo = jax.nn.sigmoid(l2[:, 0:1] - l2[:, 1:2])
        g_hi = jax.nn.sigmoid(l2[:, 1:2] - l2[:, 0:1])

        def expert(wg_ref, wu_ref, wd_ref):
            a = jnp.dot(xb, wg_ref[0], preferred_element_type=F32)
            u = jnp.dot(xb, wu_ref[0], preferred_element_type=F32)
            hidden = (a * jax.nn.sigmoid(a) * u).astype(BF16)
            return jnp.dot(hidden, wd_ref[0], preferred_element_type=F32)

        y = g_lo * expert(wg_lo_ref, wu_lo_ref, wd_lo_ref) + g_hi * expert(wg_hi_ref, wu_hi_ref, wd_hi_ref)
        o_ref[...] = _layer_norm(DN_ALPHA * x + y, lng_ref[...], lnb_ref[...])


def _moe(xs, plan, wr2, br2, wg, wu, wd, lng, lnb):
    _, tb, e_lo, e_hi, valid = plan
    n_slots = xs.shape[0]
    n_tiles = n_slots // MOE_T
    lo = lambda t, tb, elo, ehi, v: (elo[t], 0, 0)
    hi = lambda t, tb, elo, ehi, v: (ehi[t], 0, 0)
    const = lambda t, tb, elo, ehi, v: (0, 0)
    wblk = (1, D_MODEL, D_FF)
    grid_spec = pltpu.PrefetchScalarGridSpec(
        num_scalar_prefetch=4,
        grid=(n_tiles,),
        in_specs=[pl.BlockSpec((MOE_T, D_MODEL), lambda t, tb, elo, ehi, v: (t, 0)),
                  pl.BlockSpec((1, D_MODEL, LANES), lambda t, tb, elo, ehi, v: (tb[t], 0, 0)),
                  pl.BlockSpec((1, 1, LANES), lambda t, tb, elo, ehi, v: (tb[t], 0, 0)),
                  pl.BlockSpec(wblk, lo), pl.BlockSpec(wblk, lo), pl.BlockSpec((1, D_FF, D_MODEL), lo),
                  pl.BlockSpec(wblk, hi), pl.BlockSpec(wblk, hi), pl.BlockSpec((1, D_FF, D_MODEL), hi),
                  pl.BlockSpec((1, D_MODEL), const), pl.BlockSpec((1, D_MODEL), const)],
        out_specs=pl.BlockSpec((MOE_T, D_MODEL), lambda t, tb, elo, ehi, v: (t, 0)),
    )
    return pl.pallas_call(
        _moe_kernel,
        grid_spec=grid_spec,
        out_shape=jax.ShapeDtypeStruct((n_slots, D_MODEL), F32),
        compiler_params=_cp("arbitrary"),
        name="moe",
    )(tb, e_lo, e_hi, valid, xs, wr2, br2, wg, wu, wd, wg, wu, wd, lng, lnb)


def _split_w_in(w_in_l):
    sizes = (GLA_QK_W, GLA_QK_W, GLA_V_W, GLA_V_W, GLA_RANK,
             N_DIL * DIL_GROUP_W, N_DIL * DIL_GROUP_W, N_DIL * DIL_GROUP_W, 2 * D_MODEL)
    starts = [sum(sizes[:i]) for i in range(len(sizes))]
    blk = [w_in_l[:, s:s + z] for s, z in zip(starts, sizes)]
    q_g, k_g, v_g, r_g, a_g, q_d, k_d, v_d, gates = blk
    w_main = jnp.concatenate([q_g, k_g, v_g, r_g, gates], axis=1).astype(BF16)
    w_a = jnp.pad(a_g, ((0, 0), (0, LANES - GLA_RANK))).astype(BF16)
    w_dil = [jnp.concatenate([m[:, g * DIL_GROUP_W:(g + 1) * DIL_GROUP_W] for m in (q_d, k_d, v_d)],
                             axis=1).astype(BF16) for g in range(N_DIL)]
    return w_main, w_a, w_dil


def _router_pair_tables(w_router, b_router):
    cols_w, cols_b = [], []
    for g in range(N_GROUPS):
        for ea, eb_ in PAIRS:
            lo, hi = EXPERTS_PER_GROUP * g + ea, EXPERTS_PER_GROUP * g + eb_
            w2 = jnp.stack([w_router[:, lo], w_router[:, hi]], axis=1)
            cols_w.append(jnp.pad(w2, ((0, 0), (0, LANES - 2))))
            cols_b.append(jnp.pad(jnp.stack([b_router[lo], b_router[hi]]), (0, LANES - 2)))
    return jnp.stack(cols_w).astype(BF16), jnp.stack(cols_b).reshape(N_BUCKETS, 1, LANES).astype(F32)


def kernel(x, positions, w_in, w_alpha_up, b_alpha, gla_norm, w_gla_o, w_att_o, w_out, ln1_g, ln1_b,
           w_router, b_router, w_e_gate, w_e_up, w_e_down, ln2_g, ln2_b):
    batch, seq, d_model = x.shape
    depth = w_in.shape[0]
    n = batch * seq
    assert d_model == D_MODEL and seq % (DIL_PATTERN[-1][1] * DIL_STEPS) == 0

    cos_t, sin_t = _rope_tables(positions)
    tri, ups = _gla_masks()
    head_of_col = jnp.arange(DIL_GROUP_W) // DIL_HD
    expand = (jnp.arange(LANES)[:, None] == head_of_col[None, :]).astype(BF16)
    wrt = w_router.T.astype(BF16)
    br = jnp.broadcast_to(b_router.astype(F32)[:, None], (N_EXPERTS, LANES))
    wr2, br2 = _router_pair_tables(w_router, b_router)
    n_slots = n + N_BUCKETS * MOE_T

    xf = x.reshape(n, D_MODEL)
    xin = xf
    for i in range(depth):
        w_main, w_a, w_dil = _split_w_in(w_in[i])
        h_main = _proj(xin, w_main, BF16, PROJ_TN)
        a_proj = _proj(xin, w_a, F32, LANES)
        wup = jnp.pad(w_alpha_up[i], ((0, LANES - GLA_RANK), (0, 0))).astype(BF16)
        og = _gla(h_main, a_proj, wup, b_alpha[i].reshape(1, -1).astype(F32),
                  gla_norm[i].reshape(1, -1).astype(F32), tri, ups, batch, seq)
        attn = []
        for g, (_, dil) in enumerate(DIL_PATTERN):
            qkv = _proj_dil(xin, w_dil[g], cos_t, sin_t, batch, seq, dil)
            attn.append(_dil_attn(qkv, batch, seq, dil))
        x1, bucket = _mix(og.reshape(n, GLA_V_W), h_main, attn, xf,
                          w_gla_o[i].astype(BF16), w_att_o[i].astype(BF16), w_out[i].astype(BF16),
                          ln1_g[i].reshape(1, -1), ln1_b[i].reshape(1, -1), wrt, br, expand, batch, seq)
        bucket = bucket.reshape(n)
        rank, counts = _bucket_ranks(bucket, n)
        plan = _sort_plan(bucket, rank, counts, n)
        xs = _scatter_rows(x1, plan[0], n_slots)
        ys = _moe(xs, plan, wr2, br2, w_e_gate[i].astype(BF16), w_e_up[i].astype(BF16),
                  w_e_down[i].astype(BF16), ln2_g[i].reshape(1, -1), ln2_b[i].reshape(1, -1))
        xf, xin = _gather_rows(ys, plan[0], n)
    return xf.reshape(batch, seq, D_MODEL)
```

```python
import functools

import jax
import jax.numpy as jnp
from jax import lax
from jax.experimental import pallas as pl
from jax.experimental.pallas import tpu as pltpu

F32 = jnp.float32
BF16 = jnp.bfloat16
I32 = jnp.int32

D_MODEL = 1024
N_LAYERS = 4
GLA_HEADS = 4
GLA_DK = 128
GLA_DV = 256
GLA_RANK = 16
GLA_TAU = 16.0
GLA_CHUNK = 64
DIL_PATTERN = ((128, 1), (512, 4), (2048, 16))
DIL_HEADS = 8
DIL_HD = 64
DIL_STEPS = 128
ROPE_THETA = 10000.0
N_EXPERTS = 16
N_GROUPS = 4
EXPERTS_PER_GROUP = 4
D_FF = 1024
LN_EPS = 1e-5
RMS_EPS = 1e-6
DN_ALPHA = (2 * N_LAYERS) ** 0.25
NEG_INF = -1e30

GLA_QK_W = GLA_HEADS * GLA_DK
GLA_V_W = GLA_HEADS * GLA_DV
DIL_GROUP_W = DIL_HEADS * DIL_HD
N_DIL = len(DIL_PATTERN)
MAIN_W = 2 * GLA_QK_W + 2 * GLA_V_W + 2 * D_MODEL
GATE_COL = 2 * GLA_QK_W + 2 * GLA_V_W

LANES = 128
PAIRS = ((0, 1), (0, 2), (0, 3), (1, 2), (1, 3), (2, 3))
N_BUCKETS = N_GROUPS * len(PAIRS)
BUCKET_ROWS = 32

PROJ_TM = 1024
PROJ_TN = 1024
GLA_T = 512
MIX_TM = 512
SORT_TS = 1024
MOE_T = 256
ATTN_TQ = 512
VMEM_LIMIT = 56 * 1024 * 1024


def _cp(*sem):
    return pltpu.CompilerParams(dimension_semantics=sem, vmem_limit_bytes=VMEM_LIMIT)


def _rope_kernel(pos_ref, invf_ref, cos_ref, sin_ref):
    ang = pos_ref[...].astype(F32) * invf_ref[...]
    cos_ref[...] = jnp.cos(ang)
    sin_ref[...] = jnp.sin(ang)


def _rope_tables(positions):
    n = positions.size
    tm = 2048
    half = DIL_HD // 2
    inv_freq = jnp.power(jnp.float32(ROPE_THETA), -jnp.arange(0, DIL_HD, 2, dtype=F32) / DIL_HD)
    invf = jnp.tile(inv_freq, LANES // half).reshape(1, LANES)
    return pl.pallas_call(
        _rope_kernel,
        grid=(n // tm,),
        in_specs=[pl.BlockSpec((tm, 1), lambda i: (i, 0)),
                  pl.BlockSpec((1, LANES), lambda i: (0, 0))],
        out_specs=[pl.BlockSpec((tm, LANES), lambda i: (i, 0)),
                   pl.BlockSpec((tm, LANES), lambda i: (i, 0))],
        out_shape=[jax.ShapeDtypeStruct((n, LANES), F32)] * 2,
        compiler_params=_cp("parallel"),
        name="rope_tables",
    )(positions.reshape(n, 1), invf)


def _proj_kernel(x_ref, w_ref, o_ref, xb_ref):
    @pl.when(pl.program_id(1) == 0)
    def _():
        xb_ref[...] = x_ref[...].astype(BF16)

    o_ref[...] = jnp.dot(xb_ref[...], w_ref[...], preferred_element_type=F32).astype(o_ref.dtype)


def _proj(x, w, out_dtype, tn, n):
    k = x.shape[1]
    ncol = w.shape[1]
    tm = min(PROJ_TM, n)
    return pl.pallas_call(
        _proj_kernel,
        grid=(n // tm, ncol // tn),
        in_specs=[pl.BlockSpec((tm, k), lambda i, j: (i, 0)),
                  pl.BlockSpec((k, tn), lambda i, j: (0, j))],
        out_specs=pl.BlockSpec((tm, tn), lambda i, j: (i, j)),
        out_shape=jax.ShapeDtypeStruct((n, ncol), out_dtype),
        scratch_shapes=[pltpu.VMEM((tm, k), BF16)],
        compiler_params=_cp("parallel", "arbitrary"),
        name="proj",
    )(x, w)


def _proj_dil_kernel(x_ref, w_ref, cos_ref, sin_ref, o_ref, xb_ref, acc_ref, *, dil, tm):
    j = pl.program_id(1)

    @pl.when(j == 0)
    def _():
        xb_ref[...] = x_ref[...].astype(BF16)

    acc = jnp.dot(xb_ref[...], w_ref[...], preferred_element_type=F32)

    def store_chunks(val):
        for c in range(DIL_GROUP_W // LANES):
            acc_ref[c] = val[:, c * LANES:(c + 1) * LANES]

    @pl.when(j < 2)
    def _():
        scale = jnp.where(j == 0, DIL_HD ** -0.5, 1.0)
        cos_t = cos_ref[...] * scale
        sin_t = sin_ref[...] * scale
        for quad in range(DIL_GROUP_W // (2 * LANES)):
            x1 = acc[:, 2 * quad * LANES:(2 * quad + 1) * LANES]
            x2 = acc[:, (2 * quad + 1) * LANES:(2 * quad + 2) * LANES]
            acc_ref[2 * quad] = x1 * cos_t - x2 * sin_t
            acc_ref[2 * quad + 1] = x2 * cos_t + x1 * sin_t

    @pl.when(j == 2)
    def _():
        store_chunks(acc)

    rows = tm // dil
    for r in range(dil):
        for c in range(DIL_GROUP_W // LANES):
            cs = slice(c * LANES, (c + 1) * LANES)
            if dil == 1:
                o_ref[0, 0, r, :, cs] = acc_ref[c].astype(BF16)
            else:
                o_ref[0, 0, r, :, cs] = acc_ref[c, pl.ds(r, rows, stride=dil), :].astype(BF16)


def _proj_dil(x, w, cos_t, sin_t, batch, seq, dil):
    n, k = batch * seq, x.shape[1]
    tm = min(PROJ_TM, seq)
    tiles_per_seq = seq // tm
    sub = seq // dil
    kern = functools.partial(_proj_dil_kernel, dil=dil, tm=tm)
    return pl.pallas_call(
        kern,
        grid=(n // tm, 3),
        in_specs=[pl.BlockSpec((tm, k), lambda i, j: (i, 0)),
                  pl.BlockSpec((k, DIL_GROUP_W), lambda i, j: (0, j)),
                  pl.BlockSpec((tm, LANES), lambda i, j: (i, 0)),
                  pl.BlockSpec((tm, LANES), lambda i, j: (i, 0))],
        out_specs=pl.BlockSpec((1, 1, dil, tm // dil, DIL_GROUP_W),
                               lambda i, j: (j, i // tiles_per_seq, 0, i % tiles_per_seq, 0)),
        out_shape=jax.ShapeDtypeStruct((3, batch, dil, sub, DIL_GROUP_W), BF16),
        scratch_shapes=[pltpu.VMEM((tm, k), BF16), pltpu.VMEM((DIL_GROUP_W // LANES, tm, LANES), F32)],
        compiler_params=_cp("parallel", "arbitrary"),
        name=f"proj_dil{dil}",
    )(x, w, cos_t, sin_t)


def _split_dot(mat_bf16, val_f32):
    hi = val_f32.astype(BF16)
    lo = (val_f32 - hi.astype(F32)).astype(BF16)
    return (jnp.dot(mat_bf16, hi, preferred_element_type=F32)
            + jnp.dot(mat_bf16, lo, preferred_element_type=F32))


def _gla_kernel(q_ref, k_ref, v_ref, r_ref, a_ref, wup_ref, balpha_ref, gnorm_ref, tri_ref, ups_ref,
                o_ref, state_ref, sprev_ref):
    t = pl.program_id(1)
    n_chunks = GLA_T // GLA_CHUNK

    @pl.when(t == 0)
    def _():
        state_ref[...] = jnp.zeros_like(state_ref)

    z = jnp.dot(a_ref[0].astype(BF16), wup_ref[...], preferred_element_type=F32) + balpha_ref[...]
    log_a = (jnp.minimum(z, 0.0) - jnp.log1p(jnp.exp(-jnp.abs(z)))) / GLA_TAU
    tri = tri_ref[...]
    b = _split_dot(tri, log_a)
    c = _split_dot(ups_ref[...], log_a)
    eb = jnp.exp(b)
    kf = k_ref[0].astype(F32)
    qt = (q_ref[0].astype(F32) * eb * (GLA_DK ** -0.5)).astype(BF16)
    kt = (kf * jnp.exp(-b)).astype(BF16)
    ks = (kf * jnp.exp(c)).astype(BF16)
    causal = tri > 0

    for h in range(GLA_HEADS):
        ck = slice(h * GLA_DK, (h + 1) * GLA_DK)
        cv = slice(h * GLA_DV, (h + 1) * GLA_DV)
        qt_h, kt_h, ks_h = qt[:, ck], kt[:, ck], ks[:, ck]
        v_h = v_ref[0, :, cv]
        att = lax.dot_general(qt_h, kt_h, (((1,), (1,)), ((), ())), preferred_element_type=F32)
        att = jnp.where(causal, att, 0.0).astype(BF16)
        o_h = jnp.dot(att, v_h, preferred_element_type=F32)

        state = state_ref[h]
        for n in range(n_chunks):
            rs = slice(n * GLA_CHUNK, (n + 1) * GLA_CHUNK)
            contrib = lax.dot_general(v_h[rs], ks_h[rs], (((0,), (0,)), ((), ())),
                                      preferred_element_type=F32)
            sprev_ref[h, n] = state.astype(BF16)
            decay = eb[(n + 1) * GLA_CHUNK - 1:(n + 1) * GLA_CHUNK, ck]
            state = state * decay + contrib
        state_ref[h] = state

        inter = [lax.dot_general(qt_h[n * GLA_CHUNK:(n + 1) * GLA_CHUNK], sprev_ref[h, n],
                                 (((1,), (1,)), ((), ())), preferred_element_type=F32)
                 for n in range(n_chunks)]
        o_h = o_h + jnp.concatenate(inter, axis=0)

        o_h = o_h * lax.rsqrt(jnp.mean(jnp.square(o_h), axis=-1, keepdims=True) + RMS_EPS)
        o_h = o_h * gnorm_ref[:, cv]
        r_h = r_ref[0, :, cv].astype(F32)
        o_ref[0, :, cv] = (o_h * (r_h * jax.nn.sigmoid(r_h))).astype(o_ref.dtype)


def _gla(h_main, a_proj, wup, balpha, gnorm, tri, ups, batch, seq):
    h3 = h_main.reshape(batch, seq, MAIN_W)
    a3 = a_proj.reshape(batch, seq, LANES)
    n_chunks = GLA_T // GLA_CHUNK
    return pl.pallas_call(
        _gla_kernel,
        grid=(batch, seq // GLA_T),
        in_specs=[pl.BlockSpec((1, GLA_T, GLA_QK_W), lambda b, t: (b, t, 0)),
                  pl.BlockSpec((1, GLA_T, GLA_QK_W), lambda b, t: (b, t, 1)),
                  pl.BlockSpec((1, GLA_T, GLA_V_W), lambda b, t: (b, t, 1)),
                  pl.BlockSpec((1, GLA_T, GLA_V_W), lambda b, t: (b, t, 2)),
                  pl.BlockSpec((1, GLA_T, LANES), lambda b, t: (b, t, 0)),
                  pl.BlockSpec((LANES, GLA_QK_W), lambda b, t: (0, 0)),
                  pl.BlockSpec((1, GLA_QK_W), lambda b, t: (0, 0)),
                  pl.BlockSpec((1, GLA_V_W), lambda b, t: (0, 0)),
                  pl.BlockSpec((GLA_T, GLA_T), lambda b, t: (0, 0)),
                  pl.BlockSpec((GLA_T, GLA_T), lambda b, t: (0, 0))],
        out_specs=pl.BlockSpec((1, GLA_T, GLA_V_W), lambda b, t: (b, t, 0)),
        out_shape=jax.ShapeDtypeStruct((batch, seq, GLA_V_W), BF16),
        scratch_shapes=[pltpu.VMEM((GLA_HEADS, GLA_DV, GLA_DK), F32),
                        pltpu.VMEM((GLA_HEADS, n_chunks, GLA_DV, GLA_DK), BF16)],
        compiler_params=_cp("parallel", "arbitrary"),
        name="gla",
    )(h3, h3, h3, h3, a3, wup, balpha, gnorm, tri, ups)


def _gla_masks():
    i = jnp.arange(GLA_T)[:, None]
    j = jnp.arange(GLA_T)[None, :]
    same = (i // GLA_CHUNK) == (j // GLA_CHUNK)
    tri = (same & (j <= i)).astype(BF16)
    ups = (same & (j > i)).astype(BF16)
    return tri, ups


def _dil_attn_kernel(q_ref, kc_ref, kp_ref, vc_ref, vp_ref, o_ref, lse_ref, *, tq):
    n = pl.program_id(2)
    st = DIL_STEPS
    qi = lax.broadcasted_iota(I32, (st, 2 * st), 0)
    kj = lax.broadcasted_iota(I32, (st, 2 * st), 1)
    in_prev = kj < st
    band_bias = jnp.where(jnp.where(in_prev, kj - qi, qi + st - kj) >= 0, 0.0, NEG_INF)
    prev_cols = jnp.where(in_prev, 1.0, 0.0)
    lane = lax.broadcasted_iota(I32, (st, LANES), 1)
    low = lane < DIL_HD
    ones = jnp.ones((2 * st, LANES), BF16)
    nt = (((1,), (1,)), ((), ()))

    no_prev = jnp.where(n == 0, NEG_INF, 0.0)
    first_bias = band_bias + prev_cols * no_prev

    def window(cur_ref, prev_ref, i, cs):
        if i == 0:
            return jnp.concatenate([prev_ref[0, 0, 0, :, cs], cur_ref[0, 0, 0, 0:st, cs]], axis=0)
        return cur_ref[0, 0, 0, (i - 1) * st:(i + 1) * st, cs]

    quad_w = 2 * LANES
    quad_lane = lax.broadcasted_iota(I32, (st, quad_w), 1)
    head_in_quad = (quad_lane % LANES) // (DIL_HD // 2)

    for i in range(tq // st):
        r0 = i * st
        bias = first_bias if i == 0 else band_bias
        bias4 = jnp.concatenate([bias] * 4, axis=0)
        stats = jnp.zeros((st, LANES), F32)
        outs = []
        for quad in range(DIL_HEADS // 4):
            qs = slice(quad * quad_w, (quad + 1) * quad_w)
            qq = q_ref[0, 0, 0, r0:r0 + st, qs]
            zero = jnp.zeros_like(qq)
            lhs = jnp.concatenate([jnp.where(head_in_quad == u, qq, zero) for u in range(4)], axis=0)
            s = lax.dot_general(lhs, window(kc_ref, kp_ref, i, qs), nt,
                                preferred_element_type=F32) + bias4
            m = jnp.max(s, axis=-1, keepdims=True)
            pexp = jnp.exp(s - m).astype(BF16)
            m_rep = jnp.broadcast_to(m, (4 * st, LANES))
            for pair in range(2):
                p = 2 * quad + pair
                cs = slice(p * LANES, (p + 1) * LANES)
                v_ext = jnp.concatenate([window(vc_ref, vp_ref, i, cs), ones], axis=1)
                o = jnp.dot(pexp[2 * pair * st:(2 * pair + 2) * st], v_ext,
                            preferred_element_type=F32)
                outs.append(jnp.where(low, o[0:st, 0:LANES], o[st:2 * st, 0:LANES]))
                l_rep = o[:, LANES:2 * LANES]
                for half in range(2):
                    rs = slice(half * st, (half + 1) * st)
                    ms = slice((2 * pair + half) * st, (2 * pair + half + 1) * st)
                    stats = jnp.where(lane == 2 * p + half, m_rep[ms], stats)
                    stats = jnp.where(lane == DIL_HEADS + 2 * p + half, l_rep[rs], stats)
        o_ref[0, 0, r0:r0 + st, :] = jnp.concatenate(outs, axis=1).astype(o_ref.dtype)
        lse_ref[0, 0, r0:r0 + st, :] = stats


def _dil_attn(qkv, batch, seq, dil):
    sub = seq // dil
    tq = min(ATTN_TQ, sub)
    per = tq // DIL_STEPS

    def cur(which):
        return pl.BlockSpec((1, 1, 1, tq, DIL_GROUP_W), lambda b, r, n: (which, b, r, n, 0))

    def prev(which):
        return pl.BlockSpec((1, 1, 1, DIL_STEPS, DIL_GROUP_W),
                            lambda b, r, n: (which, b, r, jnp.maximum(n * per - 1, 0), 0))

    return pl.pallas_call(
        functools.partial(_dil_attn_kernel, tq=tq),
        grid=(batch, dil, sub // tq),
        in_specs=[cur(0), cur(1), prev(1), cur(2), prev(2)],
        out_specs=[pl.BlockSpec((1, 1, tq, DIL_GROUP_W), lambda b, r, n: (b, r, n, 0)),
                   pl.BlockSpec((1, 1, tq, LANES), lambda b, r, n: (b, r, n, 0))],
        out_shape=[jax.ShapeDtypeStruct((batch, dil, sub, DIL_GROUP_W), BF16),
                   jax.ShapeDtypeStruct((batch, dil, sub, LANES), F32)],
        compiler_params=_cp("parallel", "parallel", "arbitrary"),
        name=f"dil_attn{dil}",
    )(qkv, qkv, qkv, qkv, qkv)


def _layer_norm(x, g, b):
    mu = jnp.mean(x, axis=-1, keepdims=True)
    var = jnp.mean(jnp.square(x - mu), axis=-1, keepdims=True)
    return (x - mu) * lax.rsqrt(var + LN_EPS) * g + b


def _mix_kernel(og_ref, g1_ref, g2_ref, o0_ref, o1_ref, o2_ref, l0_ref, l1_ref, l2_ref, x_ref,
                wgo_ref, wao_ref, wout_ref, lng_ref, lnb_ref, wrt_ref, br_ref, expand_ref,
                x1_ref, bucket_ref, osc_ref, lsc_ref):
    tm = x_ref.shape[0]

    def interleaved(o_ref, l_ref, dil, slot):
        if dil == 1:
            return o_ref[0, 0].astype(F32), l_ref[0, 0]
        rows = tm // dil
        chunks = DIL_GROUP_W // LANES
        for r in range(dil):
            for c in range(chunks):
                osc_ref[slot, c, pl.ds(r, rows, stride=dil), :] = (
                    o_ref[0, r, :, c * LANES:(c + 1) * LANES].astype(F32))
            lsc_ref[slot, pl.ds(r, rows, stride=dil), :] = l_ref[0, r]
        return jnp.concatenate([osc_ref[slot, c] for c in range(chunks)], axis=1), lsc_ref[slot]

    groups = [interleaved(o_ref, l_ref, dil, slot)
              for slot, (o_ref, l_ref, (_, dil)) in enumerate(
                  zip((o0_ref, o1_ref, o2_ref), (l0_ref, l1_ref, l2_ref), DIL_PATTERN))]
    stats = [g[1] for g in groups]
    m_max = jnp.maximum(jnp.maximum(stats[0], stats[1]), stats[2])
    es = [jnp.exp(s - m_max) for s in stats]
    ls = [pltpu.roll(s, LANES - DIL_HEADS, 1) for s in stats]
    inv = 1.0 / (es[0] * ls[0] + es[1] * ls[1] + es[2] * ls[2])
    head_lane = lax.broadcasted_iota(I32, (tm, LANES), 1) < DIL_HEADS
    att = jnp.zeros((tm, DIL_GROUP_W), F32)
    for (num_g, _), e in zip(groups, es):
        coef = jnp.where(head_lane, e * inv, 0.0)
        att = att + _split_dot_rhs(coef, expand_ref[...]) * num_g

    y_att = jnp.dot(att.astype(BF16), wao_ref[...], preferred_element_type=F32)
    y_gla = jnp.dot(og_ref[...], wgo_ref[...], preferred_element_type=F32)
    mixed = (jax.nn.sigmoid(g1_ref[...].astype(F32)) * y_gla
             + jax.nn.sigmoid(g2_ref[...].astype(F32)) * y_att)
    mix = jnp.dot(mixed.astype(BF16), wout_ref[...], preferred_element_type=F32)
    x1 = _layer_norm(DN_ALPHA * x_ref[...] + mix, lng_ref[...], lnb_ref[...])
    x1_ref[...] = x1

    logits = lax.dot_general(wrt_ref[...], x1.astype(BF16), (((1,), (1,)), ((), ())),
                             preferred_element_type=F32) + br_ref[:, 0:1]
    e = jnp.exp(logits - jnp.max(logits, axis=0, keepdims=True))
    probs = e / jnp.sum(e, axis=0, keepdims=True)
    best = None
    for g in range(N_GROUPS):
        for pi, (ea, eb_) in enumerate(PAIRS):
            cand = probs[4 * g + ea:4 * g + ea + 1] + probs[4 * g + eb_:4 * g + eb_ + 1]
            idx = g * len(PAIRS) + pi
            if best is None:
                best, bidx = cand, jnp.zeros(cand.shape, I32)
            else:
                better = cand > best
                best = jnp.where(better, cand, best)
                bidx = jnp.where(better, idx, bidx)
    bucket_ref[0] = bidx


def _split_dot_rhs(val_f32, mat_bf16):
    hi = val_f32.astype(BF16)
    lo = (val_f32 - hi.astype(F32)).astype(BF16)
    return (jnp.dot(hi, mat_bf16, preferred_element_type=F32)
            + jnp.dot(lo, mat_bf16, preferred_element_type=F32))


def _mix(og, h_main, attn, x, wgo, wao, wout, lng, lnb, wrt, br, expand, batch, seq):
    n = batch * seq
    tm = min(MIX_TM, seq)
    tps = seq // tm
    row = lambda i: (i, 0)
    const = lambda i: (0, 0)
    o_specs, l_specs, o_args, l_args = [], [], [], []
    for (o_g, lse_g), (_, dil) in zip(attn, DIL_PATTERN):
        o_specs.append(pl.BlockSpec((1, dil, tm // dil, DIL_GROUP_W), lambda i: (i // tps, 0, i % tps, 0)))
        l_specs.append(pl.BlockSpec((1, dil, tm // dil, LANES), lambda i: (i // tps, 0, i % tps, 0)))
        o_args.append(o_g)
        l_args.append(lse_g)
    gate_blk = GATE_COL // D_MODEL
    return pl.pallas_call(
        _mix_kernel,
        grid=(n // tm,),
        in_specs=[pl.BlockSpec((tm, GLA_V_W), row),
                  pl.BlockSpec((tm, D_MODEL), lambda i: (i, gate_blk)),
                  pl.BlockSpec((tm, D_MODEL), lambda i: (i, gate_blk + 1)),
                  *o_specs, *l_specs,
                  pl.BlockSpec((tm, D_MODEL), row),
                  pl.BlockSpec((GLA_V_W, D_MODEL), const),
                  pl.BlockSpec((DIL_GROUP_W, D_MODEL), const),
                  pl.BlockSpec((D_MODEL, D_MODEL), const),
                  pl.BlockSpec((1, D_MODEL), const),
                  pl.BlockSpec((1, D_MODEL), const),
                  pl.BlockSpec((N_EXPERTS, D_MODEL), const),
                  pl.BlockSpec((N_EXPERTS, LANES), const),
                  pl.BlockSpec((LANES, DIL_GROUP_W), const)],
        out_specs=[pl.BlockSpec((tm, D_MODEL), row),
                   pl.BlockSpec((1, 1, tm), lambda i: (i, 0, 0))],
        out_shape=[jax.ShapeDtypeStruct((n, D_MODEL), F32),
                   jax.ShapeDtypeStruct((n // tm, 1, tm), I32)],
        scratch_shapes=[pltpu.VMEM((N_DIL, DIL_GROUP_W // LANES, tm, LANES), F32),
                        pltpu.VMEM((N_DIL, tm, LANES), F32)],
        compiler_params=_cp("parallel"),
        name="mix",
    )(og, h_main, h_main, *o_args, *l_args, x, wgo, wao, wout, lng, lnb, wrt, br, expand)


def _rank_kernel(bucket_ref, upper_ref, rank_ref, counts_ref, carry_ref):
    @pl.when(pl.program_id(0) == 0)
    def _():
        carry_ref[...] = jnp.zeros_like(carry_ref)

    b = bucket_ref[0]
    ts = b.shape[1]
    onehot = lax.broadcasted_iota(I32, (BUCKET_ROWS, ts), 0) == b
    oh_f = jnp.where(onehot, 1.0, 0.0)
    before = jnp.dot(oh_f.astype(BF16), upper_ref[...], preferred_element_type=F32)
    carry = carry_ref[...]
    rank = jnp.sum(oh_f * (before + carry[:, 0:1]), axis=0, keepdims=True)
    rank_ref[0] = rank.astype(I32)
    carry = carry + jnp.sum(oh_f, axis=1, keepdims=True)
    carry_ref[...] = carry
    counts_ref[...] = carry.astype(I32)


def _bucket_ranks(bucket, n):
    ts = min(SORT_TS, n)
    b3 = bucket.reshape(n // ts, 1, ts)
    j = jnp.arange(ts)
    upper = (j[:, None] < j[None, :]).astype(BF16)
    rank, counts = pl.pallas_call(
        _rank_kernel,
        grid=(n // ts,),
        in_specs=[pl.BlockSpec((1, 1, ts), lambda i: (i, 0, 0)),
                  pl.BlockSpec((ts, ts), lambda i: (0, 0))],
        out_specs=[pl.BlockSpec((1, 1, ts), lambda i: (i, 0, 0)),
                   pl.BlockSpec((BUCKET_ROWS, LANES), lambda i: (0, 0))],
        out_shape=[jax.ShapeDtypeStruct((n // ts, 1, ts), I32),
                   jax.ShapeDtypeStruct((BUCKET_ROWS, LANES), I32)],
        scratch_shapes=[pltpu.VMEM((BUCKET_ROWS, LANES), F32)],
        compiler_params=_cp("arbitrary"),
        name="bucket_rank",
    )(b3, upper)
    return rank.reshape(n), counts[:N_BUCKETS, 0]


def _sort_plan(bucket, rank, counts, n):
    tiles = (counts + MOE_T - 1) // MOE_T
    tile_end = jnp.cumsum(tiles)
    offsets = (tile_end - tiles) * MOE_T
    dest = offsets[bucket] + rank
    n_tiles = n // MOE_T + N_BUCKETS
    tile_ids = jnp.arange(n_tiles, dtype=I32)
    total = tile_end[-1]
    last = jnp.minimum(tile_ids, total - 1)
    tb = jnp.sum((last[:, None] >= tile_end[None, :]).astype(I32), axis=1)
    tb = jnp.minimum(tb, N_BUCKETS - 1)
    n_valid = jnp.clip(counts[tb] - (tile_ids - (tile_end - tiles)[tb]) * MOE_T, 0, MOE_T)
    n_valid = jnp.where(tile_ids < total, n_valid, 0)
    pair = jnp.array(PAIRS, dtype=I32)
    e_lo = (tb // len(PAIRS)) * EXPERTS_PER_GROUP + pair[tb % len(PAIRS), 0]
    e_hi = (tb // len(PAIRS)) * EXPERTS_PER_GROUP + pair[tb % len(PAIRS), 1]
    return dest.astype(I32), tb, e_lo, e_hi, n_valid.astype(I32)


def _invert_kernel(dest_ref, src_ref, table_ref, sem, *, chunk, n_slots):
    i = pl.program_id(0)

    @pl.when(i == 0)
    def _():
        def clear(j, carry):
            table_ref[j] = 0
            return carry
        lax.fori_loop(0, n_slots, clear, 0, unroll=8)

    def put(j, carry):
        table_ref[dest_ref[0, 0, j]] = i * chunk + j
        return carry
    lax.fori_loop(0, chunk, put, 0, unroll=8)

    @pl.when(i == pl.num_programs(0) - 1)
    def _():
        copy = pltpu.make_async_copy(table_ref, src_ref, sem)
        copy.start()
        copy.wait()


def _invert(dest, n_slots):
    n = dest.shape[0]
    chunk = min(2048, n)
    return pl.pallas_call(
        functools.partial(_invert_kernel, chunk=chunk, n_slots=n_slots),
        grid=(n // chunk,),
        in_specs=[pl.BlockSpec((1, 1, chunk), lambda i: (i, 0, 0), memory_space=pltpu.SMEM)],
        out_specs=pl.BlockSpec(memory_space=pl.ANY),
        out_shape=jax.ShapeDtypeStruct((n_slots,), I32),
        scratch_shapes=[pltpu.SMEM((n_slots,), I32), pltpu.SemaphoreType.DMA(())],
        compiler_params=_cp("arbitrary"),
        name="invert_perm",
    )(dest.reshape(n // chunk, 1, chunk))


def _moe_kernel(tb_ref, elo_ref, ehi_ref, nvalid_ref, src_ref, nxt_ref, x_hbm, wr2_ref, br2_ref,
                wg_lo_ref, wu_lo_ref, wd_lo_ref, wg_hi_ref, wu_hi_ref, wd_hi_ref, lng_ref, lnb_ref,
                out_hbm, xbuf_ref, ybuf_ref, gsem, ssem, *, n_tokens):
    del tb_ref, elo_ref, ehi_ref
    t = pl.program_id(0)
    slot = t % 2
    other = 1 - slot
    tile = MOE_T

    def gather_copy(idx_ref, r, sl):
        return pltpu.make_async_copy(x_hbm.at[pl.ds(idx_ref[0, 0, r], 1)],
                                     xbuf_ref.at[sl, pl.ds(r, 1)], gsem.at[sl])

    def scatter_copy(dst, r, sl):
        return pltpu.make_async_copy(ybuf_ref.at[sl, pl.ds(r, 1)], out_hbm.at[pl.ds(dst, 1)], ssem.at[sl])

    def trash_row(r, sl):
        return n_tokens + sl * tile + r

    @pl.when(t == 0)
    def _():
        ybuf_ref[...] = jnp.zeros_like(ybuf_ref)
        for r in range(tile):
            gather_copy(src_ref, r, 0).start()
        for sl in range(2):
            for r in range(tile):
                scatter_copy(trash_row(r, sl), r, sl).start()

    for r in range(tile):
        gather_copy(src_ref, r, slot).wait()

    @pl.when(nvalid_ref[t] == 0)
    def _():
        for r in range(tile):
            gather_copy(nxt_ref, r, other).start()

    @pl.when(nvalid_ref[t] != 0)
    def _():
        for r in range(tile):
            gather_copy(nxt_ref, r, other).start()
        x = xbuf_ref[slot]
        xb = x.astype(BF16)
        l2 = jnp.dot(xb, wr2_ref[0], preferred_element_type=F32) + br2_ref[0]
        g_lo = jax.nn.sigmoid(l2[:, 0:1] - l2[:, 1:2])
        g_hi = jax.nn.sigmoid(l2[:, 1:2] - l2[:, 0:1])

        def expert(wg_ref, wu_ref, wd_ref):
            a = jnp.dot(xb, wg_ref[0], preferred_element_type=F32)
            u = jnp.dot(xb, wu_ref[0], preferred_element_type=F32)
            hidden = (a * jax.nn.sigmoid(a) * u).astype(BF16)
            return jnp.dot(hidden, wd_ref[0], preferred_element_type=F32)

        y = g_lo * expert(wg_lo_ref, wu_lo_ref, wd_lo_ref) + g_hi * expert(wg_hi_ref, wu_hi_ref, wd_hi_ref)
        y = _layer_norm(DN_ALPHA * x + y, lng_ref[...], lnb_ref[...])
        for r in range(tile):
            scatter_copy(0, r, slot).wait()
        ybuf_ref[slot] = y
        n_valid = nvalid_ref[t]
        for r in range(tile):
            dst = jnp.where(r < n_valid, src_ref[0, 0, r], trash_row(r, slot))
            scatter_copy(dst, r, slot).start()

    @pl.when(t == pl.num_programs(0) - 1)
    def _():
        for r in range(tile):
            gather_copy(nxt_ref, r, other).wait()
        for sl in range(2):
            for r in range(tile):
                scatter_copy(0, r, sl).wait()


def _moe(x1, src, plan, wr2, br2, wg, wu, wd, lng, lnb):
    _, tb, e_lo, e_hi, n_valid = plan
    n = x1.shape[0]
    n_tiles = src.shape[0] // MOE_T
    src3 = src.reshape(n_tiles, 1, MOE_T)
    lo = lambda t, tb, elo, ehi, v: (elo[t], 0, 0)
    hi = lambda t, tb, elo, ehi, v: (ehi[t], 0, 0)
    const = lambda t, tb, elo, ehi, v: (0, 0)
    wblk = (1, D_MODEL, D_FF)
    idx_blk = (1, 1, MOE_T)
    grid_spec = pltpu.PrefetchScalarGridSpec(
        num_scalar_prefetch=4,
        grid=(n_tiles,),
        in_specs=[pl.BlockSpec(idx_blk, lambda t, tb, elo, ehi, v: (t, 0, 0), memory_space=pltpu.SMEM),
                  pl.BlockSpec(idx_blk, lambda t, tb, elo, ehi, v: (jnp.minimum(t + 1, n_tiles - 1), 0, 0),
                               memory_space=pltpu.SMEM),
                  pl.BlockSpec(memory_space=pl.ANY),
                  pl.BlockSpec((1, D_MODEL, LANES), lambda t, tb, elo, ehi, v: (tb[t], 0, 0)),
                  pl.BlockSpec((1, 1, LANES), lambda t, tb, elo, ehi, v: (tb[t], 0, 0)),
                  pl.BlockSpec(wblk, lo), pl.BlockSpec(wblk, lo), pl.BlockSpec((1, D_FF, D_MODEL), lo),
                  pl.BlockSpec(wblk, hi), pl.BlockSpec(wblk, hi), pl.BlockSpec((1, D_FF, D_MODEL), hi),
                  pl.BlockSpec((1, D_MODEL), const), pl.BlockSpec((1, D_MODEL), const)],
        out_specs=pl.BlockSpec(memory_space=pl.ANY),
        scratch_shapes=[pltpu.VMEM((2, MOE_T, D_MODEL), F32), pltpu.VMEM((2, MOE_T, D_MODEL), F32),
                        pltpu.SemaphoreType.DMA((2,)), pltpu.SemaphoreType.DMA((2,))],
    )
    return pl.pallas_call(
        functools.partial(_moe_kernel, n_tokens=n),
        grid_spec=grid_spec,
        out_shape=jax.ShapeDtypeStruct((n + 2 * MOE_T, D_MODEL), F32),
        compiler_params=_cp("arbitrary"),
        name="moe",
    )(tb, e_lo, e_hi, n_valid, src3, src3, x1, wr2, br2, wg, wu, wd, wg, wu, wd, lng, lnb)


def _split_w_in(w_in_l):
    sizes = (GLA_QK_W, GLA_QK_W, GLA_V_W, GLA_V_W, GLA_RANK,
             N_DIL * DIL_GROUP_W, N_DIL * DIL_GROUP_W, N_DIL * DIL_GROUP_W, 2 * D_MODEL)
    starts = [sum(sizes[:i]) for i in range(len(sizes))]
    blk = [w_in_l[:, s:s + z] for s, z in zip(starts, sizes)]
    q_g, k_g, v_g, r_g, a_g, q_d, k_d, v_d, gates = blk
    w_main = jnp.concatenate([q_g, k_g, v_g, r_g, gates], axis=1).astype(BF16)
    w_a = jnp.pad(a_g, ((0, 0), (0, LANES - GLA_RANK))).astype(BF16)
    quad = jnp.asarray(_quad_layout())
    w_dil = []
    for g in range(N_DIL):
        gs = slice(g * DIL_GROUP_W, (g + 1) * DIL_GROUP_W)
        w_dil.append(jnp.concatenate([q_d[:, gs][:, quad], k_d[:, gs][:, quad], v_d[:, gs]],
                                     axis=1).astype(BF16))
    return w_main, w_a, w_dil


def _quad_layout():
    half = DIL_HD // 2
    cols = []
    for quad in range(DIL_HEADS // 4):
        for part in range(2):
            for u in range(4):
                start = (4 * quad + u) * DIL_HD + part * half
                cols.extend(range(start, start + half))
    return cols


def _router_pair_tables(w_router, b_router):
    cols_w, cols_b = [], []
    for g in range(N_GROUPS):
        for ea, eb_ in PAIRS:
            lo, hi = EXPERTS_PER_GROUP * g + ea, EXPERTS_PER_GROUP * g + eb_
            w2 = jnp.stack([w_router[:, lo], w_router[:, hi]], axis=1)
            cols_w.append(jnp.pad(w2, ((0, 0), (0, LANES - 2))))
            cols_b.append(jnp.pad(jnp.stack([b_router[lo], b_router[hi]]), (0, LANES - 2)))
    return jnp.stack(cols_w).astype(BF16), jnp.stack(cols_b).reshape(N_BUCKETS, 1, LANES).astype(F32)


def kernel(x, positions, w_in, w_alpha_up, b_alpha, gla_norm, w_gla_o, w_att_o, w_out, ln1_g, ln1_b,
           w_router, b_router, w_e_gate, w_e_up, w_e_down, ln2_g, ln2_b):
    batch, seq, d_model = x.shape
    depth = w_in.shape[0]
    n = batch * seq
    assert d_model == D_MODEL and seq % (DIL_PATTERN[-1][1] * DIL_STEPS) == 0

    cos_t, sin_t = _rope_tables(positions)
    tri, ups = _gla_masks()
    head_of_col = jnp.arange(DIL_GROUP_W) // DIL_HD
    expand = (jnp.arange(LANES)[:, None] == head_of_col[None, :]).astype(BF16)
    wrt = w_router.T.astype(BF16)
    br = jnp.broadcast_to(b_router.astype(F32)[:, None], (N_EXPERTS, LANES))
    wr2, br2 = _router_pair_tables(w_router, b_router)
    n_slots = n + N_BUCKETS * MOE_T

    xf = x.reshape(n, D_MODEL)
    for i in range(depth):
        w_main, w_a, w_dil = _split_w_in(w_in[i])
        h_main = _proj(xf, w_main, BF16, PROJ_TN, n)
        a_proj = _proj(xf, w_a, F32, LANES, n)
        wup = jnp.pad(w_alpha_up[i], ((0, LANES - GLA_RANK), (0, 0))).astype(BF16)
        og = _gla(h_main, a_proj, wup, b_alpha[i].reshape(1, -1).astype(F32),
                  gla_norm[i].reshape(1, -1).astype(F32), tri, ups, batch, seq)
        attn = []
        for g, (_, dil) in enumerate(DIL_PATTERN):
            qkv = _proj_dil(xf, w_dil[g], cos_t, sin_t, batch, seq, dil)
            attn.append(_dil_attn(qkv, batch, seq, dil))
        x1, bucket = _mix(og.reshape(n, GLA_V_W), h_main, attn, xf,
                          w_gla_o[i].astype(BF16), w_att_o[i].astype(BF16), w_out[i].astype(BF16),
                          ln1_g[i].reshape(1, -1), ln1_b[i].reshape(1, -1), wrt, br, expand, batch, seq)
        bucket = bucket.reshape(n)
        rank, counts = _bucket_ranks(bucket, n)
        plan = _sort_plan(bucket, rank, counts, n)
        src = _invert(plan[0], n_slots)
        xf = _moe(x1, src, plan, wr2, br2, w_e_gate[i].astype(BF16), w_e_up[i].astype(BF16),
                  w_e_down[i].astype(BF16), ln2_g[i].reshape(1, -1), ln2_b[i].reshape(1, -1))
    return xf[:n].reshape(batch, seq, D_MODEL)
```

```python
import functools

import jax
import jax.numpy as jnp
from jax import lax
from jax.experimental import pallas as pl
from jax.experimental.pallas import tpu as pltpu

F32 = jnp.float32
BF16 = jnp.bfloat16
I32 = jnp.int32

D_MODEL = 1024
N_LAYERS = 4
GLA_HEADS = 4
GLA_DK = 128
GLA_DV = 256
GLA_RANK = 16
GLA_TAU = 16.0
GLA_CHUNK = 64
DIL_PATTERN = ((128, 1), (512, 4), (2048, 16))
DIL_HEADS = 8
DIL_HD = 64
DIL_STEPS = 128
ROPE_THETA = 10000.0
N_EXPERTS = 16
N_GROUPS = 4
EXPERTS_PER_GROUP = 4
D_FF = 1024
LN_EPS = 1e-5
RMS_EPS = 1e-6
DN_ALPHA = (2 * N_LAYERS) ** 0.25
NEG_INF = -1e30

GLA_QK_W = GLA_HEADS * GLA_DK
GLA_V_W = GLA_HEADS * GLA_DV
DIL_GROUP_W = DIL_HEADS * DIL_HD
N_DIL = len(DIL_PATTERN)
MAIN_W = 2 * GLA_QK_W + 2 * GLA_V_W + 2 * D_MODEL
GATE_COL = 2 * GLA_QK_W + 2 * GLA_V_W

LANES = 128
PAIRS = ((0, 1), (0, 2), (0, 3), (1, 2), (1, 3), (2, 3))
N_BUCKETS = N_GROUPS * len(PAIRS)
BUCKET_ROWS = 32

PROJ_TM = 1024
PROJ_TN = 1024
GLA_T = 512
MIX_TM = 512
SORT_TS = 1024
MOE_T = 256
ATTN_TQ = 512
VMEM_LIMIT = 56 * 1024 * 1024


def _cp(*sem):
    return pltpu.CompilerParams(dimension_semantics=sem, vmem_limit_bytes=VMEM_LIMIT)


def _rope_kernel(pos_ref, invf_ref, cos_ref, sin_ref):
    ang = pos_ref[...].astype(F32) * invf_ref[...]
    cos_ref[...] = jnp.cos(ang)
    sin_ref[...] = jnp.sin(ang)


def _rope_tables(positions):
    n = positions.size
    tm = 2048
    half = DIL_HD // 2
    inv_freq = jnp.power(jnp.float32(ROPE_THETA), -jnp.arange(0, DIL_HD, 2, dtype=F32) / DIL_HD)
    invf = jnp.tile(inv_freq, LANES // half).reshape(1, LANES)
    return pl.pallas_call(
        _rope_kernel,
        grid=(n // tm,),
        in_specs=[pl.BlockSpec((tm, 1), lambda i: (i, 0)),
                  pl.BlockSpec((1, LANES), lambda i: (0, 0))],
        out_specs=[pl.BlockSpec((tm, LANES), lambda i: (i, 0)),
                   pl.BlockSpec((tm, LANES), lambda i: (i, 0))],
        out_shape=[jax.ShapeDtypeStruct((n, LANES), F32)] * 2,
        compiler_params=_cp("parallel"),
        name="rope_tables",
    )(positions.reshape(n, 1), invf)


def _proj_kernel(x_ref, w_ref, o_ref, xb_ref):
    @pl.when(pl.program_id(1) == 0)
    def _():
        xb_ref[...] = x_ref[...].astype(BF16)

    o_ref[...] = jnp.dot(xb_ref[...], w_ref[...], preferred_element_type=F32).astype(o_ref.dtype)


def _proj(x, w, out_dtype, tn, n):
    k = x.shape[1]
    ncol = w.shape[1]
    tm = min(PROJ_TM, n)
    return pl.pallas_call(
        _proj_kernel,
        grid=(n // tm, ncol // tn),
        in_specs=[pl.BlockSpec((tm, k), lambda i, j: (i, 0)),
                  pl.BlockSpec((k, tn), lambda i, j: (0, j))],
        out_specs=pl.BlockSpec((tm, tn), lambda i, j: (i, j)),
        out_shape=jax.ShapeDtypeStruct((n, ncol), out_dtype),
        scratch_shapes=[pltpu.VMEM((tm, k), BF16)],
        compiler_params=_cp("parallel", "arbitrary"),
        name="proj",
    )(x, w)


def _proj_dil_kernel(x_ref, w_ref, cos_ref, sin_ref, o_ref, *slab_refs, dil, tm):
    acc_ref = slab_refs[0] if dil > 1 else None
    acc = jnp.dot(x_ref[...].astype(BF16), w_ref[...], preferred_element_type=F32)
    slabs = DIL_GROUP_W // LANES

    def emit(which, slab, val):
        if dil == 1:
            o_ref[which, 0, 0, :, slab * LANES:(slab + 1) * LANES] = val.astype(BF16)
        else:
            acc_ref[which * slabs + slab] = val

    def col(which, slab):
        start = which * DIL_GROUP_W + slab * LANES
        return acc[:, start:start + LANES]

    for which in range(2):
        scale = DIL_HD ** -0.5 if which == 0 else 1.0
        cos_t = cos_ref[...] * scale
        sin_t = sin_ref[...] * scale
        for quad in range(slabs // 2):
            x1, x2 = col(which, 2 * quad), col(which, 2 * quad + 1)
            emit(which, 2 * quad, x1 * cos_t - x2 * sin_t)
            emit(which, 2 * quad + 1, x2 * cos_t + x1 * sin_t)
    for slab in range(slabs):
        emit(2, slab, col(2, slab))

    if dil > 1:
        rows = tm // dil
        step = 4 if dil > 4 else dil
        outer = dil // step
        part = tm // step
        for s in range(3 * slabs):
            which, slab = divmod(s, slabs)
            if outer > 1:
                tmp_ref = slab_refs[1]
                for lo in range(step):
                    tmp_ref[s, lo * part:(lo + 1) * part, :] = acc_ref[s, pl.ds(lo, part, stride=step), :]
            for r in range(dil):
                hi, lo = divmod(r, step)
                if outer > 1:
                    val = tmp_ref[s, pl.ds(lo * part + hi, rows, stride=outer), :]
                else:
                    val = acc_ref[s, pl.ds(r, rows, stride=dil), :]
                o_ref[which, 0, r, :, slab * LANES:(slab + 1) * LANES] = val.astype(BF16)


def _proj_dil(x, w, cos_t, sin_t, batch, seq, dil):
    n, k = batch * seq, x.shape[1]
    tm = min(PROJ_TM, seq)
    tiles_per_seq = seq // tm
    sub = seq // dil
    kern = functools.partial(_proj_dil_kernel, dil=dil, tm=tm)
    n_slabs = 3 * DIL_GROUP_W // LANES
    return pl.pallas_call(
        kern,
        grid=(n // tm,),
        in_specs=[pl.BlockSpec((tm, k), lambda i: (i, 0)),
                  pl.BlockSpec((k, 3 * DIL_GROUP_W), lambda i: (0, 0)),
                  pl.BlockSpec((tm, LANES), lambda i: (i, 0)),
                  pl.BlockSpec((tm, LANES), lambda i: (i, 0))],
        out_specs=pl.BlockSpec((3, 1, dil, tm // dil, DIL_GROUP_W),
                               lambda i: (0, i // tiles_per_seq, 0, i % tiles_per_seq, 0)),
        out_shape=jax.ShapeDtypeStruct((3, batch, dil, sub, DIL_GROUP_W), BF16),
        scratch_shapes=[pltpu.VMEM((n_slabs, tm, LANES), F32)] * ((dil > 1) + (dil > 4)),
        compiler_params=_cp("parallel"),
        name=f"proj_dil{dil}",
    )(x, w, cos_t, sin_t)


def _split_dot(mat_bf16, val_f32):
    hi = val_f32.astype(BF16)
    lo = (val_f32 - hi.astype(F32)).astype(BF16)
    return (jnp.dot(mat_bf16, hi, preferred_element_type=F32)
            + jnp.dot(mat_bf16, lo, preferred_element_type=F32))


def _gla_kernel(q_ref, k_ref, v_ref, r_ref, a_ref, wup_ref, balpha_ref, gnorm_ref, tri_ref,
                o_ref, state_ref, sprev_ref):
    t = pl.program_id(1)
    n_chunks = GLA_T // GLA_CHUNK

    @pl.when(t == 0)
    def _():
        state_ref[...] = jnp.zeros_like(state_ref)

    z = jnp.dot(a_ref[0].astype(BF16), wup_ref[...], preferred_element_type=F32) + balpha_ref[...]
    log_a = (jnp.minimum(z, 0.0) - jnp.log1p(jnp.exp(-jnp.abs(z)))) / GLA_TAU
    tri = tri_ref[...]
    b = _split_dot(tri, log_a)
    b_end = jnp.concatenate(
        [jnp.broadcast_to(b[(n + 1) * GLA_CHUNK - 1:(n + 1) * GLA_CHUNK], (GLA_CHUNK, GLA_QK_W))
         for n in range(n_chunks)], axis=0)
    c = b_end - b
    eb = jnp.exp(b)
    kf = k_ref[0].astype(F32)
    qt = (q_ref[0].astype(F32) * eb * (GLA_DK ** -0.5)).astype(BF16)
    kt = (kf * jnp.exp(-b)).astype(BF16)
    ks = (kf * jnp.exp(c)).astype(BF16)
    causal = tri > 0

    for h in range(GLA_HEADS):
        ck = slice(h * GLA_DK, (h + 1) * GLA_DK)
        cv = slice(h * GLA_DV, (h + 1) * GLA_DV)
        qt_h, kt_h, ks_h = qt[:, ck], kt[:, ck], ks[:, ck]
        v_h = v_ref[0, :, cv]
        att = lax.dot_general(qt_h, kt_h, (((1,), (1,)), ((), ())), preferred_element_type=F32)
        att = jnp.where(causal, att, 0.0).astype(BF16)
        o_h = jnp.dot(att, v_h, preferred_element_type=F32)

        state = state_ref[h]
        for n in range(n_chunks):
            rs = slice(n * GLA_CHUNK, (n + 1) * GLA_CHUNK)
            contrib = lax.dot_general(v_h[rs], ks_h[rs], (((0,), (0,)), ((), ())),
                                      preferred_element_type=F32)
            sprev_ref[h, n] = state.astype(BF16)
            decay = eb[(n + 1) * GLA_CHUNK - 1:(n + 1) * GLA_CHUNK, ck]
            state = state * decay + contrib
        state_ref[h] = state

        inter = [lax.dot_general(qt_h[n * GLA_CHUNK:(n + 1) * GLA_CHUNK], sprev_ref[h, n],
                                 (((1,), (1,)), ((), ())), preferred_element_type=F32)
                 for n in range(n_chunks)]
        o_h = o_h + jnp.concatenate(inter, axis=0)

        o_h = o_h * lax.rsqrt(jnp.mean(jnp.square(o_h), axis=-1, keepdims=True) + RMS_EPS)
        o_h = o_h * gnorm_ref[:, cv]
        r_h = r_ref[0, :, cv].astype(F32)
        o_ref[0, :, cv] = (o_h * (r_h * jax.nn.sigmoid(r_h))).astype(o_ref.dtype)


def _gla(h_main, a_proj, wup, balpha, gnorm, tri, batch, seq):
    h3 = h_main.reshape(batch, seq, MAIN_W)
    a3 = a_proj.reshape(batch, seq, LANES)
    n_chunks = GLA_T // GLA_CHUNK
    return pl.pallas_call(
        _gla_kernel,
        grid=(batch, seq // GLA_T),
        in_specs=[pl.BlockSpec((1, GLA_T, GLA_QK_W), lambda b, t: (b, t, 0)),
                  pl.BlockSpec((1, GLA_T, GLA_QK_W), lambda b, t: (b, t, 1)),
                  pl.BlockSpec((1, GLA_T, GLA_V_W), lambda b, t: (b, t, 1)),
                  pl.BlockSpec((1, GLA_T, GLA_V_W), lambda b, t: (b, t, 2)),
                  pl.BlockSpec((1, GLA_T, LANES), lambda b, t: (b, t, 0)),
                  pl.BlockSpec((LANES, GLA_QK_W), lambda b, t: (0, 0)),
                  pl.BlockSpec((1, GLA_QK_W), lambda b, t: (0, 0)),
                  pl.BlockSpec((1, GLA_V_W), lambda b, t: (0, 0)),
                  pl.BlockSpec((GLA_T, GLA_T), lambda b, t: (0, 0))],
        out_specs=pl.BlockSpec((1, GLA_T, GLA_V_W), lambda b, t: (b, t, 0)),
        out_shape=jax.ShapeDtypeStruct((batch, seq, GLA_V_W), BF16),
        scratch_shapes=[pltpu.VMEM((GLA_HEADS, GLA_DV, GLA_DK), F32),
                        pltpu.VMEM((GLA_HEADS, n_chunks, GLA_DV, GLA_DK), BF16)],
        compiler_params=_cp("parallel", "arbitrary"),
        name="gla",
    )(h3, h3, h3, h3, a3, wup, balpha, gnorm, tri)


def _gla_masks():
    i = jnp.arange(GLA_T)[:, None]
    j = jnp.arange(GLA_T)[None, :]
    same = (i // GLA_CHUNK) == (j // GLA_CHUNK)
    return (same & (j <= i)).astype(BF16)


def _dil_attn_kernel(q_ref, kc_ref, kp_ref, vc_ref, vp_ref, o_ref, lse_ref, *, tq):
    n = pl.program_id(2)
    st = DIL_STEPS
    qi = lax.broadcasted_iota(I32, (st, 2 * st), 0)
    kj = lax.broadcasted_iota(I32, (st, 2 * st), 1)
    in_prev = kj < st
    band_bias = jnp.where(jnp.where(in_prev, kj - qi, qi + st - kj) >= 0, 0.0, NEG_INF)
    prev_cols = jnp.where(in_prev, 1.0, 0.0)
    lane = lax.broadcasted_iota(I32, (st, LANES), 1)
    low = lane < DIL_HD
    ones = jnp.ones((2 * st, LANES), BF16)
    nt = (((1,), (1,)), ((), ()))

    no_prev = jnp.where(n == 0, NEG_INF, 0.0)
    first_bias = band_bias + prev_cols * no_prev

    def window(cur_ref, prev_ref, i, cs):
        if i == 0:
            return jnp.concatenate([prev_ref[0, 0, 0, :, cs], cur_ref[0, 0, 0, 0:st, cs]], axis=0)
        return cur_ref[0, 0, 0, (i - 1) * st:(i + 1) * st, cs]

    quad_w = 2 * LANES
    quad_lane = lax.broadcasted_iota(I32, (st, quad_w), 1)
    head_in_quad = (quad_lane % LANES) // (DIL_HD // 2)

    for i in range(tq // st):
        r0 = i * st
        bias = first_bias if i == 0 else band_bias
        bias4 = jnp.concatenate([bias] * 4, axis=0)
        stats = jnp.zeros((st, LANES), F32)
        outs = []
        for quad in range(DIL_HEADS // 4):
            qs = slice(quad * quad_w, (quad + 1) * quad_w)
            qq = q_ref[0, 0, 0, r0:r0 + st, qs]
            zero = jnp.zeros_like(qq)
            lhs = jnp.concatenate([jnp.where(head_in_quad == u, qq, zero) for u in range(4)], axis=0)
            s = lax.dot_general(lhs, window(kc_ref, kp_ref, i, qs), nt,
                                preferred_element_type=F32) + bias4
            m = jnp.max(s, axis=-1, keepdims=True)
            pexp = jnp.exp(s - m).astype(BF16)
            m_rep = jnp.broadcast_to(m, (4 * st, LANES))
            for pair in range(2):
                p = 2 * quad + pair
                cs = slice(p * LANES, (p + 1) * LANES)
                v_ext = jnp.concatenate([window(vc_ref, vp_ref, i, cs), ones], axis=1)
                o = jnp.dot(pexp[2 * pair * st:(2 * pair + 2) * st], v_ext,
                            preferred_element_type=F32)
                outs.append(jnp.where(low, o[0:st, 0:LANES], o[st:2 * st, 0:LANES]))
                l_rep = o[:, LANES:2 * LANES]
                for half in range(2):
                    rs = slice(half * st, (half + 1) * st)
                    ms = slice((2 * pair + half) * st, (2 * pair + half + 1) * st)
                    stats = jnp.where(lane == 2 * p + half, m_rep[ms], stats)
                    stats = jnp.where(lane == DIL_HEADS + 2 * p + half, l_rep[rs], stats)
        o_ref[0, 0, r0:r0 + st, :] = jnp.concatenate(outs, axis=1).astype(o_ref.dtype)
        lse_ref[0, 0, r0:r0 + st, :] = stats


def _dil_attn(qkv, batch, seq, dil):
    sub = seq // dil
    tq = min(ATTN_TQ, sub)
    per = tq // DIL_STEPS

    def cur(which):
        return pl.BlockSpec((1, 1, 1, tq, DIL_GROUP_W), lambda b, r, n: (which, b, r, n, 0))

    def prev(which):
        return pl.BlockSpec((1, 1, 1, DIL_STEPS, DIL_GROUP_W),
                            lambda b, r, n: (which, b, r, jnp.maximum(n * per - 1, 0), 0))

    return pl.pallas_call(
        functools.partial(_dil_attn_kernel, tq=tq),
        grid=(batch, dil, sub // tq),
        in_specs=[cur(0), cur(1), prev(1), cur(2), prev(2)],
        out_specs=[pl.BlockSpec((1, 1, tq, DIL_GROUP_W), lambda b, r, n: (b, r, n, 0)),
                   pl.BlockSpec((1, 1, tq, LANES), lambda b, r, n: (b, r, n, 0))],
        out_shape=[jax.ShapeDtypeStruct((batch, dil, sub, DIL_GROUP_W), BF16),
                   jax.ShapeDtypeStruct((batch, dil, sub, LANES), F32)],
        compiler_params=_cp("parallel", "parallel", "arbitrary"),
        name=f"dil_attn{dil}",
    )(qkv, qkv, qkv, qkv, qkv)


def _layer_norm(x, g, b):
    mu = jnp.mean(x, axis=-1, keepdims=True)
    var = jnp.mean(jnp.square(x - mu), axis=-1, keepdims=True)
    return (x - mu) * lax.rsqrt(var + LN_EPS) * g + b


def _mix_kernel(og_ref, g1_ref, g2_ref, o0_ref, o1_ref, o2_ref, l0_ref, l1_ref, l2_ref, x_ref,
                wgo_ref, wao_ref, wout_ref, lng_ref, lnb_ref, wrt_ref, br_ref, expand_ref,
                x1_ref, bucket_ref, osc_ref, lsc_ref):
    tm = x_ref.shape[0]

    def interleaved(o_ref, l_ref, dil, slot):
        if dil == 1:
            return o_ref[0, 0].astype(F32), l_ref[0, 0]
        rows = tm // dil
        chunks = DIL_GROUP_W // LANES
        for r in range(dil):
            for c in range(chunks):
                osc_ref[slot, c, pl.ds(r, rows, stride=dil), :] = (
                    o_ref[0, r, :, c * LANES:(c + 1) * LANES].astype(F32))
            lsc_ref[slot, pl.ds(r, rows, stride=dil), :] = l_ref[0, r]
        return jnp.concatenate([osc_ref[slot, c] for c in range(chunks)], axis=1), lsc_ref[slot]

    groups = [interleaved(o_ref, l_ref, dil, slot)
              for slot, (o_ref, l_ref, (_, dil)) in enumerate(
                  zip((o0_ref, o1_ref, o2_ref), (l0_ref, l1_ref, l2_ref), DIL_PATTERN))]
    stats = [g[1] for g in groups]
    m_max = jnp.maximum(jnp.maximum(stats[0], stats[1]), stats[2])
    es = [jnp.exp(s - m_max) for s in stats]
    ls = [pltpu.roll(s, LANES - DIL_HEADS, 1) for s in stats]
    inv = 1.0 / (es[0] * ls[0] + es[1] * ls[1] + es[2] * ls[2])
    head_lane = lax.broadcasted_iota(I32, (tm, LANES), 1) < DIL_HEADS
    att = jnp.zeros((tm, DIL_GROUP_W), F32)
    for (num_g, _), e in zip(groups, es):
        coef = jnp.where(head_lane, e * inv, 0.0)
        att = att + _split_dot_rhs(coef, expand_ref[...]) * num_g

    y_att = jnp.dot(att.astype(BF16), wao_ref[...], preferred_element_type=F32)
    y_gla = jnp.dot(og_ref[...], wgo_ref[...], preferred_element_type=F32)
    mixed = (jax.nn.sigmoid(g1_ref[...].astype(F32)) * y_gla
             + jax.nn.sigmoid(g2_ref[...].astype(F32)) * y_att)
    mix = jnp.dot(mixed.astype(BF16), wout_ref[...], preferred_element_type=F32)
    x1 = _layer_norm(DN_ALPHA * x_ref[...] + mix, lng_ref[...], lnb_ref[...])
    x1_ref[...] = x1

    logits = lax.dot_general(wrt_ref[...], x1.astype(BF16), (((1,), (1,)), ((), ())),
                             preferred_element_type=F32) + br_ref[:, 0:1]
    e = jnp.exp(logits - jnp.max(logits, axis=0, keepdims=True))
    probs = e / jnp.sum(e, axis=0, keepdims=True)
    best = None
    for g in range(N_GROUPS):
        for pi, (ea, eb_) in enumerate(PAIRS):
            cand = probs[4 * g + ea:4 * g + ea + 1] + probs[4 * g + eb_:4 * g + eb_ + 1]
            idx = g * len(PAIRS) + pi
            if best is None:
                best, bidx = cand, jnp.zeros(cand.shape, I32)
            else:
                better = cand > best
                best = jnp.where(better, cand, best)
                bidx = jnp.where(better, idx, bidx)
    bucket_ref[0] = bidx


def _split_dot_rhs(val_f32, mat_bf16):
    hi = val_f32.astype(BF16)
    lo = (val_f32 - hi.astype(F32)).astype(BF16)
    return (jnp.dot(hi, mat_bf16, preferred_element_type=F32)
            + jnp.dot(lo, mat_bf16, preferred_element_type=F32))


def _mix(og, h_main, attn, x, wgo, wao, wout, lng, lnb, wrt, br, expand, batch, seq):
    n = batch * seq
    tm = min(MIX_TM, seq)
    tps = seq // tm
    row = lambda i: (i, 0)
    const = lambda i: (0, 0)
    o_specs, l_specs, o_args, l_args = [], [], [], []
    for (o_g, lse_g), (_, dil) in zip(attn, DIL_PATTERN):
        o_specs.append(pl.BlockSpec((1, dil, tm // dil, DIL_GROUP_W), lambda i: (i // tps, 0, i % tps, 0)))
        l_specs.append(pl.BlockSpec((1, dil, tm // dil, LANES), lambda i: (i // tps, 0, i % tps, 0)))
        o_args.append(o_g)
        l_args.append(lse_g)
    gate_blk = GATE_COL // D_MODEL
    return pl.pallas_call(
        _mix_kernel,
        grid=(n // tm,),
        in_specs=[pl.BlockSpec((tm, GLA_V_W), row),
                  pl.BlockSpec((tm, D_MODEL), lambda i: (i, gate_blk)),
                  pl.BlockSpec((tm, D_MODEL), lambda i: (i, gate_blk + 1)),
                  *o_specs, *l_specs,
                  pl.BlockSpec((tm, D_MODEL), row),
                  pl.BlockSpec((GLA_V_W, D_MODEL), const),
                  pl.BlockSpec((DIL_GROUP_W, D_MODEL), const),
                  pl.BlockSpec((D_MODEL, D_MODEL), const),
                  pl.BlockSpec((1, D_MODEL), const),
                  pl.BlockSpec((1, D_MODEL), const),
                  pl.BlockSpec((N_EXPERTS, D_MODEL), const),
                  pl.BlockSpec((N_EXPERTS, LANES), const),
                  pl.BlockSpec((LANES, DIL_GROUP_W), const)],
        out_specs=[pl.BlockSpec((tm, D_MODEL), row),
                   pl.BlockSpec((1, 1, tm), lambda i: (i, 0, 0))],
        out_shape=[jax.ShapeDtypeStruct((n, D_MODEL), F32),
                   jax.ShapeDtypeStruct((n // tm, 1, tm), I32)],
        scratch_shapes=[pltpu.VMEM((N_DIL, DIL_GROUP_W // LANES, tm, LANES), F32),
                        pltpu.VMEM((N_DIL, tm, LANES), F32)],
        compiler_params=_cp("parallel"),
        name="mix",
    )(og, h_main, h_main, *o_args, *l_args, x, wgo, wao, wout, lng, lnb, wrt, br, expand)


def _rank_kernel(bucket_ref, upper_ref, rank_ref, counts_ref, carry_ref):
    @pl.when(pl.program_id(0) == 0)
    def _():
        carry_ref[...] = jnp.zeros_like(carry_ref)

    b = bucket_ref[0]
    ts = b.shape[1]
    onehot = lax.broadcasted_iota(I32, (BUCKET_ROWS, ts), 0) == b
    oh_f = jnp.where(onehot, 1.0, 0.0)
    before = jnp.dot(oh_f.astype(BF16), upper_ref[...], preferred_element_type=F32)
    carry = carry_ref[...]
    rank = jnp.sum(oh_f * (before + carry[:, 0:1]), axis=0, keepdims=True)
    rank_ref[0] = rank.astype(I32)
    carry = carry + jnp.sum(oh_f, axis=1, keepdims=True)
    carry_ref[...] = carry
    counts_ref[...] = carry.astype(I32)


def _bucket_ranks(bucket, n):
    ts = min(SORT_TS, n)
    b3 = bucket.reshape(n // ts, 1, ts)
    j = jnp.arange(ts)
    upper = (j[:, None] < j[None, :]).astype(BF16)
    rank, counts = pl.pallas_call(
        _rank_kernel,
        grid=(n // ts,),
        in_specs=[pl.BlockSpec((1, 1, ts), lambda i: (i, 0, 0)),
                  pl.BlockSpec((ts, ts), lambda i: (0, 0))],
        out_specs=[pl.BlockSpec((1, 1, ts), lambda i: (i, 0, 0)),
                   pl.BlockSpec((BUCKET_ROWS, LANES), lambda i: (0, 0))],
        out_shape=[jax.ShapeDtypeStruct((n // ts, 1, ts), I32),
                   jax.ShapeDtypeStruct((BUCKET_ROWS, LANES), I32)],
        scratch_shapes=[pltpu.VMEM((BUCKET_ROWS, LANES), F32)],
        compiler_params=_cp("arbitrary"),
        name="bucket_rank",
    )(b3, upper)
    return rank.reshape(n), counts[:N_BUCKETS, 0]


def _sort_plan(bucket, rank, counts, n):
    tiles = (counts + MOE_T - 1) // MOE_T
    tile_end = jnp.cumsum(tiles)
    offsets = (tile_end - tiles) * MOE_T
    dest = offsets[bucket] + rank
    n_tiles = n // MOE_T + N_BUCKETS
    tile_ids = jnp.arange(n_tiles, dtype=I32)
    total = tile_end[-1]
    last = jnp.minimum(tile_ids, total - 1)
    tb = jnp.sum((last[:, None] >= tile_end[None, :]).astype(I32), axis=1)
    tb = jnp.minimum(tb, N_BUCKETS - 1)
    n_valid = jnp.clip(counts[tb] - (tile_ids - (tile_end - tiles)[tb]) * MOE_T, 0, MOE_T)
    n_valid = jnp.where(tile_ids < total, n_valid, 0)
    pair = jnp.array(PAIRS, dtype=I32)
    e_lo = (tb // len(PAIRS)) * EXPERTS_PER_GROUP + pair[tb % len(PAIRS), 0]
    e_hi = (tb // len(PAIRS)) * EXPERTS_PER_GROUP + pair[tb % len(PAIRS), 1]
    pad_lo = jnp.concatenate([offsets + counts, total[None] * MOE_T])
    pad_hi = jnp.concatenate([tile_end * MOE_T, jnp.full((1,), n_tiles * MOE_T, I32)])
    pad_ranges = jnp.stack([pad_lo, pad_hi]).astype(I32)
    return dest.astype(I32), tb, e_lo, e_hi, n_valid.astype(I32), pad_ranges


def _invert_kernel(pad_ref, dest_ref, src_ref, table_ref, sem, *, chunk):
    i = pl.program_id(0)

    @pl.when(i == 0)
    def _():
        def clear(j, carry):
            table_ref[j] = 0
            return carry
        for b in range(N_BUCKETS + 1):
            lax.fori_loop(pad_ref[0, b], pad_ref[1, b], clear, 0)

    def put(j, carry):
        table_ref[dest_ref[0, 0, j]] = i * chunk + j
        return carry
    lax.fori_loop(0, chunk, put, 0, unroll=8)

    @pl.when(i == pl.num_programs(0) - 1)
    def _():
        copy = pltpu.make_async_copy(table_ref, src_ref, sem)
        copy.start()
        copy.wait()


def _invert(dest, pad_ranges, n_slots):
    n = dest.shape[0]
    chunk = min(2048, n)
    return pl.pallas_call(
        functools.partial(_invert_kernel, chunk=chunk),
        grid=(n // chunk,),
        in_specs=[pl.BlockSpec(memory_space=pltpu.SMEM),
                  pl.BlockSpec((1, 1, chunk), lambda i: (i, 0, 0), memory_space=pltpu.SMEM)],
        out_specs=pl.BlockSpec(memory_space=pl.ANY),
        out_shape=jax.ShapeDtypeStruct((n_slots,), I32),
        scratch_shapes=[pltpu.SMEM((n_slots,), I32), pltpu.SemaphoreType.DMA(())],
        compiler_params=_cp("arbitrary"),
        name="invert_perm",
    )(pad_ranges, dest.reshape(n // chunk, 1, chunk))


def _moe_kernel(tb_ref, elo_ref, ehi_ref, nvalid_ref, prv_ref, src_ref, nxt_ref, x_hbm, wr2_ref, br2_ref,
                wg_lo_ref, wu_lo_ref, wd_lo_ref, wg_hi_ref, wu_hi_ref, wd_hi_ref, lng_ref, lnb_ref,
                out_hbm, xbuf_ref, ybuf_ref, gsem, ssem, *, n_tokens):
    del tb_ref, elo_ref, ehi_ref
    t = pl.program_id(0)
    last = pl.num_programs(0) - 1
    slot = t % 2
    other = 1 - slot
    tile = MOE_T
    n_chunks = 8
    per_chunk = tile // n_chunks

    def gather_copy(idx_ref, r, sl):
        return pltpu.make_async_copy(x_hbm.at[pl.ds(idx_ref[0, 0, r], 1)],
                                     xbuf_ref.at[sl, pl.ds(r, 1)], gsem.at[sl])

    def scatter_copy(dst, r, sl):
        return pltpu.make_async_copy(ybuf_ref.at[sl, pl.ds(r, 1)], out_hbm.at[pl.ds(dst, 1)], ssem.at[sl])

    def scatter_rows(idx_ref, n_valid, rows, sl):
        for r in rows:
            dst = jnp.where(r < n_valid, idx_ref[0, 0, r], n_tokens + sl * tile + r)
            scatter_copy(dst, r, sl).start()

    prev_valid = jnp.where(t > 0, nvalid_ref[jnp.maximum(t - 1, 0)], 0)

    def issue(chunk):
        rows = range(chunk * per_chunk, (chunk + 1) * per_chunk)
        for r in rows:
            gather_copy(nxt_ref, r, other).start()
        scatter_rows(prv_ref, prev_valid, rows, other)

    @pl.when(t == 0)
    def _():
        ybuf_ref[...] = jnp.zeros_like(ybuf_ref)
        for r in range(tile):
            gather_copy(src_ref, r, 0).start()
        scatter_rows(src_ref, 0, range(tile), 0)

    for r in range(tile):
        gather_copy(src_ref, r, slot).wait()
    for r in range(tile):
        scatter_copy(0, r, slot).wait()

    @pl.when(nvalid_ref[t] == 0)
    def _():
        for chunk in range(n_chunks):
            issue(chunk)

    @pl.when(nvalid_ref[t] != 0)
    def _():
        x = xbuf_ref[slot]
        xb = x.astype(BF16)
        issue(0)
        l2 = jnp.dot(xb, wr2_ref[0], preferred_element_type=F32) + br2_ref[0]
        g_lo = jax.nn.sigmoid(l2[:, 0:1] - l2[:, 1:2])
        g_hi = jax.nn.sigmoid(l2[:, 1:2] - l2[:, 0:1])

        def expert(wg_ref, wu_ref, wd_ref, first_chunk):
            a = jnp.dot(xb, wg_ref[0], preferred_element_type=F32)
            issue(first_chunk)
            u = jnp.dot(xb, wu_ref[0], preferred_element_type=F32)
            issue(first_chunk + 1)
            hidden = (a * jax.nn.sigmoid(a) * u).astype(BF16)
            out = jnp.dot(hidden, wd_ref[0], preferred_element_type=F32)
            issue(first_chunk + 2)
            return out

        y = g_lo * expert(wg_lo_ref, wu_lo_ref, wd_lo_ref, 1)
        y = y + g_hi * expert(wg_hi_ref, wu_hi_ref, wd_hi_ref, 4)
        ybuf_ref[slot] = _layer_norm(DN_ALPHA * x + y, lng_ref[...], lnb_ref[...])
        issue(7)

    @pl.when(t == last)
    def _():
        for r in range(tile):
            gather_copy(nxt_ref, r, other).wait()
        scatter_rows(src_ref, nvalid_ref[t], range(tile), slot)
        for sl in range(2):
            for r in range(tile):
                scatter_copy(0, r, sl).wait()


def _moe(x1, src, plan, wr2, br2, wg, wu, wd, lng, lnb):
    _, tb, e_lo, e_hi, n_valid, _ = plan
    n = x1.shape[0]
    n_tiles = src.shape[0] // MOE_T
    src3 = src.reshape(n_tiles, 1, MOE_T)
    lo = lambda t, tb, elo, ehi, v: (elo[t], 0, 0)
    hi = lambda t, tb, elo, ehi, v: (ehi[t], 0, 0)
    const = lambda t, tb, elo, ehi, v: (0, 0)
    wblk = (1, D_MODEL, D_FF)
    idx_blk = (1, 1, MOE_T)
    grid_spec = pltpu.PrefetchScalarGridSpec(
        num_scalar_prefetch=4,
        grid=(n_tiles,),
        in_specs=[pl.BlockSpec(idx_blk, lambda t, tb, elo, ehi, v: (jnp.maximum(t - 1, 0), 0, 0),
                               memory_space=pltpu.SMEM),
                  pl.BlockSpec(idx_blk, lambda t, tb, elo, ehi, v: (t, 0, 0), memory_space=pltpu.SMEM),
                  pl.BlockSpec(idx_blk, lambda t, tb, elo, ehi, v: (jnp.minimum(t + 1, n_tiles - 1), 0, 0),
                               memory_space=pltpu.SMEM),
                  pl.BlockSpec(memory_space=pl.ANY),
                  pl.BlockSpec((1, D_MODEL, LANES), lambda t, tb, elo, ehi, v: (tb[t], 0, 0)),
                  pl.BlockSpec((1, 1, LANES), lambda t, tb, elo, ehi, v: (tb[t], 0, 0)),
                  pl.BlockSpec(wblk, lo), pl.BlockSpec(wblk, lo), pl.BlockSpec((1, D_FF, D_MODEL), lo),
                  pl.BlockSpec(wblk, hi), pl.BlockSpec(wblk, hi), pl.BlockSpec((1, D_FF, D_MODEL), hi),
                  pl.BlockSpec((1, D_MODEL), const), pl.BlockSpec((1, D_MODEL), const)],
        out_specs=pl.BlockSpec(memory_space=pl.ANY),
        scratch_shapes=[pltpu.VMEM((2, MOE_T, D_MODEL), F32), pltpu.VMEM((2, MOE_T, D_MODEL), F32),
                        pltpu.SemaphoreType.DMA((2,)), pltpu.SemaphoreType.DMA((2,))],
    )
    return pl.pallas_call(
        functools.partial(_moe_kernel, n_tokens=n),
        grid_spec=grid_spec,
        out_shape=jax.ShapeDtypeStruct((n + 2 * MOE_T, D_MODEL), F32),
        compiler_params=_cp("arbitrary"),
        name="moe",
    )(tb, e_lo, e_hi, n_valid, src3, src3, src3, x1, wr2, br2, wg, wu, wd, wg, wu, wd, lng, lnb)


def _split_w_in(w_in_l):
    sizes = (GLA_QK_W, GLA_QK_W, GLA_V_W, GLA_V_W, GLA_RANK,
             N_DIL * DIL_GROUP_W, N_DIL * DIL_GROUP_W, N_DIL * DIL_GROUP_W, 2 * D_MODEL)
    starts = [sum(sizes[:i]) for i in range(len(sizes))]
    blk = [w_in_l[:, s:s + z] for s, z in zip(starts, sizes)]
    q_g, k_g, v_g, r_g, a_g, q_d, k_d, v_d, gates = blk
    w_main = jnp.concatenate([q_g, k_g, v_g, r_g, gates], axis=1).astype(BF16)
    w_a = jnp.pad(a_g, ((0, 0), (0, LANES - GLA_RANK))).astype(BF16)
    quad = jnp.asarray(_quad_layout())
    w_dil = []
    for g in range(N_DIL):
        gs = slice(g * DIL_GROUP_W, (g + 1) * DIL_GROUP_W)
        w_dil.append(jnp.concatenate([q_d[:, gs][:, quad], k_d[:, gs][:, quad], v_d[:, gs]],
                                     axis=1).astype(BF16))
    return w_main, w_a, w_dil


def _quad_layout():
    half = DIL_HD // 2
    cols = []
    for quad in range(DIL_HEADS // 4):
        for part in range(2):
            for u in range(4):
                start = (4 * quad + u) * DIL_HD + part * half
                cols.extend(range(start, start + half))
    return cols


def _router_pair_tables(w_router, b_router):
    cols_w, cols_b = [], []
    for g in range(N_GROUPS):
        for ea, eb_ in PAIRS:
            lo, hi = EXPERTS_PER_GROUP * g + ea, EXPERTS_PER_GROUP * g + eb_
            w2 = jnp.stack([w_router[:, lo], w_router[:, hi]], axis=1)
            cols_w.append(jnp.pad(w2, ((0, 0), (0, LANES - 2))))
            cols_b.append(jnp.pad(jnp.stack([b_router[lo], b_router[hi]]), (0, LANES - 2)))
    return jnp.stack(cols_w).astype(BF16), jnp.stack(cols_b).reshape(N_BUCKETS, 1, LANES).astype(F32)


def kernel(x, positions, w_in, w_alpha_up, b_alpha, gla_norm, w_gla_o, w_att_o, w_out, ln1_g, ln1_b,
           w_router, b_router, w_e_gate, w_e_up, w_e_down, ln2_g, ln2_b):
    batch, seq, d_model = x.shape
    depth = w_in.shape[0]
    n = batch * seq
    assert d_model == D_MODEL and seq % (DIL_PATTERN[-1][1] * DIL_STEPS) == 0

    cos_t, sin_t = _rope_tables(positions)
    tri = _gla_masks()
    head_of_col = jnp.arange(DIL_GROUP_W) // DIL_HD
    expand = (jnp.arange(LANES)[:, None] == head_of_col[None, :]).astype(BF16)
    wrt = w_router.T.astype(BF16)
    br = jnp.broadcast_to(b_router.astype(F32)[:, None], (N_EXPERTS, LANES))
    wr2, br2 = _router_pair_tables(w_router, b_router)
    n_slots = n + N_BUCKETS * MOE_T

    xf = x.reshape(n, D_MODEL)
    for i in range(depth):
        w_main, w_a, w_dil = _split_w_in(w_in[i])
        h_main = _proj(xf, w_main, BF16, PROJ_TN, n)
        a_proj = _proj(xf, w_a, F32, LANES, n)
        wup = jnp.pad(w_alpha_up[i], ((0, LANES - GLA_RANK), (0, 0))).astype(BF16)
        og = _gla(h_main, a_proj, wup, b_alpha[i].reshape(1, -1).astype(F32),
                  gla_norm[i].reshape(1, -1).astype(F32), tri, batch, seq)
        attn = []
        for g, (_, dil) in enumerate(DIL_PATTERN):
            qkv = _proj_dil(xf, w_dil[g], cos_t, sin_t, batch, seq, dil)
            attn.append(_dil_attn(qkv, batch, seq, dil))
        x1, bucket = _mix(og.reshape(n, GLA_V_W), h_main, attn, xf,
                          w_gla_o[i].astype(BF16), w_att_o[i].astype(BF16), w_out[i].astype(BF16),
                          ln1_g[i].reshape(1, -1), ln1_b[i].reshape(1, -1), wrt, br, expand, batch, seq)
        bucket = bucket.reshape(n)
        rank, counts = _bucket_ranks(bucket, n)
        plan = _sort_plan(bucket, rank, counts, n)
        src = _invert(plan[0], plan[5], n_slots)
        xf = _moe(x1, src, plan, wr2, br2, w_e_gate[i].astype(BF16), w_e_up[i].astype(BF16),
                  w_e_down[i].astype(BF16), ln2_g[i].reshape(1, -1), ln2_b[i].reshape(1, -1))
    return xf[:n].reshape(batch, seq, D_MODEL)
```

```python
import functools

import jax
import jax.numpy as jnp
from jax import lax
from jax.experimental import pallas as pl
from jax.experimental.pallas import tpu as pltpu

F32 = jnp.float32
BF16 = jnp.bfloat16
I32 = jnp.int32

D_MODEL = 1024
N_LAYERS = 4
GLA_HEADS = 4
GLA_DK = 128
GLA_DV = 256
GLA_RANK = 16
GLA_TAU = 16.0
GLA_CHUNK = 64
DIL_PATTERN = ((128, 1), (512, 4), (2048, 16))
DIL_HEADS = 8
DIL_HD = 64
DIL_STEPS = 128
ROPE_THETA = 10000.0
N_EXPERTS = 16
N_GROUPS = 4
EXPERTS_PER_GROUP = 4
D_FF = 1024
LN_EPS = 1e-5
RMS_EPS = 1e-6
DN_ALPHA = (2 * N_LAYERS) ** 0.25
NEG_INF = -1e30

GLA_QK_W = GLA_HEADS * GLA_DK
GLA_V_W = GLA_HEADS * GLA_DV
DIL_GROUP_W = DIL_HEADS * DIL_HD
N_DIL = len(DIL_PATTERN)
MAIN_W = 2 * GLA_QK_W + 2 * GLA_V_W + 2 * D_MODEL
GATE_COL = 2 * GLA_QK_W + 2 * GLA_V_W

LANES = 128
TOKEN_ROWS = D_MODEL // LANES
PAIRS = ((0, 1), (0, 2), (0, 3), (1, 2), (1, 3), (2, 3))
N_BUCKETS = N_GROUPS * len(PAIRS)
BUCKET_ROWS = 32

PROJ_TM = 1024
PROJ_TN = 1024
GLA_T = 512
MIX_TM = 512
SORT_TS = 1024
MOE_T = 256
ATTN_TQ = 512
VMEM_LIMIT = 56 * 1024 * 1024


def _cp(*sem):
    return pltpu.CompilerParams(dimension_semantics=sem, vmem_limit_bytes=VMEM_LIMIT)


def _rope_kernel(pos_ref, invf_ref, cos_ref, sin_ref):
    ang = pos_ref[...].astype(F32) * invf_ref[...]
    cos_ref[...] = jnp.cos(ang)
    sin_ref[...] = jnp.sin(ang)


def _rope_tables(positions):
    n = positions.size
    tm = 2048
    half = DIL_HD // 2
    inv_freq = jnp.power(jnp.float32(ROPE_THETA), -jnp.arange(0, DIL_HD, 2, dtype=F32) / DIL_HD)
    invf = jnp.tile(inv_freq, LANES // half).reshape(1, LANES)
    return pl.pallas_call(
        _rope_kernel,
        grid=(n // tm,),
        in_specs=[pl.BlockSpec((tm, 1), lambda i: (i, 0)),
                  pl.BlockSpec((1, LANES), lambda i: (0, 0))],
        out_specs=[pl.BlockSpec((tm, LANES), lambda i: (i, 0)),
                   pl.BlockSpec((tm, LANES), lambda i: (i, 0))],
        out_shape=[jax.ShapeDtypeStruct((n, LANES), F32)] * 2,
        compiler_params=_cp("parallel"),
        name="rope_tables",
    )(positions.reshape(n, 1), invf)


def _proj_kernel(x_ref, w_ref, o_ref, xb_ref):
    @pl.when(pl.program_id(1) == 0)
    def _():
        xb_ref[...] = x_ref[...].astype(BF16)

    o_ref[...] = jnp.dot(xb_ref[...], w_ref[...], preferred_element_type=F32).astype(o_ref.dtype)


def _proj(x, w, out_dtype, tn, n):
    k = x.shape[1]
    ncol = w.shape[1]
    tm = min(PROJ_TM, n)
    return pl.pallas_call(
        _proj_kernel,
        grid=(n // tm, ncol // tn),
        in_specs=[pl.BlockSpec((tm, k), lambda i, j: (i, 0)),
                  pl.BlockSpec((k, tn), lambda i, j: (0, j))],
        out_specs=pl.BlockSpec((tm, tn), lambda i, j: (i, j)),
        out_shape=jax.ShapeDtypeStruct((n, ncol), out_dtype),
        scratch_shapes=[pltpu.VMEM((tm, k), BF16)],
        compiler_params=_cp("parallel", "arbitrary"),
        name="proj",
    )(x, w)


def _proj_dil_kernel(x_ref, w_ref, cos_ref, sin_ref, o_ref, *slab_refs, dil, tm):
    acc_ref = slab_refs[0] if dil > 1 else None
    acc = jnp.dot(x_ref[...].astype(BF16), w_ref[...], preferred_element_type=F32)
    slabs = DIL_GROUP_W // LANES

    def emit(which, slab, val):
        if dil == 1:
            o_ref[which, 0, 0, :, slab * LANES:(slab + 1) * LANES] = val.astype(BF16)
        else:
            acc_ref[which * slabs + slab] = val

    def col(which, slab):
        start = which * DIL_GROUP_W + slab * LANES
        return acc[:, start:start + LANES]

    for which in range(2):
        scale = DIL_HD ** -0.5 if which == 0 else 1.0
        cos_t = cos_ref[...] * scale
        sin_t = sin_ref[...] * scale
        for quad in range(slabs // 2):
            x1, x2 = col(which, 2 * quad), col(which, 2 * quad + 1)
            emit(which, 2 * quad, x1 * cos_t - x2 * sin_t)
            emit(which, 2 * quad + 1, x2 * cos_t + x1 * sin_t)
    for slab in range(slabs):
        emit(2, slab, col(2, slab))

    if dil > 1:
        rows = tm // dil
        step = 4 if dil > 4 else dil
        outer = dil // step
        part = tm // step
        for s in range(3 * slabs):
            which, slab = divmod(s, slabs)
            if outer > 1:
                tmp_ref = slab_refs[1]
                for lo in range(step):
                    tmp_ref[s, lo * part:(lo + 1) * part, :] = acc_ref[s, pl.ds(lo, part, stride=step), :]
            for r in range(dil):
                hi, lo = divmod(r, step)
                if outer > 1:
                    val = tmp_ref[s, pl.ds(lo * part + hi, rows, stride=outer), :]
                else:
                    val = acc_ref[s, pl.ds(r, rows, stride=dil), :]
                o_ref[which, 0, r, :, slab * LANES:(slab + 1) * LANES] = val.astype(BF16)


def _proj_dil(x, w, cos_t, sin_t, batch, seq, dil):
    n, k = batch * seq, x.shape[1]
    tm = min(PROJ_TM, seq)
    tiles_per_seq = seq // tm
    sub = seq // dil
    kern = functools.partial(_proj_dil_kernel, dil=dil, tm=tm)
    n_slabs = 3 * DIL_GROUP_W // LANES
    return pl.pallas_call(
        kern,
        grid=(n // tm,),
        in_specs=[pl.BlockSpec((tm, k), lambda i: (i, 0)),
                  pl.BlockSpec((k, 3 * DIL_GROUP_W), lambda i: (0, 0)),
                  pl.BlockSpec((tm, LANES), lambda i: (i, 0)),
                  pl.BlockSpec((tm, LANES), lambda i: (i, 0))],
        out_specs=pl.BlockSpec((3, 1, dil, tm // dil, DIL_GROUP_W),
                               lambda i: (0, i // tiles_per_seq, 0, i % tiles_per_seq, 0)),
        out_shape=jax.ShapeDtypeStruct((3, batch, dil, sub, DIL_GROUP_W), BF16),
        scratch_shapes=[pltpu.VMEM((n_slabs, tm, LANES), F32)] * ((dil > 1) + (dil > 4)),
        compiler_params=_cp("parallel"),
        name=f"proj_dil{dil}",
    )(x, w, cos_t, sin_t)


def _split_dot(mat_bf16, val_f32):
    hi = val_f32.astype(BF16)
    lo = (val_f32 - hi.astype(F32)).astype(BF16)
    return (jnp.dot(mat_bf16, hi, preferred_element_type=F32)
            + jnp.dot(mat_bf16, lo, preferred_element_type=F32))


def _gla_kernel(q_ref, k_ref, v_ref, r_ref, a_ref, wup_ref, balpha_ref, gnorm_ref, tri_ref,
                o_ref, state_ref, sprev_ref):
    t = pl.program_id(1)
    n_chunks = GLA_T // GLA_CHUNK

    @pl.when(t == 0)
    def _():
        state_ref[...] = jnp.zeros_like(state_ref)

    z = jnp.dot(a_ref[0].astype(BF16), wup_ref[...], preferred_element_type=F32) + balpha_ref[...]
    log_a = (jnp.minimum(z, 0.0) - jnp.log1p(jnp.exp(-jnp.abs(z)))) / GLA_TAU
    tri = tri_ref[...]
    b = _split_dot(tri, log_a)
    b_end = jnp.concatenate(
        [jnp.broadcast_to(b[(n + 1) * GLA_CHUNK - 1:(n + 1) * GLA_CHUNK], (GLA_CHUNK, GLA_QK_W))
         for n in range(n_chunks)], axis=0)
    c = b_end - b
    eb = jnp.exp(b)
    kf = k_ref[0].astype(F32)
    qt = (q_ref[0].astype(F32) * eb * (GLA_DK ** -0.5)).astype(BF16)
    kt = (kf * jnp.exp(-b)).astype(BF16)
    ks = (kf * jnp.exp(c)).astype(BF16)
    causal = tri > 0

    for h in range(GLA_HEADS):
        ck = slice(h * GLA_DK, (h + 1) * GLA_DK)
        cv = slice(h * GLA_DV, (h + 1) * GLA_DV)
        qt_h, kt_h, ks_h = qt[:, ck], kt[:, ck], ks[:, ck]
        v_h = v_ref[0, :, cv]
        att = lax.dot_general(qt_h, kt_h, (((1,), (1,)), ((), ())), preferred_element_type=F32)
        att = jnp.where(causal, att, 0.0).astype(BF16)
        o_h = jnp.dot(att, v_h, preferred_element_type=F32)

        state = state_ref[h]
        for n in range(n_chunks):
            rs = slice(n * GLA_CHUNK, (n + 1) * GLA_CHUNK)
            contrib = lax.dot_general(v_h[rs], ks_h[rs], (((0,), (0,)), ((), ())),
                                      preferred_element_type=F32)
            sprev_ref[h, n] = state.astype(BF16)
            decay = eb[(n + 1) * GLA_CHUNK - 1:(n + 1) * GLA_CHUNK, ck]
            state = state * decay + contrib
        state_ref[h] = state

        inter = [lax.dot_general(qt_h[n * GLA_CHUNK:(n + 1) * GLA_CHUNK], sprev_ref[h, n],
                                 (((1,), (1,)), ((), ())), preferred_element_type=F32)
                 for n in range(n_chunks)]
        o_h = o_h + jnp.concatenate(inter, axis=0)

        o_h = o_h * lax.rsqrt(jnp.mean(jnp.square(o_h), axis=-1, keepdims=True) + RMS_EPS)
        o_h = o_h * gnorm_ref[:, cv]
        r_h = r_ref[0, :, cv].astype(F32)
        o_ref[0, :, cv] = (o_h * (r_h * jax.nn.sigmoid(r_h))).astype(o_ref.dtype)


def _gla(h_main, a_proj, wup, balpha, gnorm, tri, batch, seq):
    h3 = h_main.reshape(batch, seq, MAIN_W)
    a3 = a_proj.reshape(batch, seq, LANES)
    n_chunks = GLA_T // GLA_CHUNK
    return pl.pallas_call(
        _gla_kernel,
        grid=(batch, seq // GLA_T),
        in_specs=[pl.BlockSpec((1, GLA_T, GLA_QK_W), lambda b, t: (b, t, 0)),
                  pl.BlockSpec((1, GLA_T, GLA_QK_W), lambda b, t: (b, t, 1)),
                  pl.BlockSpec((1, GLA_T, GLA_V_W), lambda b, t: (b, t, 1)),
                  pl.BlockSpec((1, GLA_T, GLA_V_W), lambda b, t: (b, t, 2)),
                  pl.BlockSpec((1, GLA_T, LANES), lambda b, t: (b, t, 0)),
                  pl.BlockSpec((LANES, GLA_QK_W), lambda b, t: (0, 0)),
                  pl.BlockSpec((1, GLA_QK_W), lambda b, t: (0, 0)),
                  pl.BlockSpec((1, GLA_V_W), lambda b, t: (0, 0)),
                  pl.BlockSpec((GLA_T, GLA_T), lambda b, t: (0, 0))],
        out_specs=pl.BlockSpec((1, GLA_T, GLA_V_W), lambda b, t: (b, t, 0)),
        out_shape=jax.ShapeDtypeStruct((batch, seq, GLA_V_W), BF16),
        scratch_shapes=[pltpu.VMEM((GLA_HEADS, GLA_DV, GLA_DK), F32),
                        pltpu.VMEM((GLA_HEADS, n_chunks, GLA_DV, GLA_DK), BF16)],
        compiler_params=_cp("parallel", "arbitrary"),
        name="gla",
    )(h3, h3, h3, h3, a3, wup, balpha, gnorm, tri)


def _gla_masks():
    i = jnp.arange(GLA_T)[:, None]
    j = jnp.arange(GLA_T)[None, :]
    same = (i // GLA_CHUNK) == (j // GLA_CHUNK)
    return (same & (j <= i)).astype(BF16)


def _dil_attn_kernel(q_ref, kc_ref, kp_ref, vc_ref, vp_ref, o_ref, lse_ref, *, tq):
    n = pl.program_id(2)
    st = DIL_STEPS
    qi = lax.broadcasted_iota(I32, (st, 2 * st), 0)
    kj = lax.broadcasted_iota(I32, (st, 2 * st), 1)
    in_prev = kj < st
    band_bias = jnp.where(jnp.where(in_prev, kj - qi, qi + st - kj) >= 0, 0.0, NEG_INF)
    prev_cols = jnp.where(in_prev, 1.0, 0.0)
    lane = lax.broadcasted_iota(I32, (st, LANES), 1)
    low = lane < DIL_HD
    ones = jnp.ones((2 * st, LANES), BF16)
    nt = (((1,), (1,)), ((), ()))

    no_prev = jnp.where(n == 0, NEG_INF, 0.0)
    first_bias = band_bias + prev_cols * no_prev

    def window(cur_ref, prev_ref, i, cs):
        if i == 0:
            return jnp.concatenate([prev_ref[0, 0, 0, :, cs], cur_ref[0, 0, 0, 0:st, cs]], axis=0)
        return cur_ref[0, 0, 0, (i - 1) * st:(i + 1) * st, cs]

    quad_w = 2 * LANES
    quad_lane = lax.broadcasted_iota(I32, (st, quad_w), 1)
    head_in_quad = (quad_lane % LANES) // (DIL_HD // 2)

    for i in range(tq // st):
        r0 = i * st
        bias = first_bias if i == 0 else band_bias
        bias4 = jnp.concatenate([bias] * 4, axis=0)
        stats = jnp.zeros((st, LANES), F32)
        outs = []
        for quad in range(DIL_HEADS // 4):
            qs = slice(quad * quad_w, (quad + 1) * quad_w)
            qq = q_ref[0, 0, 0, r0:r0 + st, qs]
            zero = jnp.zeros_like(qq)
            lhs = jnp.concatenate([jnp.where(head_in_quad == u, qq, zero) for u in range(4)], axis=0)
            s = lax.dot_general(lhs, window(kc_ref, kp_ref, i, qs), nt,
                                preferred_element_type=F32) + bias4
            m = jnp.max(s, axis=-1, keepdims=True)
            pexp = jnp.exp(s - m).astype(BF16)
            m_rep = jnp.broadcast_to(m, (4 * st, LANES))
            for pair in range(2):
                p = 2 * quad + pair
                cs = slice(p * LANES, (p + 1) * LANES)
                v_ext = jnp.concatenate([window(vc_ref, vp_ref, i, cs), ones], axis=1)
                o = jnp.dot(pexp[2 * pair * st:(2 * pair + 2) * st], v_ext,
                            preferred_element_type=F32)
                outs.append(jnp.where(low, o[0:st, 0:LANES], o[st:2 * st, 0:LANES]))
                l_rep = o[:, LANES:2 * LANES]
                for half in range(2):
                    rs = slice(half * st, (half + 1) * st)
                    ms = slice((2 * pair + half) * st, (2 * pair + half + 1) * st)
                    stats = jnp.where(lane == 2 * p + half, m_rep[ms], stats)
                    stats = jnp.where(lane == DIL_HEADS + 2 * p + half, l_rep[rs], stats)
        o_ref[0, 0, r0:r0 + st, :] = jnp.concatenate(outs, axis=1).astype(o_ref.dtype)
        lse_ref[0, 0, r0:r0 + st, :] = stats


def _dil_attn(qkv, batch, seq, dil):
    sub = seq // dil
    tq = min(ATTN_TQ, sub)
    per = tq // DIL_STEPS

    def cur(which):
        return pl.BlockSpec((1, 1, 1, tq, DIL_GROUP_W), lambda b, r, n: (which, b, r, n, 0))

    def prev(which):
        return pl.BlockSpec((1, 1, 1, DIL_STEPS, DIL_GROUP_W),
                            lambda b, r, n: (which, b, r, jnp.maximum(n * per - 1, 0), 0))

    return pl.pallas_call(
        functools.partial(_dil_attn_kernel, tq=tq),
        grid=(batch, dil, sub // tq),
        in_specs=[cur(0), cur(1), prev(1), cur(2), prev(2)],
        out_specs=[pl.BlockSpec((1, 1, tq, DIL_GROUP_W), lambda b, r, n: (b, r, n, 0)),
                   pl.BlockSpec((1, 1, tq, LANES), lambda b, r, n: (b, r, n, 0))],
        out_shape=[jax.ShapeDtypeStruct((batch, dil, sub, DIL_GROUP_W), BF16),
                   jax.ShapeDtypeStruct((batch, dil, sub, LANES), F32)],
        compiler_params=_cp("parallel", "parallel", "arbitrary"),
        name=f"dil_attn{dil}",
    )(qkv, qkv, qkv, qkv, qkv)


def _store_token_tiles(ref, val):
    rows = val.shape[0]
    for c in range(TOKEN_ROWS):
        ref[pl.ds(c, rows, stride=TOKEN_ROWS), :] = val[:, c * LANES:(c + 1) * LANES]


def _load_token_tiles(ref, rows):
    return jnp.concatenate([ref[pl.ds(c, rows, stride=TOKEN_ROWS), :] for c in range(TOKEN_ROWS)], axis=1)


def _layer_norm(x, g, b):
    mu = jnp.mean(x, axis=-1, keepdims=True)
    var = jnp.mean(jnp.square(x - mu), axis=-1, keepdims=True)
    return (x - mu) * lax.rsqrt(var + LN_EPS) * g + b


def _mix_kernel(og_ref, g1_ref, g2_ref, o0_ref, o1_ref, o2_ref, l0_ref, l1_ref, l2_ref, x_ref,
                wgo_ref, wao_ref, wout_ref, lng_ref, lnb_ref, wrt_ref, br_ref, expand_ref,
                x1_ref, bucket_ref, osc_ref, lsc_ref):
    tm = x_ref.shape[0]

    def interleaved(o_ref, l_ref, dil, slot):
        if dil == 1:
            return o_ref[0, 0].astype(F32), l_ref[0, 0]
        rows = tm // dil
        chunks = DIL_GROUP_W // LANES
        for r in range(dil):
            for c in range(chunks):
                osc_ref[slot, c, pl.ds(r, rows, stride=dil), :] = (
                    o_ref[0, r, :, c * LANES:(c + 1) * LANES].astype(F32))
            lsc_ref[slot, pl.ds(r, rows, stride=dil), :] = l_ref[0, r]
        return jnp.concatenate([osc_ref[slot, c] for c in range(chunks)], axis=1), lsc_ref[slot]

    groups = [interleaved(o_ref, l_ref, dil, slot)
              for slot, (o_ref, l_ref, (_, dil)) in enumerate(
                  zip((o0_ref, o1_ref, o2_ref), (l0_ref, l1_ref, l2_ref), DIL_PATTERN))]
    stats = [g[1] for g in groups]
    m_max = jnp.maximum(jnp.maximum(stats[0], stats[1]), stats[2])
    es = [jnp.exp(s - m_max) for s in stats]
    ls = [pltpu.roll(s, LANES - DIL_HEADS, 1) for s in stats]
    inv = 1.0 / (es[0] * ls[0] + es[1] * ls[1] + es[2] * ls[2])
    head_lane = lax.broadcasted_iota(I32, (tm, LANES), 1) < DIL_HEADS
    att = jnp.zeros((tm, DIL_GROUP_W), F32)
    for (num_g, _), e in zip(groups, es):
        coef = jnp.where(head_lane, e * inv, 0.0)
        att = att + _split_dot_rhs(coef, expand_ref[...]) * num_g

    y_att = jnp.dot(att.astype(BF16), wao_ref[...], preferred_element_type=F32)
    y_gla = jnp.dot(og_ref[...], wgo_ref[...], preferred_element_type=F32)
    mixed = (jax.nn.sigmoid(g1_ref[...].astype(F32)) * y_gla
             + jax.nn.sigmoid(g2_ref[...].astype(F32)) * y_att)
    mix = jnp.dot(mixed.astype(BF16), wout_ref[...], preferred_element_type=F32)
    x1 = _layer_norm(DN_ALPHA * x_ref[...] + mix, lng_ref[...], lnb_ref[...])
    _store_token_tiles(x1_ref, x1)

    logits = lax.dot_general(wrt_ref[...], x1.astype(BF16), (((1,), (1,)), ((), ())),
                             preferred_element_type=F32) + br_ref[:, 0:1]
    e = jnp.exp(logits - jnp.max(logits, axis=0, keepdims=True))
    probs = e / jnp.sum(e, axis=0, keepdims=True)
    best = None
    for g in range(N_GROUPS):
        for pi, (ea, eb_) in enumerate(PAIRS):
            cand = probs[4 * g + ea:4 * g + ea + 1] + probs[4 * g + eb_:4 * g + eb_ + 1]
            idx = g * len(PAIRS) + pi
            if best is None:
                best, bidx = cand, jnp.zeros(cand.shape, I32)
            else:
                better = cand > best
                best = jnp.where(better, cand, best)
                bidx = jnp.where(better, idx, bidx)
    bucket_ref[0] = bidx


def _split_dot_rhs(val_f32, mat_bf16):
    hi = val_f32.astype(BF16)
    lo = (val_f32 - hi.astype(F32)).astype(BF16)
    return (jnp.dot(hi, mat_bf16, preferred_element_type=F32)
            + jnp.dot(lo, mat_bf16, preferred_element_type=F32))


def _mix(og, h_main, attn, x, wgo, wao, wout, lng, lnb, wrt, br, expand, batch, seq):
    n = batch * seq
    tm = min(MIX_TM, seq)
    tps = seq // tm
    row = lambda i: (i, 0)
    const = lambda i: (0, 0)
    o_specs, l_specs, o_args, l_args = [], [], [], []
    for (o_g, lse_g), (_, dil) in zip(attn, DIL_PATTERN):
        o_specs.append(pl.BlockSpec((1, dil, tm // dil, DIL_GROUP_W), lambda i: (i // tps, 0, i % tps, 0)))
        l_specs.append(pl.BlockSpec((1, dil, tm // dil, LANES), lambda i: (i // tps, 0, i % tps, 0)))
        o_args.append(o_g)
        l_args.append(lse_g)
    gate_blk = GATE_COL // D_MODEL
    return pl.pallas_call(
        _mix_kernel,
        grid=(n // tm,),
        in_specs=[pl.BlockSpec((tm, GLA_V_W), row),
                  pl.BlockSpec((tm, D_MODEL), lambda i: (i, gate_blk)),
                  pl.BlockSpec((tm, D_MODEL), lambda i: (i, gate_blk + 1)),
                  *o_specs, *l_specs,
                  pl.BlockSpec((tm, D_MODEL), row),
                  pl.BlockSpec((GLA_V_W, D_MODEL), const),
                  pl.BlockSpec((DIL_GROUP_W, D_MODEL), const),
                  pl.BlockSpec((D_MODEL, D_MODEL), const),
                  pl.BlockSpec((1, D_MODEL), const),
                  pl.BlockSpec((1, D_MODEL), const),
                  pl.BlockSpec((N_EXPERTS, D_MODEL), const),
                  pl.BlockSpec((N_EXPERTS, LANES), const),
                  pl.BlockSpec((LANES, DIL_GROUP_W), const)],
        out_specs=[pl.BlockSpec((tm * TOKEN_ROWS, LANES), row),
                   pl.BlockSpec((1, 1, tm), lambda i: (i, 0, 0))],
        out_shape=[jax.ShapeDtypeStruct((n * TOKEN_ROWS, LANES), F32),
                   jax.ShapeDtypeStruct((n // tm, 1, tm), I32)],
        scratch_shapes=[pltpu.VMEM((N_DIL, DIL_GROUP_W // LANES, tm, LANES), F32),
                        pltpu.VMEM((N_DIL, tm, LANES), F32)],
        compiler_params=_cp("parallel"),
        name="mix",
    )(og, h_main, h_main, *o_args, *l_args, x, wgo, wao, wout, lng, lnb, wrt, br, expand)


def _rank_kernel(bucket_ref, upper_ref, rank_ref, counts_ref, carry_ref):
    @pl.when(pl.program_id(0) == 0)
    def _():
        carry_ref[...] = jnp.zeros_like(carry_ref)

    b = bucket_ref[0]
    ts = b.shape[1]
    onehot = lax.broadcasted_iota(I32, (BUCKET_ROWS, ts), 0) == b
    oh_f = jnp.where(onehot, 1.0, 0.0)
    before = jnp.dot(oh_f.astype(BF16), upper_ref[...], preferred_element_type=F32)
    carry = carry_ref[...]
    rank = jnp.sum(oh_f * (before + carry[:, 0:1]), axis=0, keepdims=True)
    rank_ref[0] = rank.astype(I32)
    carry = carry + jnp.sum(oh_f, axis=1, keepdims=True)
    carry_ref[...] = carry
    counts_ref[...] = carry.astype(I32)


def _bucket_ranks(bucket, n):
    ts = min(SORT_TS, n)
    b3 = bucket.reshape(n // ts, 1, ts)
    j = jnp.arange(ts)
    upper = (j[:, None] < j[None, :]).astype(BF16)
    rank, counts = pl.pallas_call(
        _rank_kernel,
        grid=(n // ts,),
        in_specs=[pl.BlockSpec((1, 1, ts), lambda i: (i, 0, 0)),
                  pl.BlockSpec((ts, ts), lambda i: (0, 0))],
        out_specs=[pl.BlockSpec((1, 1, ts), lambda i: (i, 0, 0)),
                   pl.BlockSpec((BUCKET_ROWS, LANES), lambda i: (0, 0))],
        out_shape=[jax.ShapeDtypeStruct((n // ts, 1, ts), I32),
                   jax.ShapeDtypeStruct((BUCKET_ROWS, LANES), I32)],
        scratch_shapes=[pltpu.VMEM((BUCKET_ROWS, LANES), F32)],
        compiler_params=_cp("arbitrary"),
        name="bucket_rank",
    )(b3, upper)
    return rank.reshape(n), counts[:N_BUCKETS, 0]


def _sort_plan(bucket, rank, counts, n):
    tiles = (counts + MOE_T - 1) // MOE_T
    tile_end = jnp.cumsum(tiles)
    offsets = (tile_end - tiles) * MOE_T
    dest = offsets[bucket] + rank
    n_tiles = n // MOE_T + N_BUCKETS
    tile_ids = jnp.arange(n_tiles, dtype=I32)
    total = tile_end[-1]
    last = jnp.minimum(tile_ids, total - 1)
    tb = jnp.sum((last[:, None] >= tile_end[None, :]).astype(I32), axis=1)
    tb = jnp.minimum(tb, N_BUCKETS - 1)
    n_valid = jnp.clip(counts[tb] - (tile_ids - (tile_end - tiles)[tb]) * MOE_T, 0, MOE_T)
    n_valid = jnp.where(tile_ids < total, n_valid, 0)
    pair = jnp.array(PAIRS, dtype=I32)
    e_lo = (tb // len(PAIRS)) * EXPERTS_PER_GROUP + pair[tb % len(PAIRS), 0]
    e_hi = (tb // len(PAIRS)) * EXPERTS_PER_GROUP + pair[tb % len(PAIRS), 1]
    pad_lo = jnp.concatenate([offsets + counts, total[None] * MOE_T])
    pad_hi = jnp.concatenate([tile_end * MOE_T, jnp.full((1,), n_tiles * MOE_T, I32)])
    pad_ranges = jnp.stack([pad_lo, pad_hi]).astype(I32)
    return dest.astype(I32), tb, e_lo, e_hi, n_valid.astype(I32), pad_ranges


def _invert_kernel(pad_ref, dest_ref, src_ref, table_ref, sem, *, chunk):
    i = pl.program_id(0)

    @pl.when(i == 0)
    def _():
        def clear(j, carry):
            table_ref[j] = 0
            return carry
        for b in range(N_BUCKETS + 1):
            lax.fori_loop(pad_ref[0, b], pad_ref[1, b], clear, 0)

    def put(j, carry):
        table_ref[dest_ref[0, 0, j]] = i * chunk + j
        return carry
    lax.fori_loop(0, chunk, put, 0, unroll=8)

    @pl.when(i == pl.num_programs(0) - 1)
    def _():
        copy = pltpu.make_async_copy(table_ref, src_ref, sem)
        copy.start()
        copy.wait()


def _invert(dest, pad_ranges, n_slots):
    n = dest.shape[0]
    chunk = min(2048, n)
    return pl.pallas_call(
        functools.partial(_invert_kernel, chunk=chunk),
        grid=(n // chunk,),
        in_specs=[pl.BlockSpec(memory_space=pltpu.SMEM),
                  pl.BlockSpec((1, 1, chunk), lambda i: (i, 0, 0), memory_space=pltpu.SMEM)],
        out_specs=pl.BlockSpec(memory_space=pl.ANY),
        out_shape=jax.ShapeDtypeStruct((n_slots,), I32),
        scratch_shapes=[pltpu.SMEM((n_slots,), I32), pltpu.SemaphoreType.DMA(())],
        compiler_params=_cp("arbitrary"),
        name="invert_perm",
    )(pad_ranges, dest.reshape(n // chunk, 1, chunk))


def _moe_kernel(tb_ref, elo_ref, ehi_ref, nvalid_ref, prv_ref, src_ref, nxt_ref, x_hbm, wr2_ref, br2_ref,
                wg_lo_ref, wu_lo_ref, wd_lo_ref, wg_hi_ref, wu_hi_ref, wd_hi_ref, lng_ref, lnb_ref,
                out_hbm, xbuf_ref, ybuf_ref, gsem, ssem, *, n_tokens):
    del tb_ref, elo_ref, ehi_ref
    t = pl.program_id(0)
    last = pl.num_programs(0) - 1
    slot = t % 2
    other = 1 - slot
    tile = MOE_T
    n_chunks = 8
    per_chunk = tile // n_chunks

    def token_rows(tok):
        if isinstance(tok, int):
            return pl.ds(tok * TOKEN_ROWS, TOKEN_ROWS)
        return pl.ds(pl.multiple_of(tok * TOKEN_ROWS, TOKEN_ROWS), TOKEN_ROWS)

    def gather_copy(idx_ref, r, sl):
        return pltpu.make_async_copy(x_hbm.at[token_rows(idx_ref[0, 0, r])],
                                     xbuf_ref.at[sl, token_rows(r)], gsem.at[sl])

    def scatter_copy(dst, r, sl):
        return pltpu.make_async_copy(ybuf_ref.at[sl, token_rows(r)], out_hbm.at[token_rows(dst)], ssem.at[sl])

    def scatter_rows(idx_ref, n_valid, rows, sl):
        for r in rows:
            dst = jnp.where(r < n_valid, idx_ref[0, 0, r], n_tokens + sl * tile + r)
            scatter_copy(dst, r, sl).start()

    prev_valid = jnp.where(t > 0, nvalid_ref[jnp.maximum(t - 1, 0)], 0)

    def issue(chunk):
        rows = range(chunk * per_chunk, (chunk + 1) * per_chunk)
        for r in rows:
            gather_copy(nxt_ref, r, other).start()
        scatter_rows(prv_ref, prev_valid, rows, other)

    @pl.when(t == 0)
    def _():
        ybuf_ref[...] = jnp.zeros_like(ybuf_ref)
        for r in range(tile):
            gather_copy(src_ref, r, 0).start()
        scatter_rows(src_ref, 0, range(tile), 0)

    for r in range(tile):
        gather_copy(src_ref, r, slot).wait()
    for r in range(tile):
        scatter_copy(0, r, slot).wait()

    @pl.when(nvalid_ref[t] == 0)
    def _():
        for chunk in range(n_chunks):
            issue(chunk)

    @pl.when(nvalid_ref[t] != 0)
    def _():
        x = _load_token_tiles(xbuf_ref.at[slot], tile)
        xb = x.astype(BF16)
        issue(0)
        l2 = jnp.dot(xb, wr2_ref[0], preferred_element_type=F32) + br2_ref[0]
        g_lo = jax.nn.sigmoid(l2[:, 0:1] - l2[:, 1:2])
        g_hi = jax.nn.sigmoid(l2[:, 1:2] - l2[:, 0:1])

        def expert(wg_ref, wu_ref, wd_ref, first_chunk):
            a = jnp.dot(xb, wg_ref[0], preferred_element_type=F32)
            issue(first_chunk)
            u = jnp.dot(xb, wu_ref[0], preferred_element_type=F32)
            issue(first_chunk + 1)
            hidden = (a * jax.nn.sigmoid(a) * u).astype(BF16)
            out = jnp.dot(hidden, wd_ref[0], preferred_element_type=F32)
            issue(first_chunk + 2)
            return out

        y = g_lo * expert(wg_lo_ref, wu_lo_ref, wd_lo_ref, 1)
        y = y + g_hi * expert(wg_hi_ref, wu_hi_ref, wd_hi_ref, 4)
        _store_token_tiles(ybuf_ref.at[slot], _layer_norm(DN_ALPHA * x + y, lng_ref[...], lnb_ref[...]))
        issue(7)

    @pl.when(t == last)
    def _():
        for r in range(tile):
            gather_copy(nxt_ref, r, other).wait()
        scatter_rows(src_ref, nvalid_ref[t], range(tile), slot)
        for sl in range(2):
            for r in range(tile):
                scatter_copy(0, r, sl).wait()


def _moe(x1, src, plan, wr2, br2, wg, wu, wd, lng, lnb):
    _, tb, e_lo, e_hi, n_valid, _ = plan
    n = x1.shape[0] // TOKEN_ROWS
    n_tiles = src.shape[0] // MOE_T
    src3 = src.reshape(n_tiles, 1, MOE_T)
    lo = lambda t, tb, elo, ehi, v: (elo[t], 0, 0)
    hi = lambda t, tb, elo, ehi, v: (ehi[t], 0, 0)
    const = lambda t, tb, elo, ehi, v: (0, 0)
    wblk = (1, D_MODEL, D_FF)
    idx_blk = (1, 1, MOE_T)
    grid_spec = pltpu.PrefetchScalarGridSpec(
        num_scalar_prefetch=4,
        grid=(n_tiles,),
        in_specs=[pl.BlockSpec(idx_blk, lambda t, tb, elo, ehi, v: (jnp.maximum(t - 1, 0), 0, 0),
                               memory_space=pltpu.SMEM),
                  pl.BlockSpec(idx_blk, lambda t, tb, elo, ehi, v: (t, 0, 0), memory_space=pltpu.SMEM),
                  pl.BlockSpec(idx_blk, lambda t, tb, elo, ehi, v: (jnp.minimum(t + 1, n_tiles - 1), 0, 0),
                               memory_space=pltpu.SMEM),
                  pl.BlockSpec(memory_space=pl.ANY),
                  pl.BlockSpec((1, D_MODEL, LANES), lambda t, tb, elo, ehi, v: (tb[t], 0, 0)),
                  pl.BlockSpec((1, 1, LANES), lambda t, tb, elo, ehi, v: (tb[t], 0, 0)),
                  pl.BlockSpec(wblk, lo), pl.BlockSpec(wblk, lo), pl.BlockSpec((1, D_FF, D_MODEL), lo),
                  pl.BlockSpec(wblk, hi), pl.BlockSpec(wblk, hi), pl.BlockSpec((1, D_FF, D_MODEL), hi),
                  pl.BlockSpec((1, D_MODEL), const), pl.BlockSpec((1, D_MODEL), const)],
        out_specs=pl.BlockSpec(memory_space=pl.ANY),
        scratch_shapes=[pltpu.VMEM((2, MOE_T * TOKEN_ROWS, LANES), F32),
                        pltpu.VMEM((2, MOE_T * TOKEN_ROWS, LANES), F32),
                        pltpu.SemaphoreType.DMA((2,)), pltpu.SemaphoreType.DMA((2,))],
    )
    return pl.pallas_call(
        functools.partial(_moe_kernel, n_tokens=n),
        grid_spec=grid_spec,
        out_shape=jax.ShapeDtypeStruct(((n + 2 * MOE_T) * TOKEN_ROWS, LANES), F32),
        compiler_params=_cp("arbitrary"),
        name="moe",
    )(tb, e_lo, e_hi, n_valid, src3, src3, src3, x1, wr2, br2, wg, wu, wd, wg, wu, wd, lng, lnb)


def _untile_kernel(xt_ref, x_ref, xb_ref):
    x = _load_token_tiles(xt_ref, x_ref.shape[0])
    x_ref[...] = x
    xb_ref[...] = x.astype(BF16)


def _untile(xt, n):
    tm = min(PROJ_TM, n)
    return pl.pallas_call(
        _untile_kernel,
        grid=(n // tm,),
        in_specs=[pl.BlockSpec((tm * TOKEN_ROWS, LANES), lambda i: (i, 0))],
        out_specs=[pl.BlockSpec((tm, D_MODEL), lambda i: (i, 0))] * 2,
        out_shape=[jax.ShapeDtypeStruct((n, D_MODEL), F32), jax.ShapeDtypeStruct((n, D_MODEL), BF16)],
        compiler_params=_cp("parallel"),
        name="untile",
    )(xt)


def _split_w_in(w_in_l):
    sizes = (GLA_QK_W, GLA_QK_W, GLA_V_W, GLA_V_W, GLA_RANK,
             N_DIL * DIL_GROUP_W, N_DIL * DIL_GROUP_W, N_DIL * DIL_GROUP_W, 2 * D_MODEL)
    starts = [sum(sizes[:i]) for i in range(len(sizes))]
    blk = [w_in_l[:, s:s + z] for s, z in zip(starts, sizes)]
    q_g, k_g, v_g, r_g, a_g, q_d, k_d, v_d, gates = blk
    w_main = jnp.concatenate([q_g, k_g, v_g, r_g, gates], axis=1).astype(BF16)
    w_a = jnp.pad(a_g, ((0, 0), (0, LANES - GLA_RANK))).astype(BF16)
    quad = jnp.asarray(_quad_layout())
    w_dil = []
    for g in range(N_DIL):
        gs = slice(g * DIL_GROUP_W, (g + 1) * DIL_GROUP_W)
        w_dil.append(jnp.concatenate([q_d[:, gs][:, quad], k_d[:, gs][:, quad], v_d[:, gs]],
                                     axis=1).astype(BF16))
    return w_main, w_a, w_dil


def _quad_layout():
    half = DIL_HD // 2
    cols = []
    for quad in range(DIL_HEADS // 4):
        for part in range(2):
            for u in range(4):
                start = (4 * quad + u) * DIL_HD + part * half
                cols.extend(range(start, start + half))
    return cols


def _router_pair_tables(w_router, b_router):
    cols_w, cols_b = [], []
    for g in range(N_GROUPS):
        for ea, eb_ in PAIRS:
            lo, hi = EXPERTS_PER_GROUP * g + ea, EXPERTS_PER_GROUP * g + eb_
            w2 = jnp.stack([w_router[:, lo], w_router[:, hi]], axis=1)
            cols_w.append(jnp.pad(w2, ((0, 0), (0, LANES - 2))))
            cols_b.append(jnp.pad(jnp.stack([b_router[lo], b_router[hi]]), (0, LANES - 2)))
    return jnp.stack(cols_w).astype(BF16), jnp.stack(cols_b).reshape(N_BUCKETS, 1, LANES).astype(F32)


def kernel(x, positions, w_in, w_alpha_up, b_alpha, gla_norm, w_gla_o, w_att_o, w_out, ln1_g, ln1_b,
           w_router, b_router, w_e_gate, w_e_up, w_e_down, ln2_g, ln2_b):
    batch, seq, d_model = x.shape
    depth = w_in.shape[0]
    n = batch * seq
    assert d_model == D_MODEL and seq % (DIL_PATTERN[-1][1] * DIL_STEPS) == 0

    cos_t, sin_t = _rope_tables(positions)
    tri = _gla_masks()
    head_of_col = jnp.arange(DIL_GROUP_W) // DIL_HD
    expand = (jnp.arange(LANES)[:, None] == head_of_col[None, :]).astype(BF16)
    wrt = w_router.T.astype(BF16)
    br = jnp.broadcast_to(b_router.astype(F32)[:, None], (N_EXPERTS, LANES))
    wr2, br2 = _router_pair_tables(w_router, b_router)
    n_slots = n + N_BUCKETS * MOE_T

    xf = xin = x.reshape(n, D_MODEL)
    for i in range(depth):
        w_main, w_a, w_dil = _split_w_in(w_in[i])
        h_main = _proj(xin, w_main, BF16, PROJ_TN, n)
        a_proj = _proj(xin, w_a, F32, LANES, n)
        wup = jnp.pad(w_alpha_up[i], ((0, LANES - GLA_RANK), (0, 0))).astype(BF16)
        og = _gla(h_main, a_proj, wup, b_alpha[i].reshape(1, -1).astype(F32),
                  gla_norm[i].reshape(1, -1).astype(F32), tri, batch, seq)
        attn = []
        for g, (_, dil) in enumerate(DIL_PATTERN):
            qkv = _proj_dil(xin, w_dil[g], cos_t, sin_t, batch, seq, dil)
            attn.append(_dil_attn(qkv, batch, seq, dil))
        x1, bucket = _mix(og.reshape(n, GLA_V_W), h_main, attn, xf,
                          w_gla_o[i].astype(BF16), w_att_o[i].astype(BF16), w_out[i].astype(BF16),
                          ln1_g[i].reshape(1, -1), ln1_b[i].reshape(1, -1), wrt, br, expand, batch, seq)
        bucket = bucket.reshape(n)
        rank, counts = _bucket_ranks(bucket, n)
        plan = _sort_plan(bucket, rank, counts, n)
        src = _invert(plan[0], plan[5], n_slots)
        xt = _moe(x1, src, plan, wr2, br2, w_e_gate[i].astype(BF16), w_e_up[i].astype(BF16),
                  w_e_down[i].astype(BF16), ln2_g[i].reshape(1, -1), ln2_b[i].reshape(1, -1))
        xf, xin = _untile(xt, n)
    return xf.reshape(batch, seq, D_MODEL)
```

```python
import functools

import jax
import jax.numpy as jnp
from jax import lax
from jax.experimental import pallas as pl
from jax.experimental.pallas import tpu as pltpu

F32 = jnp.float32
BF16 = jnp.bfloat16
I32 = jnp.int32

D_MODEL = 1024
N_LAYERS = 4
GLA_HEADS = 4
GLA_DK = 128
GLA_DV = 256
GLA_RANK = 16
GLA_TAU = 16.0
GLA_CHUNK = 64
DIL_PATTERN = ((128, 1), (512, 4), (2048, 16))
DIL_HEADS = 8
DIL_HD = 64
DIL_STEPS = 128
ROPE_THETA = 10000.0
N_EXPERTS = 16
N_GROUPS = 4
EXPERTS_PER_GROUP = 4
D_FF = 1024
LN_EPS = 1e-5
RMS_EPS = 1e-6
DN_ALPHA = (2 * N_LAYERS) ** 0.25
NEG_INF = -1e30

GLA_QK_W = GLA_HEADS * GLA_DK
GLA_V_W = GLA_HEADS * GLA_DV
DIL_GROUP_W = DIL_HEADS * DIL_HD
N_DIL = len(DIL_PATTERN)
MAIN_W = 2 * GLA_QK_W + 2 * GLA_V_W + 2 * D_MODEL
GATE_COL = 2 * GLA_QK_W + 2 * GLA_V_W

LANES = 128
TOKEN_ROWS = D_MODEL // LANES
PAIRS = ((0, 1), (0, 2), (0, 3), (1, 2), (1, 3), (2, 3))
N_BUCKETS = N_GROUPS * len(PAIRS)
BUCKET_ROWS = 32

PROJ_TM = 1024
PROJ_TN = 1024
GLA_T = 512
MIX_TM = 512
SORT_TS = 1024
MOE_T = 256
ATTN_TQ = 512
VMEM_LIMIT = 56 * 1024 * 1024


def _cp(*sem):
    return pltpu.CompilerParams(dimension_semantics=sem, vmem_limit_bytes=VMEM_LIMIT)


def _rope_kernel(pos_ref, invf_ref, cos_ref, sin_ref):
    ang = pos_ref[...].astype(F32) * invf_ref[...]
    cos_ref[...] = jnp.cos(ang)
    sin_ref[...] = jnp.sin(ang)


def _rope_tables(positions):
    n = positions.size
    tm = 2048
    half = DIL_HD // 2
    inv_freq = jnp.power(jnp.float32(ROPE_THETA), -jnp.arange(0, DIL_HD, 2, dtype=F32) / DIL_HD)
    invf = jnp.tile(inv_freq, LANES // half).reshape(1, LANES)
    return pl.pallas_call(
        _rope_kernel,
        grid=(n // tm,),
        in_specs=[pl.BlockSpec((tm, 1), lambda i: (i, 0)),
                  pl.BlockSpec((1, LANES), lambda i: (0, 0))],
        out_specs=[pl.BlockSpec((tm, LANES), lambda i: (i, 0)),
                   pl.BlockSpec((tm, LANES), lambda i: (i, 0))],
        out_shape=[jax.ShapeDtypeStruct((n, LANES), F32)] * 2,
        compiler_params=_cp("parallel"),
        name="rope_tables",
    )(positions.reshape(n, 1), invf)


def _proj_kernel(x_ref, w_ref, o_ref, xb_ref):
    @pl.when(pl.program_id(1) == 0)
    def _():
        xb_ref[...] = x_ref[...].astype(BF16)

    o_ref[...] = jnp.dot(xb_ref[...], w_ref[...], preferred_element_type=F32).astype(o_ref.dtype)


def _proj(x, w, out_dtype, tn, n):
    k = x.shape[1]
    ncol = w.shape[1]
    tm = min(PROJ_TM, n)
    return pl.pallas_call(
        _proj_kernel,
        grid=(n // tm, ncol // tn),
        in_specs=[pl.BlockSpec((tm, k), lambda i, j: (i, 0)),
                  pl.BlockSpec((k, tn), lambda i, j: (0, j))],
        out_specs=pl.BlockSpec((tm, tn), lambda i, j: (i, j)),
        out_shape=jax.ShapeDtypeStruct((n, ncol), out_dtype),
        scratch_shapes=[pltpu.VMEM((tm, k), BF16)],
        compiler_params=_cp("parallel", "arbitrary"),
        name="proj",
    )(x, w)


def _proj_dil_kernel(x_ref, w_ref, cos_ref, sin_ref, o_ref, *slab_refs, dil, tm):
    acc_ref = slab_refs[0] if dil > 1 else None
    acc = jnp.dot(x_ref[...].astype(BF16), w_ref[...], preferred_element_type=F32)
    slabs = DIL_GROUP_W // LANES

    def emit(which, slab, val):
        if dil == 1:
            o_ref[which, 0, 0, :, slab * LANES:(slab + 1) * LANES] = val.astype(BF16)
        else:
            acc_ref[which * slabs + slab] = val

    def col(which, slab):
        start = which * DIL_GROUP_W + slab * LANES
        return acc[:, start:start + LANES]

    for which in range(2):
        scale = DIL_HD ** -0.5 if which == 0 else 1.0
        cos_t = cos_ref[...] * scale
        sin_t = sin_ref[...] * scale
        for quad in range(slabs // 2):
            x1, x2 = col(which, 2 * quad), col(which, 2 * quad + 1)
            emit(which, 2 * quad, x1 * cos_t - x2 * sin_t)
            emit(which, 2 * quad + 1, x2 * cos_t + x1 * sin_t)
    for slab in range(slabs):
        emit(2, slab, col(2, slab))

    if dil > 1:
        rows = tm // dil
        step = 4 if dil > 4 else dil
        outer = dil // step
        part = tm // step
        for s in range(3 * slabs):
            which, slab = divmod(s, slabs)
            if outer > 1:
                tmp_ref = slab_refs[1]
                for lo in range(step):
                    tmp_ref[s, lo * part:(lo + 1) * part, :] = acc_ref[s, pl.ds(lo, part, stride=step), :]
            for r in range(dil):
                hi, lo = divmod(r, step)
                if outer > 1:
                    val = tmp_ref[s, pl.ds(lo * part + hi, rows, stride=outer), :]
                else:
                    val = acc_ref[s, pl.ds(r, rows, stride=dil), :]
                o_ref[which, 0, r, :, slab * LANES:(slab + 1) * LANES] = val.astype(BF16)


def _proj_dil(x, w, cos_t, sin_t, batch, seq, dil):
    n, k = batch * seq, x.shape[1]
    tm = min(PROJ_TM, seq)
    tiles_per_seq = seq // tm
    sub = seq // dil
    kern = functools.partial(_proj_dil_kernel, dil=dil, tm=tm)
    n_slabs = 3 * DIL_GROUP_W // LANES
    return pl.pallas_call(
        kern,
        grid=(n // tm,),
        in_specs=[pl.BlockSpec((tm, k), lambda i: (i, 0)),
                  pl.BlockSpec((k, 3 * DIL_GROUP_W), lambda i: (0, 0)),
                  pl.BlockSpec((tm, LANES), lambda i: (i, 0)),
                  pl.BlockSpec((tm, LANES), lambda i: (i, 0))],
        out_specs=pl.BlockSpec((3, 1, dil, tm // dil, DIL_GROUP_W),
                               lambda i: (0, i // tiles_per_seq, 0, i % tiles_per_seq, 0)),
        out_shape=jax.ShapeDtypeStruct((3, batch, dil, sub, DIL_GROUP_W), BF16),
        scratch_shapes=[pltpu.VMEM((n_slabs, tm, LANES), F32)] * ((dil > 1) + (dil > 4)),
        compiler_params=_cp("parallel"),
        name=f"proj_dil{dil}",
    )(x, w, cos_t, sin_t)


def _split_dot(mat_bf16, val_f32):
    hi = val_f32.astype(BF16)
    lo = (val_f32 - hi.astype(F32)).astype(BF16)
    return (jnp.dot(mat_bf16, hi, preferred_element_type=F32)
            + jnp.dot(mat_bf16, lo, preferred_element_type=F32))


def _gla_kernel(q_ref, k_ref, v_ref, r_ref, a_ref, wup_ref, balpha_ref, gnorm_ref, tri_ref,
                o_ref, state_ref, sprev_ref):
    t = pl.program_id(1)
    n_chunks = GLA_T // GLA_CHUNK

    @pl.when(t == 0)
    def _():
        state_ref[...] = jnp.zeros_like(state_ref)

    z = jnp.dot(a_ref[0].astype(BF16), wup_ref[...], preferred_element_type=F32) + balpha_ref[...]
    log_a = (jnp.minimum(z, 0.0) - jnp.log1p(jnp.exp(-jnp.abs(z)))) / GLA_TAU
    tri = tri_ref[...]
    b = _split_dot(tri, log_a)
    b_end = jnp.concatenate(
        [jnp.broadcast_to(b[(n + 1) * GLA_CHUNK - 1:(n + 1) * GLA_CHUNK], (GLA_CHUNK, GLA_QK_W))
         for n in range(n_chunks)], axis=0)
    c = b_end - b
    eb = jnp.exp(b)
    kf = k_ref[0].astype(F32)
    qt = (q_ref[0].astype(F32) * eb * (GLA_DK ** -0.5)).astype(BF16)
    kt = (kf * jnp.exp(-b)).astype(BF16)
    ks = (kf * jnp.exp(c)).astype(BF16)
    causal = tri > 0

    for h in range(GLA_HEADS):
        ck = slice(h * GLA_DK, (h + 1) * GLA_DK)
        cv = slice(h * GLA_DV, (h + 1) * GLA_DV)
        qt_h, kt_h, ks_h = qt[:, ck], kt[:, ck], ks[:, ck]
        v_h = v_ref[0, :, cv]
        att = lax.dot_general(qt_h, kt_h, (((1,), (1,)), ((), ())), preferred_element_type=F32)
        att = jnp.where(causal, att, 0.0).astype(BF16)
        o_h = jnp.dot(att, v_h, preferred_element_type=F32)

        state = state_ref[h]
        for n in range(n_chunks):
            rs = slice(n * GLA_CHUNK, (n + 1) * GLA_CHUNK)
            contrib = lax.dot_general(v_h[rs], ks_h[rs], (((0,), (0,)), ((), ())),
                                      preferred_element_type=F32)
            sprev_ref[h, n] = state.astype(BF16)
            decay = eb[(n + 1) * GLA_CHUNK - 1:(n + 1) * GLA_CHUNK, ck]
            state = state * decay + contrib
        state_ref[h] = state

        inter = [lax.dot_general(qt_h[n * GLA_CHUNK:(n + 1) * GLA_CHUNK], sprev_ref[h, n],
                                 (((1,), (1,)), ((), ())), preferred_element_type=F32)
                 for n in range(n_chunks)]
        o_h = o_h + jnp.concatenate(inter, axis=0)

        o_h = o_h * lax.rsqrt(jnp.mean(jnp.square(o_h), axis=-1, keepdims=True) + RMS_EPS)
        o_h = o_h * gnorm_ref[:, cv]
        r_h = r_ref[0, :, cv].astype(F32)
        o_ref[0, :, cv] = (o_h * (r_h * jax.nn.sigmoid(r_h))).astype(o_ref.dtype)


def _gla(h_main, a_proj, wup, balpha, gnorm, tri, batch, seq):
    h3 = h_main.reshape(batch, seq, MAIN_W)
    a3 = a_proj.reshape(batch, seq, LANES)
    n_chunks = GLA_T // GLA_CHUNK
    return pl.pallas_call(
        _gla_kernel,
        grid=(batch, seq // GLA_T),
        in_specs=[pl.BlockSpec((1, GLA_T, GLA_QK_W), lambda b, t: (b, t, 0)),
                  pl.BlockSpec((1, GLA_T, GLA_QK_W), lambda b, t: (b, t, 1)),
                  pl.BlockSpec((1, GLA_T, GLA_V_W), lambda b, t: (b, t, 1)),
                  pl.BlockSpec((1, GLA_T, GLA_V_W), lambda b, t: (b, t, 2)),
                  pl.BlockSpec((1, GLA_T, LANES), lambda b, t: (b, t, 0)),
                  pl.BlockSpec((LANES, GLA_QK_W), lambda b, t: (0, 0)),
                  pl.BlockSpec((1, GLA_QK_W), lambda b, t: (0, 0)),
                  pl.BlockSpec((1, GLA_V_W), lambda b, t: (0, 0)),
                  pl.BlockSpec((GLA_T, GLA_T), lambda b, t: (0, 0))],
        out_specs=pl.BlockSpec((1, GLA_T, GLA_V_W), lambda b, t: (b, t, 0)),
        out_shape=jax.ShapeDtypeStruct((batch, seq, GLA_V_W), BF16),
        scratch_shapes=[pltpu.VMEM((GLA_HEADS, GLA_DV, GLA_DK), F32),
                        pltpu.VMEM((GLA_HEADS, n_chunks, GLA_DV, GLA_DK), BF16)],
        compiler_params=_cp("parallel", "arbitrary"),
        name="gla",
    )(h3, h3, h3, h3, a3, wup, balpha, gnorm, tri)


def _gla_masks():
    i = jnp.arange(GLA_T)[:, None]
    j = jnp.arange(GLA_T)[None, :]
    same = (i // GLA_CHUNK) == (j // GLA_CHUNK)
    return (same & (j <= i)).astype(BF16)


def _dil_attn_kernel(q_ref, kc_ref, kp_ref, vc_ref, vp_ref, o_ref, lse_ref, *, tq):
    n = pl.program_id(2)
    st = DIL_STEPS
    qi = lax.broadcasted_iota(I32, (st, 2 * st), 0)
    kj = lax.broadcasted_iota(I32, (st, 2 * st), 1)
    in_prev = kj < st
    band_bias = jnp.where(jnp.where(in_prev, kj - qi, qi + st - kj) >= 0, 0.0, NEG_INF)
    prev_cols = jnp.where(in_prev, 1.0, 0.0)
    lane = lax.broadcasted_iota(I32, (st, LANES), 1)
    low = lane < DIL_HD
    ones = jnp.ones((2 * st, LANES), BF16)
    nt = (((1,), (1,)), ((), ()))

    no_prev = jnp.where(n == 0, NEG_INF, 0.0)
    first_bias = band_bias + prev_cols * no_prev

    def window(cur_ref, prev_ref, i, cs):
        if i == 0:
            return jnp.concatenate([prev_ref[0, 0, 0, :, cs], cur_ref[0, 0, 0, 0:st, cs]], axis=0)
        return cur_ref[0, 0, 0, (i - 1) * st:(i + 1) * st, cs]

    quad_w = 2 * LANES
    quad_lane = lax.broadcasted_iota(I32, (st, quad_w), 1)
    head_in_quad = (quad_lane % LANES) // (DIL_HD // 2)

    for i in range(tq // st):
        r0 = i * st
        bias = first_bias if i == 0 else band_bias
        bias4 = jnp.concatenate([bias] * 4, axis=0)
        stats = jnp.zeros((st, LANES), F32)
        outs = []
        for quad in range(DIL_HEADS // 4):
            qs = slice(quad * quad_w, (quad + 1) * quad_w)
            qq = q_ref[0, 0, 0, r0:r0 + st, qs]
            zero = jnp.zeros_like(qq)
            lhs = jnp.concatenate([jnp.where(head_in_quad == u, qq, zero) for u in range(4)], axis=0)
            s = lax.dot_general(lhs, window(kc_ref, kp_ref, i, qs), nt,
                                preferred_element_type=F32) + bias4
            m = jnp.max(s, axis=-1, keepdims=True)
            pexp = jnp.exp(s - m).astype(BF16)
            m_rep = jnp.broadcast_to(m, (4 * st, LANES))
            for pair in range(2):
                p = 2 * quad + pair
                cs = slice(p * LANES, (p + 1) * LANES)
                v_ext = jnp.concatenate([window(vc_ref, vp_ref, i, cs), ones], axis=1)
                o = jnp.dot(pexp[2 * pair * st:(2 * pair + 2) * st], v_ext,
                            preferred_element_type=F32)
                outs.append(jnp.where(low, o[0:st, 0:LANES], o[st:2 * st, 0:LANES]))
                l_rep = o[:, LANES:2 * LANES]
                for half in range(2):
                    rs = slice(half * st, (half + 1) * st)
                    ms = slice((2 * pair + half) * st, (2 * pair + half + 1) * st)
                    stats = jnp.where(lane == 2 * p + half, m_rep[ms], stats)
                    stats = jnp.where(lane == DIL_HEADS + 2 * p + half, l_rep[rs], stats)
        o_ref[0, 0, r0:r0 + st, :] = jnp.concatenate(outs, axis=1).astype(o_ref.dtype)
        lse_ref[0, 0, r0:r0 + st, :] = stats


def _dil_attn(qkv, batch, seq, dil):
    sub = seq // dil
    tq = min(ATTN_TQ, sub)
    per = tq // DIL_STEPS

    def cur(which):
        return pl.BlockSpec((1, 1, 1, tq, DIL_GROUP_W), lambda b, r, n: (which, b, r, n, 0))

    def prev(which):
        return pl.BlockSpec((1, 1, 1, DIL_STEPS, DIL_GROUP_W),
                            lambda b, r, n: (which, b, r, jnp.maximum(n * per - 1, 0), 0))

    return pl.pallas_call(
        functools.partial(_dil_attn_kernel, tq=tq),
        grid=(batch, dil, sub // tq),
        in_specs=[cur(0), cur(1), prev(1), cur(2), prev(2)],
        out_specs=[pl.BlockSpec((1, 1, tq, DIL_GROUP_W), lambda b, r, n: (b, r, n, 0)),
                   pl.BlockSpec((1, 1, tq, LANES), lambda b, r, n: (b, r, n, 0))],
        out_shape=[jax.ShapeDtypeStruct((batch, dil, sub, DIL_GROUP_W), BF16),
                   jax.ShapeDtypeStruct((batch, dil, sub, LANES), F32)],
        compiler_params=_cp("parallel", "parallel", "arbitrary"),
        name=f"dil_attn{dil}",
    )(qkv, qkv, qkv, qkv, qkv)


def _store_token_tiles(ref, val):
    rows = val.shape[0]
    for c in range(TOKEN_ROWS):
        ref[pl.ds(c, rows, stride=TOKEN_ROWS), :] = val[:, c * LANES:(c + 1) * LANES]


def _load_token_tiles(ref, rows):
    return jnp.concatenate([ref[pl.ds(c, rows, stride=TOKEN_ROWS), :] for c in range(TOKEN_ROWS)], axis=1)


def _layer_norm(x, g, b):
    mu = jnp.mean(x, axis=-1, keepdims=True)
    var = jnp.mean(jnp.square(x - mu), axis=-1, keepdims=True)
    return (x - mu) * lax.rsqrt(var + LN_EPS) * g + b


def _mix_kernel(og_ref, g1_ref, g2_ref, o0_ref, o1_ref, o2_ref, l0_ref, l1_ref, l2_ref, x_ref,
                wgo_ref, wao_ref, wout_ref, lng_ref, lnb_ref, wrt_ref, br_ref, expand_ref,
                x1_ref, bucket_ref, osc_ref, lsc_ref):
    tm = x_ref.shape[0]

    def interleaved(o_ref, l_ref, dil, slot):
        if dil == 1:
            return o_ref[0, 0].astype(F32), l_ref[0, 0]
        rows = tm // dil
        chunks = DIL_GROUP_W // LANES
        for r in range(dil):
            for c in range(chunks):
                osc_ref[slot, c, pl.ds(r, rows, stride=dil), :] = (
                    o_ref[0, r, :, c * LANES:(c + 1) * LANES].astype(F32))
            lsc_ref[slot, pl.ds(r, rows, stride=dil), :] = l_ref[0, r]
        return jnp.concatenate([osc_ref[slot, c] for c in range(chunks)], axis=1), lsc_ref[slot]

    groups = [interleaved(o_ref, l_ref, dil, slot)
              for slot, (o_ref, l_ref, (_, dil)) in enumerate(
                  zip((o0_ref, o1_ref, o2_ref), (l0_ref, l1_ref, l2_ref), DIL_PATTERN))]
    stats = [g[1] for g in groups]
    m_max = jnp.maximum(jnp.maximum(stats[0], stats[1]), stats[2])
    es = [jnp.exp(s - m_max) for s in stats]
    ls = [pltpu.roll(s, LANES - DIL_HEADS, 1) for s in stats]
    inv = 1.0 / (es[0] * ls[0] + es[1] * ls[1] + es[2] * ls[2])
    head_lane = lax.broadcasted_iota(I32, (tm, LANES), 1) < DIL_HEADS
    att = jnp.zeros((tm, DIL_GROUP_W), F32)
    for (num_g, _), e in zip(groups, es):
        coef = jnp.where(head_lane, e * inv, 0.0)
        att = att + _split_dot_rhs(coef, expand_ref[...]) * num_g

    y_att = jnp.dot(att.astype(BF16), wao_ref[...], preferred_element_type=F32)
    y_gla = jnp.dot(og_ref[...], wgo_ref[...], preferred_element_type=F32)
    mixed = (jax.nn.sigmoid(g1_ref[...].astype(F32)) * y_gla
             + jax.nn.sigmoid(g2_ref[...].astype(F32)) * y_att)
    mix = jnp.dot(mixed.astype(BF16), wout_ref[...], preferred_element_type=F32)
    x1 = _layer_norm(DN_ALPHA * x_ref[...] + mix, lng_ref[...], lnb_ref[...])
    _store_token_tiles(x1_ref, x1)

    logits = lax.dot_general(wrt_ref[...], x1.astype(BF16), (((1,), (1,)), ((), ())),
                             preferred_element_type=F32) + br_ref[:, 0:1]
    e = jnp.exp(logits - jnp.max(logits, axis=0, keepdims=True))
    probs = e / jnp.sum(e, axis=0, keepdims=True)
    best = None
    for g in range(N_GROUPS):
        for pi, (ea, eb_) in enumerate(PAIRS):
            cand = probs[4 * g + ea:4 * g + ea + 1] + probs[4 * g + eb_:4 * g + eb_ + 1]
            idx = g * len(PAIRS) + pi
            if best is None:
                best, bidx = cand, jnp.zeros(cand.shape, I32)
            else:
                better = cand > best
                best = jnp.where(better, cand, best)
                bidx = jnp.where(better, idx, bidx)
    bucket_ref[0] = bidx


def _split_dot_rhs(val_f32, mat_bf16):
    hi = val_f32.astype(BF16)
    lo = (val_f32 - hi.astype(F32)).astype(BF16)
    return (jnp.dot(hi, mat_bf16, preferred_element_type=F32)
            + jnp.dot(lo, mat_bf16, preferred_element_type=F32))


def _mix(og, h_main, attn, x, wgo, wao, wout, lng, lnb, wrt, br, expand, batch, seq):
    n = batch * seq
    tm = min(MIX_TM, seq)
    tps = seq // tm
    row = lambda i: (i, 0)
    const = lambda i: (0, 0)
    o_specs, l_specs, o_args, l_args = [], [], [], []
    for (o_g, lse_g), (_, dil) in zip(attn, DIL_PATTERN):
        o_specs.append(pl.BlockSpec((1, dil, tm // dil, DIL_GROUP_W), lambda i: (i // tps, 0, i % tps, 0)))
        l_specs.append(pl.BlockSpec((1, dil, tm // dil, LANES), lambda i: (i // tps, 0, i % tps, 0)))
        o_args.append(o_g)
        l_args.append(lse_g)
    gate_blk = GATE_COL // D_MODEL
    return pl.pallas_call(
        _mix_kernel,
        grid=(n // tm,),
        in_specs=[pl.BlockSpec((tm, GLA_V_W), row),
                  pl.BlockSpec((tm, D_MODEL), lambda i: (i, gate_blk)),
                  pl.BlockSpec((tm, D_MODEL), lambda i: (i, gate_blk + 1)),
                  *o_specs, *l_specs,
                  pl.BlockSpec((tm, D_MODEL), row),
                  pl.BlockSpec((GLA_V_W, D_MODEL), const),
                  pl.BlockSpec((DIL_GROUP_W, D_MODEL), const),
                  pl.BlockSpec((D_MODEL, D_MODEL), const),
                  pl.BlockSpec((1, D_MODEL), const),
                  pl.BlockSpec((1, D_MODEL), const),
                  pl.BlockSpec((N_EXPERTS, D_MODEL), const),
                  pl.BlockSpec((N_EXPERTS, LANES), const),
                  pl.BlockSpec((LANES, DIL_GROUP_W), const)],
        out_specs=[pl.BlockSpec((tm * TOKEN_ROWS, LANES), row),
                   pl.BlockSpec((1, 1, tm), lambda i: (i, 0, 0))],
        out_shape=[jax.ShapeDtypeStruct((n * TOKEN_ROWS, LANES), F32),
                   jax.ShapeDtypeStruct((n // tm, 1, tm), I32)],
        scratch_shapes=[pltpu.VMEM((N_DIL, DIL_GROUP_W // LANES, tm, LANES), F32),
                        pltpu.VMEM((N_DIL, tm, LANES), F32)],
        compiler_params=_cp("parallel"),
        name="mix",
    )(og, h_main, h_main, *o_args, *l_args, x, wgo, wao, wout, lng, lnb, wrt, br, expand)


def _rank_kernel(bucket_ref, upper_ref, rank_ref, counts_ref, carry_ref):
    @pl.when(pl.program_id(0) == 0)
    def _():
        carry_ref[...] = jnp.zeros_like(carry_ref)

    b = bucket_ref[0]
    ts = b.shape[1]
    onehot = lax.broadcasted_iota(I32, (BUCKET_ROWS, ts), 0) == b
    oh_f = jnp.where(onehot, 1.0, 0.0)
    before = jnp.dot(oh_f.astype(BF16), upper_ref[...], preferred_element_type=F32)
    carry = carry_ref[...]
    rank = jnp.sum(oh_f * (before + carry[:, 0:1]), axis=0, keepdims=True)
    rank_ref[0] = rank.astype(I32)
    carry = carry + jnp.sum(oh_f, axis=1, keepdims=True)
    carry_ref[...] = carry
    counts_ref[...] = carry.astype(I32)


def _bucket_ranks(bucket, n):
    ts = min(SORT_TS, n)
    b3 = bucket.reshape(n // ts, 1, ts)
    j = jnp.arange(ts)
    upper = (j[:, None] < j[None, :]).astype(BF16)
    rank, counts = pl.pallas_call(
        _rank_kernel,
        grid=(n // ts,),
        in_specs=[pl.BlockSpec((1, 1, ts), lambda i: (i, 0, 0)),
                  pl.BlockSpec((ts, ts), lambda i: (0, 0))],
        out_specs=[pl.BlockSpec((1, 1, ts), lambda i: (i, 0, 0)),
                   pl.BlockSpec((BUCKET_ROWS, LANES), lambda i: (0, 0))],
        out_shape=[jax.ShapeDtypeStruct((n // ts, 1, ts), I32),
                   jax.ShapeDtypeStruct((BUCKET_ROWS, LANES), I32)],
        scratch_shapes=[pltpu.VMEM((BUCKET_ROWS, LANES), F32)],
        compiler_params=_cp("arbitrary"),
        name="bucket_rank",
    )(b3, upper)
    return rank.reshape(n), counts[:N_BUCKETS, 0]


def _sort_plan(bucket, rank, counts, n):
    tiles = (counts + MOE_T - 1) // MOE_T
    tile_end = jnp.cumsum(tiles)
    offsets = (tile_end - tiles) * MOE_T
    dest = offsets[bucket] + rank
    n_tiles = n // MOE_T + N_BUCKETS
    tile_ids = jnp.arange(n_tiles, dtype=I32)
    total = tile_end[-1]
    last = jnp.minimum(tile_ids, total - 1)
    tb = jnp.sum((last[:, None] >= tile_end[None, :]).astype(I32), axis=1)
    tb = jnp.minimum(tb, N_BUCKETS - 1)
    n_valid = jnp.clip(counts[tb] - (tile_ids - (tile_end - tiles)[tb]) * MOE_T, 0, MOE_T)
    n_valid = jnp.where(tile_ids < total, n_valid, 0)
    pair = jnp.array(PAIRS, dtype=I32)
    e_lo = (tb // len(PAIRS)) * EXPERTS_PER_GROUP + pair[tb % len(PAIRS), 0]
    e_hi = (tb // len(PAIRS)) * EXPERTS_PER_GROUP + pair[tb % len(PAIRS), 1]
    pad_lo = jnp.concatenate([offsets + counts, total[None] * MOE_T])
    pad_hi = jnp.concatenate([tile_end * MOE_T, jnp.full((1,), n_tiles * MOE_T, I32)])
    pad_ranges = jnp.stack([pad_lo, pad_hi]).astype(I32)
    return dest.astype(I32), tb, e_lo, e_hi, n_valid.astype(I32), pad_ranges


def _invert_kernel(pad_ref, dest_ref, src_ref, table_ref, sem, *, chunk):
    i = pl.program_id(0)

    @pl.when(i == 0)
    def _():
        def clear(j, carry):
            table_ref[j] = 0
            return carry
        for b in range(N_BUCKETS + 1):
            lax.fori_loop(pad_ref[0, b], pad_ref[1, b], clear, 0)

    def put(j, carry):
        table_ref[dest_ref[0, 0, j]] = i * chunk + j
        return carry
    lax.fori_loop(0, chunk, put, 0, unroll=8)

    @pl.when(i == pl.num_programs(0) - 1)
    def _():
        copy = pltpu.make_async_copy(table_ref, src_ref, sem)
        copy.start()
        copy.wait()


def _invert(dest, pad_ranges, n_slots):
    n = dest.shape[0]
    chunk = min(2048, n)
    return pl.pallas_call(
        functools.partial(_invert_kernel, chunk=chunk),
        grid=(n // chunk,),
        in_specs=[pl.BlockSpec(memory_space=pltpu.SMEM),
                  pl.BlockSpec((1, 1, chunk), lambda i: (i, 0, 0), memory_space=pltpu.SMEM)],
        out_specs=pl.BlockSpec(memory_space=pl.ANY),
        out_shape=jax.ShapeDtypeStruct((n_slots,), I32),
        scratch_shapes=[pltpu.SMEM((n_slots,), I32), pltpu.SemaphoreType.DMA(())],
        compiler_params=_cp("arbitrary"),
        name="invert_perm",
    )(pad_ranges, dest.reshape(n // chunk, 1, chunk))


def _moe_kernel(tb_ref, elo_ref, ehi_ref, nvalid_ref, prv_ref, src_ref, nxt_ref, x_hbm, wr2_ref, br2_ref,
                wg_lo_ref, wu_lo_ref, wd_lo_ref, wg_hi_ref, wu_hi_ref, wd_hi_ref, lng_ref, lnb_ref,
                out_hbm, xbuf_ref, ybuf_ref, gsem, ssem, *, n_tokens):
    del tb_ref, elo_ref, ehi_ref
    t = pl.program_id(0)
    last = pl.num_programs(0) - 1
    slot = t % 2
    other = 1 - slot
    tile = MOE_T
    n_chunks = 8
    per_chunk = tile // n_chunks

    def token_rows(tok):
        if isinstance(tok, int):
            return pl.ds(tok * TOKEN_ROWS, TOKEN_ROWS)
        return pl.ds(pl.multiple_of(tok * TOKEN_ROWS, TOKEN_ROWS), TOKEN_ROWS)

    def gather_copy(idx_ref, r, sl):
        return pltpu.make_async_copy(x_hbm.at[token_rows(idx_ref[0, 0, r])],
                                     xbuf_ref.at[sl, token_rows(r)], gsem.at[sl])

    def scatter_copy(dst, r, sl):
        return pltpu.make_async_copy(ybuf_ref.at[sl, token_rows(r)], out_hbm.at[token_rows(dst)], ssem.at[sl])

    def scatter_rows(idx_ref, n_valid, rows, sl):
        for r in rows:
            dst = jnp.where(r < n_valid, idx_ref[0, 0, r], n_tokens + sl * tile + r)
            scatter_copy(dst, r, sl).start(priority=r % 2)

    prev_valid = jnp.where(t > 0, nvalid_ref[jnp.maximum(t - 1, 0)], 0)

    def issue(chunk):
        rows = range(chunk * per_chunk, (chunk + 1) * per_chunk)
        for r in rows:
            gather_copy(nxt_ref, r, other).start(priority=r % 2)
        scatter_rows(prv_ref, prev_valid, rows, other)

    @pl.when(t == 0)
    def _():
        ybuf_ref[...] = jnp.zeros_like(ybuf_ref)
        for r in range(tile):
            gather_copy(src_ref, r, 0).start()
        scatter_rows(src_ref, 0, range(tile), 0)

    for r in range(tile):
        gather_copy(src_ref, r, slot).wait()
    for r in range(tile):
        scatter_copy(0, r, slot).wait()

    @pl.when(nvalid_ref[t] == 0)
    def _():
        for chunk in range(n_chunks):
            issue(chunk)

    @pl.when(nvalid_ref[t] != 0)
    def _():
        x = _load_token_tiles(xbuf_ref.at[slot], tile)
        xb = x.astype(BF16)
        issue(0)
        l2 = jnp.dot(xb, wr2_ref[0], preferred_element_type=F32) + br2_ref[0]
        g_lo = jax.nn.sigmoid(l2[:, 0:1] - l2[:, 1:2])
        g_hi = jax.nn.sigmoid(l2[:, 1:2] - l2[:, 0:1])

        def expert(wg_ref, wu_ref, wd_ref, first_chunk):
            a = jnp.dot(xb, wg_ref[0], preferred_element_type=F32)
            issue(first_chunk)
            u = jnp.dot(xb, wu_ref[0], preferred_element_type=F32)
            issue(first_chunk + 1)
            hidden = (a * jax.nn.sigmoid(a) * u).astype(BF16)
            out = jnp.dot(hidden, wd_ref[0], preferred_element_type=F32)
            issue(first_chunk + 2)
            return out

        y = g_lo * expert(wg_lo_ref, wu_lo_ref, wd_lo_ref, 1)
        y = y + g_hi * expert(wg_hi_ref, wu_hi_ref, wd_hi_ref, 4)
        _store_token_tiles(ybuf_ref.at[slot], _layer_norm(DN_ALPHA * x + y, lng_ref[...], lnb_ref[...]))
        issue(7)

    @pl.when(t == last)
    def _():
        for r in range(tile):
            gather_copy(nxt_ref, r, other).wait()
        scatter_rows(src_ref, nvalid_ref[t], range(tile), slot)
        for sl in range(2):
            for r in range(tile):
                scatter_copy(0, r, sl).wait()


def _moe(x1, src, plan, wr2, br2, wg, wu, wd, lng, lnb):
    _, tb, e_lo, e_hi, n_valid, _ = plan
    n = x1.shape[0] // TOKEN_ROWS
    n_tiles = src.shape[0] // MOE_T
    src3 = src.reshape(n_tiles, 1, MOE_T)
    lo = lambda t, tb, elo, ehi, v: (elo[t], 0, 0)
    hi = lambda t, tb, elo, ehi, v: (ehi[t], 0, 0)
    const = lambda t, tb, elo, ehi, v: (0, 0)
    wblk = (1, D_MODEL, D_FF)
    idx_blk = (1, 1, MOE_T)
    grid_spec = pltpu.PrefetchScalarGridSpec(
        num_scalar_prefetch=4,
        grid=(n_tiles,),
        in_specs=[pl.BlockSpec(idx_blk, lambda t, tb, elo, ehi, v: (jnp.maximum(t - 1, 0), 0, 0),
                               memory_space=pltpu.SMEM),
                  pl.BlockSpec(idx_blk, lambda t, tb, elo, ehi, v: (t, 0, 0), memory_space=pltpu.SMEM),
                  pl.BlockSpec(idx_blk, lambda t, tb, elo, ehi, v: (jnp.minimum(t + 1, n_tiles - 1), 0, 0),
                               memory_space=pltpu.SMEM),
                  pl.BlockSpec(memory_space=pl.ANY),
                  pl.BlockSpec((1, D_MODEL, LANES), lambda t, tb, elo, ehi, v: (tb[t], 0, 0)),
                  pl.BlockSpec((1, 1, LANES), lambda t, tb, elo, ehi, v: (tb[t], 0, 0)),
                  pl.BlockSpec(wblk, lo), pl.BlockSpec(wblk, lo), pl.BlockSpec((1, D_FF, D_MODEL), lo),
                  pl.BlockSpec(wblk, hi), pl.BlockSpec(wblk, hi), pl.BlockSpec((1, D_FF, D_MODEL), hi),
                  pl.BlockSpec((1, D_MODEL), const), pl.BlockSpec((1, D_MODEL), const)],
        out_specs=pl.BlockSpec(memory_space=pl.ANY),
        scratch_shapes=[pltpu.VMEM((2, MOE_T * TOKEN_ROWS, LANES), F32),
                        pltpu.VMEM((2, MOE_T * TOKEN_ROWS, LANES), F32),
                        pltpu.SemaphoreType.DMA((2,)), pltpu.SemaphoreType.DMA((2,))],
    )
    return pl.pallas_call(
        functools.partial(_moe_kernel, n_tokens=n),
        grid_spec=grid_spec,
        out_shape=jax.ShapeDtypeStruct(((n + 2 * MOE_T) * TOKEN_ROWS, LANES), F32),
        compiler_params=_cp("arbitrary"),
        name="moe",
    )(tb, e_lo, e_hi, n_valid, src3, src3, src3, x1, wr2, br2, wg, wu, wd, wg, wu, wd, lng, lnb)


def _untile_kernel(xt_ref, x_ref, xb_ref):
    x = _load_token_tiles(xt_ref, x_ref.shape[0])
    x_ref[...] = x
    xb_ref[...] = x.astype(BF16)


def _untile(xt, n):
    tm = min(PROJ_TM, n)
    return pl.pallas_call(
        _untile_kernel,
        grid=(n // tm,),
        in_specs=[pl.BlockSpec((tm * TOKEN_ROWS, LANES), lambda i: (i, 0))],
        out_specs=[pl.BlockSpec((tm, D_MODEL), lambda i: (i, 0))] * 2,
        out_shape=[jax.ShapeDtypeStruct((n, D_MODEL), F32), jax.ShapeDtypeStruct((n, D_MODEL), BF16)],
        compiler_params=_cp("parallel"),
        name="untile",
    )(xt)


def _split_w_in(w_in_l):
    sizes = (GLA_QK_W, GLA_QK_W, GLA_V_W, GLA_V_W, GLA_RANK,
             N_DIL * DIL_GROUP_W, N_DIL * DIL_GROUP_W, N_DIL * DIL_GROUP_W, 2 * D_MODEL)
    starts = [sum(sizes[:i]) for i in range(len(sizes))]
    blk = [w_in_l[:, s:s + z] for s, z in zip(starts, sizes)]
    q_g, k_g, v_g, r_g, a_g, q_d, k_d, v_d, gates = blk
    w_main = jnp.concatenate([q_g, k_g, v_g, r_g, gates], axis=1).astype(BF16)
    w_a = jnp.pad(a_g, ((0, 0), (0, LANES - GLA_RANK))).astype(BF16)
    quad = jnp.asarray(_quad_layout())
    w_dil = []
    for g in range(N_DIL):
        gs = slice(g * DIL_GROUP_W, (g + 1) * DIL_GROUP_W)
        w_dil.append(jnp.concatenate([q_d[:, gs][:, quad], k_d[:, gs][:, quad], v_d[:, gs]],
                                     axis=1).astype(BF16))
    return w_main, w_a, w_dil


def _quad_layout():
    half = DIL_HD // 2
    cols = []
    for quad in range(DIL_HEADS // 4):
        for part in range(2):
            for u in range(4):
                start = (4 * quad + u) * DIL_HD + part * half
                cols.extend(range(start, start + half))
    return cols


def _router_pair_tables(w_router, b_router):
    cols_w, cols_b = [], []
    for g in range(N_GROUPS):
        for ea, eb_ in PAIRS:
            lo, hi = EXPERTS_PER_GROUP * g + ea, EXPERTS_PER_GROUP * g + eb_
            w2 = jnp.stack([w_router[:, lo], w_router[:, hi]], axis=1)
            cols_w.append(jnp.pad(w2, ((0, 0), (0, LANES - 2))))
            cols_b.append(jnp.pad(jnp.stack([b_router[lo], b_router[hi]]), (0, LANES - 2)))
    return jnp.stack(cols_w).astype(BF16), jnp.stack(cols_b).reshape(N_BUCKETS, 1, LANES).astype(F32)


def kernel(x, positions, w_in, w_alpha_up, b_alpha, gla_norm, w_gla_o, w_att_o, w_out, ln1_g, ln1_b,
           w_router, b_router, w_e_gate, w_e_up, w_e_down, ln2_g, ln2_b):
    batch, seq, d_model = x.shape
    depth = w_in.shape[0]
    n = batch * seq
    assert d_model == D_MODEL and seq % (DIL_PATTERN[-1][1] * DIL_STEPS) == 0

    cos_t, sin_t = _rope_tables(positions)
    tri = _gla_masks()
    head_of_col = jnp.arange(DIL_GROUP_W) // DIL_HD
    expand = (jnp.arange(LANES)[:, None] == head_of_col[None, :]).astype(BF16)
    wrt = w_router.T.astype(BF16)
    br = jnp.broadcast_to(b_router.astype(F32)[:, None], (N_EXPERTS, LANES))
    wr2, br2 = _router_pair_tables(w_router, b_router)
    n_slots = n + N_BUCKETS * MOE_T

    xf = xin = x.reshape(n, D_MODEL)
    for i in range(depth):
        w_main, w_a, w_dil = _split_w_in(w_in[i])
        h_main = _proj(xin, w_main, BF16, PROJ_TN, n)
        a_proj = _proj(xin, w_a, F32, LANES, n)
        wup = jnp.pad(w_alpha_up[i], ((0, LANES - GLA_RANK), (0, 0))).astype(BF16)
        og = _gla(h_main, a_proj, wup, b_alpha[i].reshape(1, -1).astype(F32),
                  gla_norm[i].reshape(1, -1).astype(F32), tri, batch, seq)
        attn = []
        for g, (_, dil) in enumerate(DIL_PATTERN):
            qkv = _proj_dil(xin, w_dil[g], cos_t, sin_t, batch, seq, dil)
            attn.append(_dil_attn(qkv, batch, seq, dil))
        x1, bucket = _mix(og.reshape(n, GLA_V_W), h_main, attn, xf,
                          w_gla_o[i].astype(BF16), w_att_o[i].astype(BF16), w_out[i].astype(BF16),
                          ln1_g[i].reshape(1, -1), ln1_b[i].reshape(1, -1), wrt, br, expand, batch, seq)
        bucket = bucket.reshape(n)
        rank, counts = _bucket_ranks(bucket, n)
        plan = _sort_plan(bucket, rank, counts, n)
        src = _invert(plan[0], plan[5], n_slots)
        xt = _moe(x1, src, plan, wr2, br2, w_e_gate[i].astype(BF16), w_e_up[i].astype(BF16),
                  w_e_down[i].astype(BF16), ln2_g[i].reshape(1, -1), ln2_b[i].reshape(1, -1))
        xf, xin = _untile(xt, n)
    return xf.reshape(batch, seq, D_MODEL)
```

```python
import functools

import jax
import jax.numpy as jnp
from jax import lax
from jax.experimental import pallas as pl
from jax.experimental.pallas import tpu as pltpu

F32 = jnp.float32
BF16 = jnp.bfloat16
I32 = jnp.int32

D_MODEL = 1024
N_LAYERS = 4
GLA_HEADS = 4
GLA_DK = 128
GLA_DV = 256
GLA_RANK = 16
GLA_TAU = 16.0
GLA_CHUNK = 64
DIL_PATTERN = ((128, 1), (512, 4), (2048, 16))
DIL_HEADS = 8
DIL_HD = 64
DIL_STEPS = 128
ROPE_THETA = 10000.0
N_EXPERTS = 16
N_GROUPS = 4
EXPERTS_PER_GROUP = 4
D_FF = 1024
LN_EPS = 1e-5
RMS_EPS = 1e-6
DN_ALPHA = (2 * N_LAYERS) ** 0.25
NEG_INF = -1e30

GLA_QK_W = GLA_HEADS * GLA_DK
GLA_V_W = GLA_HEADS * GLA_DV
DIL_GROUP_W = DIL_HEADS * DIL_HD
N_DIL = len(DIL_PATTERN)
MAIN_W = 2 * GLA_QK_W + 2 * GLA_V_W + 2 * D_MODEL
GATE_COL = 2 * GLA_QK_W + 2 * GLA_V_W

LANES = 128
TOKEN_ROWS = D_MODEL // LANES
PAIRS = ((0, 1), (0, 2), (0, 3), (1, 2), (1, 3), (2, 3))
N_BUCKETS = N_GROUPS * len(PAIRS)
BUCKET_ROWS = 32

PROJ_TM = 1024
PROJ_MAIN_TM = 2048
PROJ_TN = 1024
GLA_T = 512
GLA_BLOCK = 2 * GLA_CHUNK
MIX_TM = 512
SORT_TS = 1024
MOE_T = 128
ATTN_TQ = 512
VMEM_LIMIT = 56 * 1024 * 1024


def _cp(*sem):
    return pltpu.CompilerParams(dimension_semantics=sem, vmem_limit_bytes=VMEM_LIMIT)


def _rope_kernel(pos_ref, invf_ref, cos_ref, sin_ref):
    ang = pos_ref[...].astype(F32) * invf_ref[...]
    cos_ref[...] = jnp.cos(ang)
    sin_ref[...] = jnp.sin(ang)


def _rope_tables(positions):
    n = positions.size
    tm = 2048
    half = DIL_HD // 2
    inv_freq = jnp.power(jnp.float32(ROPE_THETA), -jnp.arange(0, DIL_HD, 2, dtype=F32) / DIL_HD)
    invf = jnp.tile(inv_freq, LANES // half).reshape(1, LANES)
    return pl.pallas_call(
        _rope_kernel,
        grid=(n // tm,),
        in_specs=[pl.BlockSpec((tm, 1), lambda i: (i, 0)),
                  pl.BlockSpec((1, LANES), lambda i: (0, 0))],
        out_specs=[pl.BlockSpec((tm, LANES), lambda i: (i, 0)),
                   pl.BlockSpec((tm, LANES), lambda i: (i, 0))],
        out_shape=[jax.ShapeDtypeStruct((n, LANES), F32)] * 2,
        compiler_params=_cp("parallel"),
        name="rope_tables",
    )(positions.reshape(n, 1), invf)


def _proj_kernel(x_ref, w_ref, o_ref, xb_ref):
    @pl.when(pl.program_id(1) == 0)
    def _():
        xb_ref[...] = x_ref[...].astype(BF16)

    o_ref[...] = jnp.dot(xb_ref[...], w_ref[...], preferred_element_type=F32).astype(o_ref.dtype)


def _proj(x, w, out_dtype, tn, n):
    k = x.shape[1]
    ncol = w.shape[1]
    tm = min(PROJ_MAIN_TM, n)
    return pl.pallas_call(
        _proj_kernel,
        grid=(n // tm, ncol // tn),
        in_specs=[pl.BlockSpec((tm, k), lambda i, j: (i, 0)),
                  pl.BlockSpec((k, tn), lambda i, j: (0, j))],
        out_specs=pl.BlockSpec((tm, tn), lambda i, j: (i, j)),
        out_shape=jax.ShapeDtypeStruct((n, ncol), out_dtype),
        scratch_shapes=[pltpu.VMEM((tm, k), BF16)],
        compiler_params=_cp("parallel", "arbitrary"),
        name="proj",
    )(x, w)


def _proj_dil_kernel(x_ref, w_ref, cos_ref, sin_ref, o_ref, *slab_refs, dil, tm):
    acc_ref = slab_refs[0] if dil > 1 else None
    acc = jnp.dot(x_ref[...].astype(BF16), w_ref[...], preferred_element_type=F32)
    slabs = DIL_GROUP_W // LANES

    def emit(which, slab, val):
        if dil == 1:
            o_ref[which, 0, 0, :, slab * LANES:(slab + 1) * LANES] = val.astype(BF16)
        else:
            acc_ref[which * slabs + slab] = val

    def col(which, slab):
        start = which * DIL_GROUP_W + slab * LANES
        return acc[:, start:start + LANES]

    for which in range(2):
        scale = DIL_HD ** -0.5 if which == 0 else 1.0
        cos_t = cos_ref[...] * scale
        sin_t = sin_ref[...] * scale
        for quad in range(slabs // 2):
            x1, x2 = col(which, 2 * quad), col(which, 2 * quad + 1)
            emit(which, 2 * quad, x1 * cos_t - x2 * sin_t)
            emit(which, 2 * quad + 1, x2 * cos_t + x1 * sin_t)
    for slab in range(slabs):
        emit(2, slab, col(2, slab))

    if dil > 1:
        rows = tm // dil
        step = 4 if dil > 4 else dil
        outer = dil // step
        part = tm // step
        for s in range(3 * slabs):
            which, slab = divmod(s, slabs)
            if outer > 1:
                tmp_ref = slab_refs[1]
                for lo in range(step):
                    tmp_ref[s, lo * part:(lo + 1) * part, :] = acc_ref[s, pl.ds(lo, part, stride=step), :]
            for r in range(dil):
                hi, lo = divmod(r, step)
                if outer > 1:
                    val = tmp_ref[s, pl.ds(lo * part + hi, rows, stride=outer), :]
                else:
                    val = acc_ref[s, pl.ds(r, rows, stride=dil), :]
                o_ref[which, 0, r, :, slab * LANES:(slab + 1) * LANES] = val.astype(BF16)


def _proj_dil(x, w, cos_t, sin_t, batch, seq, dil):
    n, k = batch * seq, x.shape[1]
    tm = min(PROJ_TM, seq)
    tiles_per_seq = seq // tm
    sub = seq // dil
    kern = functools.partial(_proj_dil_kernel, dil=dil, tm=tm)
    n_slabs = 3 * DIL_GROUP_W // LANES
    return pl.pallas_call(
        kern,
        grid=(n // tm,),
        in_specs=[pl.BlockSpec((tm, k), lambda i: (i, 0)),
                  pl.BlockSpec((k, 3 * DIL_GROUP_W), lambda i: (0, 0)),
                  pl.BlockSpec((tm, LANES), lambda i: (i, 0)),
                  pl.BlockSpec((tm, LANES), lambda i: (i, 0))],
        out_specs=pl.BlockSpec((3, 1, dil, tm // dil, DIL_GROUP_W),
                               lambda i: (0, i // tiles_per_seq, 0, i % tiles_per_seq, 0)),
        out_shape=jax.ShapeDtypeStruct((3, batch, dil, sub, DIL_GROUP_W), BF16),
        scratch_shapes=[pltpu.VMEM((n_slabs, tm, LANES), F32)] * ((dil > 1) + (dil > 4)),
        compiler_params=_cp("parallel"),
        name=f"proj_dil{dil}",
    )(x, w, cos_t, sin_t)


def _split_dot(mat_bf16, val_f32):
    hi = val_f32.astype(BF16)
    lo = (val_f32 - hi.astype(F32)).astype(BF16)
    return (jnp.dot(mat_bf16, hi, preferred_element_type=F32)
            + jnp.dot(mat_bf16, lo, preferred_element_type=F32))


def _gla_kernel(q_ref, k_ref, v_ref, r_ref, a_ref, wup_ref, balpha_ref, gnorm_ref, tri_ref,
                o_ref, state_ref, sprev_ref):
    t = pl.program_id(1)
    n_chunks = GLA_T // GLA_CHUNK

    @pl.when(t == 0)
    def _():
        state_ref[...] = jnp.zeros_like(state_ref)

    z = jnp.dot(a_ref[0].astype(BF16), wup_ref[...], preferred_element_type=F32) + balpha_ref[...]
    log_a = (jnp.minimum(z, 0.0) - jnp.log1p(jnp.exp(-jnp.abs(z)))) / GLA_TAU
    tri = tri_ref[...]
    b = jnp.concatenate([_split_dot(tri, log_a[blk * GLA_BLOCK:(blk + 1) * GLA_BLOCK])
                         for blk in range(GLA_T // GLA_BLOCK)], axis=0)
    b_end = jnp.concatenate(
        [jnp.broadcast_to(b[(n + 1) * GLA_CHUNK - 1:(n + 1) * GLA_CHUNK], (GLA_CHUNK, GLA_QK_W))
         for n in range(n_chunks)], axis=0)
    c = b_end - b
    eb = jnp.exp(b)
    kf = k_ref[0].astype(F32)
    qt = (q_ref[0].astype(F32) * eb * (GLA_DK ** -0.5)).astype(BF16)
    kt = (kf * jnp.exp(-b)).astype(BF16)
    ks = (kf * jnp.exp(c)).astype(BF16)
    causal = tri > 0

    for h in range(GLA_HEADS):
        ck = slice(h * GLA_DK, (h + 1) * GLA_DK)
        cv = slice(h * GLA_DV, (h + 1) * GLA_DV)
        qt_h, kt_h, ks_h = qt[:, ck], kt[:, ck], ks[:, ck]
        v_h = v_ref[0, :, cv]
        intra = []
        for blk in range(GLA_T // GLA_BLOCK):
            rb = slice(blk * GLA_BLOCK, (blk + 1) * GLA_BLOCK)
            att = lax.dot_general(qt_h[rb], kt_h[rb], (((1,), (1,)), ((), ())), preferred_element_type=F32)
            att = jnp.where(causal, att, 0.0).astype(BF16)
            intra.append(jnp.dot(att, v_h[rb], preferred_element_type=F32))
        o_h = jnp.concatenate(intra, axis=0)

        state = state_ref[h]
        for n in range(n_chunks):
            rs = slice(n * GLA_CHUNK, (n + 1) * GLA_CHUNK)
            contrib = lax.dot_general(v_h[rs], ks_h[rs], (((0,), (0,)), ((), ())),
                                      preferred_element_type=F32)
            sprev_ref[h, n] = state.astype(BF16)
            decay = eb[(n + 1) * GLA_CHUNK - 1:(n + 1) * GLA_CHUNK, ck]
            state = state * decay + contrib
        state_ref[h] = state

        inter = [lax.dot_general(qt_h[n * GLA_CHUNK:(n + 1) * GLA_CHUNK], sprev_ref[h, n],
                                 (((1,), (1,)), ((), ())), preferred_element_type=F32)
                 for n in range(n_chunks)]
        o_h = o_h + jnp.concatenate(inter, axis=0)

        o_h = o_h * lax.rsqrt(jnp.mean(jnp.square(o_h), axis=-1, keepdims=True) + RMS_EPS)
        o_h = o_h * gnorm_ref[:, cv]
        r_h = r_ref[0, :, cv].astype(F32)
        o_ref[0, :, cv] = (o_h * (r_h * jax.nn.sigmoid(r_h))).astype(o_ref.dtype)


def _gla(h_main, a_proj, wup, balpha, gnorm, tri, batch, seq):
    h3 = h_main.reshape(batch, seq, MAIN_W)
    a3 = a_proj.reshape(batch, seq, LANES)
    n_chunks = GLA_T // GLA_CHUNK
    return pl.pallas_call(
        _gla_kernel,
        grid=(batch, seq // GLA_T),
        in_specs=[pl.BlockSpec((1, GLA_T, GLA_QK_W), lambda b, t: (b, t, 0)),
                  pl.BlockSpec((1, GLA_T, GLA_QK_W), lambda b, t: (b, t, 1)),
                  pl.BlockSpec((1, GLA_T, GLA_V_W), lambda b, t: (b, t, 1)),
                  pl.BlockSpec((1, GLA_T, GLA_V_W), lambda b, t: (b, t, 2)),
                  pl.BlockSpec((1, GLA_T, LANES), lambda b, t: (b, t, 0)),
                  pl.BlockSpec((LANES, GLA_QK_W), lambda b, t: (0, 0)),
                  pl.BlockSpec((1, GLA_QK_W), lambda b, t: (0, 0)),
                  pl.BlockSpec((1, GLA_V_W), lambda b, t: (0, 0)),
                  pl.BlockSpec((GLA_BLOCK, GLA_BLOCK), lambda b, t: (0, 0))],
        out_specs=pl.BlockSpec((1, GLA_T, GLA_V_W), lambda b, t: (b, t, 0)),
        out_shape=jax.ShapeDtypeStruct((batch, seq, GLA_V_W), BF16),
        scratch_shapes=[pltpu.VMEM((GLA_HEADS, GLA_DV, GLA_DK), F32),
                        pltpu.VMEM((GLA_HEADS, n_chunks, GLA_DV, GLA_DK), BF16)],
        compiler_params=_cp("parallel", "arbitrary"),
        name="gla",
    )(h3, h3, h3, h3, a3, wup, balpha, gnorm, tri)


def _gla_masks():
    i = jnp.arange(GLA_BLOCK)[:, None]
    j = jnp.arange(GLA_BLOCK)[None, :]
    same = (i // GLA_CHUNK) == (j // GLA_CHUNK)
    return (same & (j <= i)).astype(BF16)


def _dil_attn_kernel(q_ref, kc_ref, kp_ref, vc_ref, vp_ref, o_ref, lse_ref, *, tq):
    n = pl.program_id(2)
    st = DIL_STEPS
    qi = lax.broadcasted_iota(I32, (st, 2 * st), 0)
    kj = lax.broadcasted_iota(I32, (st, 2 * st), 1)
    in_prev = kj < st
    band_bias = jnp.where(jnp.where(in_prev, kj - qi, qi + st - kj) >= 0, 0.0, NEG_INF)
    prev_cols = jnp.where(in_prev, 1.0, 0.0)
    lane = lax.broadcasted_iota(I32, (st, LANES), 1)
    low = lane < DIL_HD
    ones = jnp.ones((2 * st, LANES), BF16)
    nt = (((1,), (1,)), ((), ()))

    no_prev = jnp.where(n == 0, NEG_INF, 0.0)
    first_bias = band_bias + prev_cols * no_prev

    def window(cur_ref, prev_ref, i, cs):
        if i == 0:
            return jnp.concatenate([prev_ref[0, 0, 0, :, cs], cur_ref[0, 0, 0, 0:st, cs]], axis=0)
        return cur_ref[0, 0, 0, (i - 1) * st:(i + 1) * st, cs]

    quad_w = 2 * LANES
    quad_lane = lax.broadcasted_iota(I32, (st, quad_w), 1)
    head_in_quad = (quad_lane % LANES) // (DIL_HD // 2)

    for i in range(tq // st):
        r0 = i * st
        bias = first_bias if i == 0 else band_bias
        bias4 = jnp.concatenate([bias] * 4, axis=0)
        stats = jnp.zeros((st, LANES), F32)
        outs = []
        for quad in range(DIL_HEADS // 4):
            qs = slice(quad * quad_w, (quad + 1) * quad_w)
            qq = q_ref[0, 0, 0, r0:r0 + st, qs]
            zero = jnp.zeros_like(qq)
            lhs = jnp.concatenate([jnp.where(head_in_quad == u, qq, zero) for u in range(4)], axis=0)
            s = lax.dot_general(lhs, window(kc_ref, kp_ref, i, qs), nt,
                                preferred_element_type=F32) + bias4
            m = jnp.max(s, axis=-1, keepdims=True)
            pexp = jnp.exp(s - m).astype(BF16)
            m_rep = jnp.broadcast_to(m, (4 * st, LANES))
            for pair in range(2):
                p = 2 * quad + pair
                cs = slice(p * LANES, (p + 1) * LANES)
                v_ext = jnp.concatenate([window(vc_ref, vp_ref, i, cs), ones], axis=1)
                o = jnp.dot(pexp[2 * pair * st:(2 * pair + 2) * st], v_ext,
                            preferred_element_type=F32)
                outs.append(jnp.where(low, o[0:st, 0:LANES], o[st:2 * st, 0:LANES]))
                l_rep = o[:, LANES:2 * LANES]
                for half in range(2):
                    rs = slice(half * st, (half + 1) * st)
                    ms = slice((2 * pair + half) * st, (2 * pair + half + 1) * st)
                    stats = jnp.where(lane == 2 * p + half, m_rep[ms], stats)
                    stats = jnp.where(lane == DIL_HEADS + 2 * p + half, l_rep[rs], stats)
        o_ref[0, 0, r0:r0 + st, :] = jnp.concatenate(outs, axis=1).astype(o_ref.dtype)
        lse_ref[0, 0, r0:r0 + st, :] = stats


def _dil_attn(qkv, batch, seq, dil):
    sub = seq // dil
    tq = min(ATTN_TQ, sub)
    per = tq // DIL_STEPS

    def cur(which):
        return pl.BlockSpec((1, 1, 1, tq, DIL_GROUP_W), lambda b, r, n: (which, b, r, n, 0))

    def prev(which):
        return pl.BlockSpec((1, 1, 1, DIL_STEPS, DIL_GROUP_W),
                            lambda b, r, n: (which, b, r, jnp.maximum(n * per - 1, 0), 0))

    return pl.pallas_call(
        functools.partial(_dil_attn_kernel, tq=tq),
        grid=(batch, dil, sub // tq),
        in_specs=[cur(0), cur(1), prev(1), cur(2), prev(2)],
        out_specs=[pl.BlockSpec((1, 1, tq, DIL_GROUP_W), lambda b, r, n: (b, r, n, 0)),
                   pl.BlockSpec((1, 1, tq, LANES), lambda b, r, n: (b, r, n, 0))],
        out_shape=[jax.ShapeDtypeStruct((batch, dil, sub, DIL_GROUP_W), BF16),
                   jax.ShapeDtypeStruct((batch, dil, sub, LANES), F32)],
        compiler_params=_cp("parallel", "parallel", "arbitrary"),
        name=f"dil_attn{dil}",
    )(qkv, qkv, qkv, qkv, qkv)


def _store_token_tiles(ref, val):
    rows = val.shape[0]
    for c in range(TOKEN_ROWS):
        ref[pl.ds(c, rows, stride=TOKEN_ROWS), :] = val[:, c * LANES:(c + 1) * LANES]


def _load_token_tiles(ref, rows):
    return jnp.concatenate([ref[pl.ds(c, rows, stride=TOKEN_ROWS), :] for c in range(TOKEN_ROWS)], axis=1)


def _layer_norm(x, g, b):
    mu = jnp.mean(x, axis=-1, keepdims=True)
    var = jnp.mean(jnp.square(x - mu), axis=-1, keepdims=True)
    return (x - mu) * lax.rsqrt(var + LN_EPS) * g + b


def _mix_kernel(og_ref, g1_ref, g2_ref, o0_ref, o1_ref, o2_ref, l0_ref, l1_ref, l2_ref, x_ref,
                wgo_ref, wao_ref, wout_ref, lng_ref, lnb_ref, wrt_ref, br_ref, expand_ref,
                x1_ref, bucket_ref, osc_ref, lsc_ref):
    tm = x_ref.shape[0]

    def interleaved(o_ref, l_ref, dil, slot):
        if dil == 1:
            return o_ref[0, 0].astype(F32), l_ref[0, 0]
        rows = tm // dil
        chunks = DIL_GROUP_W // LANES
        for r in range(dil):
            for c in range(chunks):
                osc_ref[slot, c, pl.ds(r, rows, stride=dil), :] = (
                    o_ref[0, r, :, c * LANES:(c + 1) * LANES].astype(F32))
            lsc_ref[slot, pl.ds(r, rows, stride=dil), :] = l_ref[0, r]
        return jnp.concatenate([osc_ref[slot, c] for c in range(chunks)], axis=1), lsc_ref[slot]

    groups = [interleaved(o_ref, l_ref, dil, slot)
              for slot, (o_ref, l_ref, (_, dil)) in enumerate(
                  zip((o0_ref, o1_ref, o2_ref), (l0_ref, l1_ref, l2_ref), DIL_PATTERN))]
    stats = [g[1] for g in groups]
    m_max = jnp.maximum(jnp.maximum(stats[0], stats[1]), stats[2])
    es = [jnp.exp(s - m_max) for s in stats]
    ls = [pltpu.roll(s, LANES - DIL_HEADS, 1) for s in stats]
    inv = 1.0 / (es[0] * ls[0] + es[1] * ls[1] + es[2] * ls[2])
    head_lane = lax.broadcasted_iota(I32, (tm, LANES), 1) < DIL_HEADS
    att = jnp.zeros((tm, DIL_GROUP_W), F32)
    for (num_g, _), e in zip(groups, es):
        coef = jnp.where(head_lane, e * inv, 0.0)
        att = att + _split_dot_rhs(coef, expand_ref[...]) * num_g

    y_att = jnp.dot(att.astype(BF16), wao_ref[...], preferred_element_type=F32)
    y_gla = jnp.dot(og_ref[...], wgo_ref[...], preferred_element_type=F32)
    mixed = (jax.nn.sigmoid(g1_ref[...].astype(F32)) * y_gla
             + jax.nn.sigmoid(g2_ref[...].astype(F32)) * y_att)
    mix = jnp.dot(mixed.astype(BF16), wout_ref[...], preferred_element_type=F32)
    x1 = _layer_norm(DN_ALPHA * x_ref[...] + mix, lng_ref[...], lnb_ref[...])
    _store_token_tiles(x1_ref, x1)

    logits = lax.dot_general(wrt_ref[...], x1.astype(BF16), (((1,), (1,)), ((), ())),
                             preferred_element_type=F32) + br_ref[:, 0:1]
    e = jnp.exp(logits - jnp.max(logits, axis=0, keepdims=True))
    probs = e / jnp.sum(e, axis=0, keepdims=True)
    best = None
    for g in range(N_GROUPS):
        for pi, (ea, eb_) in enumerate(PAIRS):
            cand = probs[4 * g + ea:4 * g + ea + 1] + probs[4 * g + eb_:4 * g + eb_ + 1]
            idx = g * len(PAIRS) + pi
            if best is None:
                best, bidx = cand, jnp.zeros(cand.shape, I32)
            else:
                better = cand > best
                best = jnp.where(better, cand, best)
                bidx = jnp.where(better, idx, bidx)
    bucket_ref[0] = bidx


def _split_dot_rhs(val_f32, mat_bf16):
    hi = val_f32.astype(BF16)
    lo = (val_f32 - hi.astype(F32)).astype(BF16)
    return (jnp.dot(hi, mat_bf16, preferred_element_type=F32)
            + jnp.dot(lo, mat_bf16, preferred_element_type=F32))


def _mix(og, h_main, attn, x, wgo, wao, wout, lng, lnb, wrt, br, expand, batch, seq):
    n = batch * seq
    tm = min(MIX_TM, seq)
    tps = seq // tm
    row = lambda i: (i, 0)
    const = lambda i: (0, 0)
    o_specs, l_specs, o_args, l_args = [], [], [], []
    for (o_g, lse_g), (_, dil) in zip(attn, DIL_PATTERN):
        o_specs.append(pl.BlockSpec((1, dil, tm // dil, DIL_GROUP_W), lambda i: (i // tps, 0, i % tps, 0)))
        l_specs.append(pl.BlockSpec((1, dil, tm // dil, LANES), lambda i: (i // tps, 0, i % tps, 0)))
        o_args.append(o_g)
        l_args.append(lse_g)
    gate_blk = GATE_COL // D_MODEL
    return pl.pallas_call(
        _mix_kernel,
        grid=(n // tm,),
        in_specs=[pl.BlockSpec((tm, GLA_V_W), row),
                  pl.BlockSpec((tm, D_MODEL), lambda i: (i, gate_blk)),
                  pl.BlockSpec((tm, D_MODEL), lambda i: (i, gate_blk + 1)),
                  *o_specs, *l_specs,
                  pl.BlockSpec((tm, D_MODEL), row),
                  pl.BlockSpec((GLA_V_W, D_MODEL), const),
                  pl.BlockSpec((DIL_GROUP_W, D_MODEL), const),
                  pl.BlockSpec((D_MODEL, D_MODEL), const),
                  pl.BlockSpec((1, D_MODEL), const),
                  pl.BlockSpec((1, D_MODEL), const),
                  pl.BlockSpec((N_EXPERTS, D_MODEL), const),
                  pl.BlockSpec((N_EXPERTS, LANES), const),
                  pl.BlockSpec((LANES, DIL_GROUP_W), const)],
        out_specs=[pl.BlockSpec((tm * TOKEN_ROWS, LANES), row),
                   pl.BlockSpec((1, 1, tm), lambda i: (i, 0, 0))],
        out_shape=[jax.ShapeDtypeStruct((n * TOKEN_ROWS, LANES), F32),
                   jax.ShapeDtypeStruct((n // tm, 1, tm), I32)],
        scratch_shapes=[pltpu.VMEM((N_DIL, DIL_GROUP_W // LANES, tm, LANES), F32),
                        pltpu.VMEM((N_DIL, tm, LANES), F32)],
        compiler_params=_cp("parallel"),
        name="mix",
    )(og, h_main, h_main, *o_args, *l_args, x, wgo, wao, wout, lng, lnb, wrt, br, expand)


def _rank_kernel(bucket_ref, upper_ref, rank_ref, counts_ref, carry_ref):
    @pl.when(pl.program_id(0) == 0)
    def _():
        carry_ref[...] = jnp.zeros_like(carry_ref)

    b = bucket_ref[0]
    ts = b.shape[1]
    onehot = lax.broadcasted_iota(I32, (BUCKET_ROWS, ts), 0) == b
    oh_f = jnp.where(onehot, 1.0, 0.0)
    before = jnp.dot(oh_f.astype(BF16), upper_ref[...], preferred_element_type=F32)
    carry = carry_ref[...]
    rank = jnp.sum(oh_f * (before + carry[:, 0:1]), axis=0, keepdims=True)
    rank_ref[0] = rank.astype(I32)
    carry = carry + jnp.sum(oh_f, axis=1, keepdims=True)
    carry_ref[...] = carry
    counts_ref[...] = carry.astype(I32)


def _bucket_ranks(bucket, n):
    ts = min(SORT_TS, n)
    b3 = bucket.reshape(n // ts, 1, ts)
    j = jnp.arange(ts)
    upper = (j[:, None] < j[None, :]).astype(BF16)
    rank, counts = pl.pallas_call(
        _rank_kernel,
        grid=(n // ts,),
        in_specs=[pl.BlockSpec((1, 1, ts), lambda i: (i, 0, 0)),
                  pl.BlockSpec((ts, ts), lambda i: (0, 0))],
        out_specs=[pl.BlockSpec((1, 1, ts), lambda i: (i, 0, 0)),
                   pl.BlockSpec((BUCKET_ROWS, LANES), lambda i: (0, 0))],
        out_shape=[jax.ShapeDtypeStruct((n // ts, 1, ts), I32),
                   jax.ShapeDtypeStruct((BUCKET_ROWS, LANES), I32)],
        scratch_shapes=[pltpu.VMEM((BUCKET_ROWS, LANES), F32)],
        compiler_params=_cp("arbitrary"),
        name="bucket_rank",
    )(b3, upper)
    return rank.reshape(n), counts[:N_BUCKETS, 0]


def _sort_plan(bucket, rank, counts, n):
    tiles = (counts + MOE_T - 1) // MOE_T
    tile_end = jnp.cumsum(tiles)
    offsets = (tile_end - tiles) * MOE_T
    dest = offsets[bucket] + rank
    n_tiles = n // MOE_T + N_BUCKETS
    tile_ids = jnp.arange(n_tiles, dtype=I32)
    total = tile_end[-1]
    last = jnp.minimum(tile_ids, total - 1)
    tb = jnp.sum((last[:, None] >= tile_end[None, :]).astype(I32), axis=1)
    tb = jnp.minimum(tb, N_BUCKETS - 1)
    n_valid = jnp.clip(counts[tb] - (tile_ids - (tile_end - tiles)[tb]) * MOE_T, 0, MOE_T)
    n_valid = jnp.where(tile_ids < total, n_valid, 0)
    pair = jnp.array(PAIRS, dtype=I32)
    e_lo = (tb // len(PAIRS)) * EXPERTS_PER_GROUP + pair[tb % len(PAIRS), 0]
    e_hi = (tb // len(PAIRS)) * EXPERTS_PER_GROUP + pair[tb % len(PAIRS), 1]
    pad_lo = jnp.concatenate([offsets + counts, total[None] * MOE_T])
    pad_hi = jnp.concatenate([tile_end * MOE_T, jnp.full((1,), n_tiles * MOE_T, I32)])
    pad_ranges = jnp.stack([pad_lo, pad_hi]).astype(I32)
    return dest.astype(I32), tb, e_lo, e_hi, n_valid.astype(I32), pad_ranges


def _invert_kernel(pad_ref, dest_ref, src_ref, table_ref, sem, *, chunk):
    i = pl.program_id(0)

    @pl.when(i == 0)
    def _():
        def clear(j, carry):
            table_ref[j] = 0
            return carry
        for b in range(N_BUCKETS + 1):
            lax.fori_loop(pad_ref[0, b], pad_ref[1, b], clear, 0)

    def put(j, carry):
        table_ref[dest_ref[0, 0, j]] = i * chunk + j
        return carry
    lax.fori_loop(0, chunk, put, 0, unroll=8)

    @pl.when(i == pl.num_programs(0) - 1)
    def _():
        copy = pltpu.make_async_copy(table_ref, src_ref, sem)
        copy.start()
        copy.wait()


def _invert(dest, pad_ranges, n_slots):
    n = dest.shape[0]
    chunk = min(2048, n)
    return pl.pallas_call(
        functools.partial(_invert_kernel, chunk=chunk),
        grid=(n // chunk,),
        in_specs=[pl.BlockSpec(memory_space=pltpu.SMEM),
                  pl.BlockSpec((1, 1, chunk), lambda i: (i, 0, 0), memory_space=pltpu.SMEM)],
        out_specs=pl.BlockSpec(memory_space=pl.ANY),
        out_shape=jax.ShapeDtypeStruct((n_slots,), I32),
        scratch_shapes=[pltpu.SMEM((n_slots,), I32), pltpu.SemaphoreType.DMA(())],
        compiler_params=_cp("arbitrary"),
        name="invert_perm",
    )(pad_ranges, dest.reshape(n // chunk, 1, chunk))


def _moe_kernel(tb_ref, elo_ref, ehi_ref, nvalid_ref, prv_ref, src_ref, nxt_ref, x_hbm, wr2_ref, br2_ref,
                wg_lo_ref, wu_lo_ref, wd_lo_ref, wg_hi_ref, wu_hi_ref, wd_hi_ref, lng_ref, lnb_ref,
                out_hbm, xbuf_ref, ybuf_ref, gsem, ssem, *, n_tokens):
    del tb_ref, elo_ref, ehi_ref
    t = pl.program_id(0)
    last = pl.num_programs(0) - 1
    slot = t % 2
    other = 1 - slot
    tile = MOE_T
    n_chunks = 8
    per_chunk = tile // n_chunks

    def token_rows(tok):
        if isinstance(tok, int):
            return pl.ds(tok * TOKEN_ROWS, TOKEN_ROWS)
        return pl.ds(pl.multiple_of(tok * TOKEN_ROWS, TOKEN_ROWS), TOKEN_ROWS)

    def gather_copy(idx_ref, r, sl):
        return pltpu.make_async_copy(x_hbm.at[token_rows(idx_ref[0, 0, r])],
                                     xbuf_ref.at[sl, token_rows(r)], gsem.at[sl])

    def scatter_copy(dst, r, sl):
        return pltpu.make_async_copy(ybuf_ref.at[sl, token_rows(r)], out_hbm.at[token_rows(dst)], ssem.at[sl])

    def scatter_rows(idx_ref, n_valid, rows, sl):
        for r in rows:
            dst = jnp.where(r < n_valid, idx_ref[0, 0, r], n_tokens + sl * tile + r)
            scatter_copy(dst, r, sl).start(priority=r % 2)

    prev_valid = jnp.where(t > 0, nvalid_ref[jnp.maximum(t - 1, 0)], 0)

    def issue(chunk):
        rows = range(chunk * per_chunk, (chunk + 1) * per_chunk)
        for r in rows:
            gather_copy(nxt_ref, r, other).start(priority=r % 2)
        scatter_rows(prv_ref, prev_valid, rows, other)

    @pl.when(t == 0)
    def _():
        ybuf_ref[...] = jnp.zeros_like(ybuf_ref)
        for r in range(tile):
            gather_copy(src_ref, r, 0).start()
        scatter_rows(src_ref, 0, range(tile), 0)

    for r in range(tile):
        gather_copy(src_ref, r, slot).wait()
    for r in range(tile):
        scatter_copy(0, r, slot).wait()

    @pl.when(nvalid_ref[t] == 0)
    def _():
        for chunk in range(n_chunks):
            issue(chunk)

    @pl.when(nvalid_ref[t] != 0)
    def _():
        x = _load_token_tiles(xbuf_ref.at[slot], tile)
        xb = x.astype(BF16)
        issue(0)
        l2 = jnp.dot(xb, wr2_ref[0], preferred_element_type=F32) + br2_ref[0]
        g_lo = jax.nn.sigmoid(l2[:, 0:1] - l2[:, 1:2])
        g_hi = jax.nn.sigmoid(l2[:, 1:2] - l2[:, 0:1])

        def expert(wg_ref, wu_ref, wd_ref, first_chunk):
            a = jnp.dot(xb, wg_ref[0], preferred_element_type=F32)
            issue(first_chunk)
            u = jnp.dot(xb, wu_ref[0], preferred_element_type=F32)
            issue(first_chunk + 1)
            hidden = (a * jax.nn.sigmoid(a) * u).astype(BF16)
            out = jnp.dot(hidden, wd_ref[0], preferred_element_type=F32)
            issue(first_chunk + 2)
            return out

        y = g_lo * expert(wg_lo_ref, wu_lo_ref, wd_lo_ref, 1)
        y = y + g_hi * expert(wg_hi_ref, wu_hi_ref, wd_hi_ref, 4)
        _store_token_tiles(ybuf_ref.at[slot], _layer_norm(DN_ALPHA * x + y, lng_ref[...], lnb_ref[...]))
        issue(7)

    @pl.when(t == last)
    def _():
        for r in range(tile):
            gather_copy(nxt_ref, r, other).wait()
        scatter_rows(src_ref, nvalid_ref[t], range(tile), slot)
        for sl in range(2):
            for r in range(tile):
                scatter_copy(0, r, sl).wait()


def _moe(x1, src, plan, wr2, br2, wg, wu, wd, lng, lnb):
    _, tb, e_lo, e_hi, n_valid, _ = plan
    n = x1.shape[0] // TOKEN_ROWS
    n_tiles = src.shape[0] // MOE_T
    src3 = src.reshape(n_tiles, 1, MOE_T)
    lo = lambda t, tb, elo, ehi, v: (elo[t], 0, 0)
    hi = lambda t, tb, elo, ehi, v: (ehi[t], 0, 0)
    const = lambda t, tb, elo, ehi, v: (0, 0)
    wblk = (1, D_MODEL, D_FF)
    idx_blk = (1, 1, MOE_T)
    grid_spec = pltpu.PrefetchScalarGridSpec(
        num_scalar_prefetch=4,
        grid=(n_tiles,),
        in_specs=[pl.BlockSpec(idx_blk, lambda t, tb, elo, ehi, v: (jnp.maximum(t - 1, 0), 0, 0),
                               memory_space=pltpu.SMEM),
                  pl.BlockSpec(idx_blk, lambda t, tb, elo, ehi, v: (t, 0, 0), memory_space=pltpu.SMEM),
                  pl.BlockSpec(idx_blk, lambda t, tb, elo, ehi, v: (jnp.minimum(t + 1, n_tiles - 1), 0, 0),
                               memory_space=pltpu.SMEM),
                  pl.BlockSpec(memory_space=pl.ANY),
                  pl.BlockSpec((1, D_MODEL, LANES), lambda t, tb, elo, ehi, v: (tb[t], 0, 0)),
                  pl.BlockSpec((1, 1, LANES), lambda t, tb, elo, ehi, v: (tb[t], 0, 0)),
                  pl.BlockSpec(wblk, lo), pl.BlockSpec(wblk, lo), pl.BlockSpec((1, D_FF, D_MODEL), lo),
                  pl.BlockSpec(wblk, hi), pl.BlockSpec(wblk, hi), pl.BlockSpec((1, D_FF, D_MODEL), hi),
                  pl.BlockSpec((1, D_MODEL), const), pl.BlockSpec((1, D_MODEL), const)],
        out_specs=pl.BlockSpec(memory_space=pl.ANY),
        scratch_shapes=[pltpu.VMEM((2, MOE_T * TOKEN_ROWS, LANES), F32),
                        pltpu.VMEM((2, MOE_T * TOKEN_ROWS, LANES), F32),
                        pltpu.SemaphoreType.DMA((2,)), pltpu.SemaphoreType.DMA((2,))],
    )
    return pl.pallas_call(
        functools.partial(_moe_kernel, n_tokens=n),
        grid_spec=grid_spec,
        out_shape=jax.ShapeDtypeStruct(((n + 2 * MOE_T) * TOKEN_ROWS, LANES), F32),
        compiler_params=_cp("arbitrary"),
        name="moe",
    )(tb, e_lo, e_hi, n_valid, src3, src3, src3, x1, wr2, br2, wg, wu, wd, wg, wu, wd, lng, lnb)


def _untile_kernel(xt_ref, x_ref, xb_ref):
    x = _load_token_tiles(xt_ref, x_ref.shape[0])
    x_ref[...] = x
    xb_ref[...] = x.astype(BF16)


def _untile(xt, n):
    tm = min(PROJ_TM, n)
    return pl.pallas_call(
        _untile_kernel,
        grid=(n // tm,),
        in_specs=[pl.BlockSpec((tm * TOKEN_ROWS, LANES), lambda i: (i, 0))],
        out_specs=[pl.BlockSpec((tm, D_MODEL), lambda i: (i, 0))] * 2,
        out_shape=[jax.ShapeDtypeStruct((n, D_MODEL), F32), jax.ShapeDtypeStruct((n, D_MODEL), BF16)],
        compiler_params=_cp("parallel"),
        name="untile",
    )(xt)


def _split_w_in(w_in_l):
    sizes = (GLA_QK_W, GLA_QK_W, GLA_V_W, GLA_V_W, GLA_RANK,
             N_DIL * DIL_GROUP_W, N_DIL * DIL_GROUP_W, N_DIL * DIL_GROUP_W, 2 * D_MODEL)
    starts = [sum(sizes[:i]) for i in range(len(sizes))]
    blk = [w_in_l[:, s:s + z] for s, z in zip(starts, sizes)]
    q_g, k_g, v_g, r_g, a_g, q_d, k_d, v_d, gates = blk
    w_main = jnp.concatenate([q_g, k_g, v_g, r_g, gates], axis=1).astype(BF16)
    w_a = jnp.pad(a_g, ((0, 0), (0, LANES - GLA_RANK))).astype(BF16)
    quad = jnp.asarray(_quad_layout())
    w_dil = []
    for g in range(N_DIL):
        gs = slice(g * DIL_GROUP_W, (g + 1) * DIL_GROUP_W)
        w_dil.append(jnp.concatenate([q_d[:, gs][:, quad], k_d[:, gs][:, quad], v_d[:, gs]],
                                     axis=1).astype(BF16))
    return w_main, w_a, w_dil


def _quad_layout():
    half = DIL_HD // 2
    cols = []
    for quad in range(DIL_HEADS // 4):
        for part in range(2):
            for u in range(4):
                start = (4 * quad + u) * DIL_HD + part * half
                cols.extend(range(start, start + half))
    return cols


def _router_pair_tables(w_router, b_router):
    cols_w, cols_b = [], []
    for g in range(N_GROUPS):
        for ea, eb_ in PAIRS:
            lo, hi = EXPERTS_PER_GROUP * g + ea, EXPERTS_PER_GROUP * g + eb_
            w2 = jnp.stack([w_router[:, lo], w_router[:, hi]], axis=1)
            cols_w.append(jnp.pad(w2, ((0, 0), (0, LANES - 2))))
            cols_b.append(jnp.pad(jnp.stack([b_router[lo], b_router[hi]]), (0, LANES - 2)))
    return jnp.stack(cols_w).astype(BF16), jnp.stack(cols_b).reshape(N_BUCKETS, 1, LANES).astype(F32)


def kernel(x, positions, w_in, w_alpha_up, b_alpha, gla_norm, w_gla_o, w_att_o, w_out, ln1_g, ln1_b,
           w_router, b_router, w_e_gate, w_e_up, w_e_down, ln2_g, ln2_b):
    batch, seq, d_model = x.shape
    depth = w_in.shape[0]
    n = batch * seq
    assert d_model == D_MODEL and seq % (DIL_PATTERN[-1][1] * DIL_STEPS) == 0

    cos_t, sin_t = _rope_tables(positions)
    tri = _gla_masks()
    head_of_col = jnp.arange(DIL_GROUP_W) // DIL_HD
    expand = (jnp.arange(LANES)[:, None] == head_of_col[None, :]).astype(BF16)
    wrt = w_router.T.astype(BF16)
    br = jnp.broadcast_to(b_router.astype(F32)[:, None], (N_EXPERTS, LANES))
    wr2, br2 = _router_pair_tables(w_router, b_router)
    n_slots = n + N_BUCKETS * MOE_T

    xf = xin = x.reshape(n, D_MODEL)
    for i in range(depth):
        w_main, w_a, w_dil = _split_w_in(w_in[i])
        h_main = _proj(xin, w_main, BF16, PROJ_TN, n)
        a_proj = _proj(xin, w_a, F32, LANES, n)
        wup = jnp.pad(w_alpha_up[i], ((0, LANES - GLA_RANK), (0, 0))).astype(BF16)
        og = _gla(h_main, a_proj, wup, b_alpha[i].reshape(1, -1).astype(F32),
                  gla_norm[i].reshape(1, -1).astype(F32), tri, batch, seq)
        attn = []
        for g, (_, dil) in enumerate(DIL_PATTERN):
            qkv = _proj_dil(xin, w_dil[g], cos_t, sin_t, batch, seq, dil)
            attn.append(_dil_attn(qkv, batch, seq, dil))
        x1, bucket = _mix(og.reshape(n, GLA_V_W), h_main, attn, xf,
                          w_gla_o[i].astype(BF16), w_att_o[i].astype(BF16), w_out[i].astype(BF16),
                          ln1_g[i].reshape(1, -1), ln1_b[i].reshape(1, -1), wrt, br, expand, batch, seq)
        bucket = bucket.reshape(n)
        rank, counts = _bucket_ranks(bucket, n)
        plan = _sort_plan(bucket, rank, counts, n)
        src = _invert(plan[0], plan[5], n_slots)
        xt = _moe(x1, src, plan, wr2, br2, w_e_gate[i].astype(BF16), w_e_up[i].astype(BF16),
                  w_e_down[i].astype(BF16), ln2_g[i].reshape(1, -1), ln2_b[i].reshape(1, -1))
        xf, xin = _untile(xt, n)
    return xf.reshape(batch, seq, D_MODEL)
```

```python
import functools

import jax
import jax.numpy as jnp
from jax import lax
from jax.experimental import pallas as pl
from jax.experimental.pallas import tpu as pltpu

F32 = jnp.float32
BF16 = jnp.bfloat16
I32 = jnp.int32

D_MODEL = 1024
N_LAYERS = 4
GLA_HEADS = 4
GLA_DK = 128
GLA_DV = 256
GLA_RANK = 16
GLA_TAU = 16.0
GLA_CHUNK = 64
DIL_PATTERN = ((128, 1), (512, 4), (2048, 16))
DIL_HEADS = 8
DIL_HD = 64
DIL_STEPS = 128
ROPE_THETA = 10000.0
N_EXPERTS = 16
N_GROUPS = 4
EXPERTS_PER_GROUP = 4
D_FF = 1024
LN_EPS = 1e-5
RMS_EPS = 1e-6
DN_ALPHA = (2 * N_LAYERS) ** 0.25
NEG_INF = -1e30

GLA_QK_W = GLA_HEADS * GLA_DK
GLA_V_W = GLA_HEADS * GLA_DV
DIL_GROUP_W = DIL_HEADS * DIL_HD
N_DIL = len(DIL_PATTERN)
MAIN_W = 2 * GLA_QK_W + 2 * GLA_V_W + 2 * D_MODEL
GATE_COL = 2 * GLA_QK_W + 2 * GLA_V_W

LANES = 128
TOKEN_ROWS = D_MODEL // LANES
PAIRS = ((0, 1), (0, 2), (0, 3), (1, 2), (1, 3), (2, 3))
N_BUCKETS = N_GROUPS * len(PAIRS)
BUCKET_ROWS = 32

PROJ_TM = 1024
PROJ_MAIN_TM = 2048
PROJ_TN = 1024
GLA_T = 512
GLA_BLOCK = 2 * GLA_CHUNK
MIX_TM = 512
SORT_TS = 1024
MOE_T = 256
ATTN_TQ = 512
VMEM_LIMIT = 56 * 1024 * 1024


def _cp(*sem):
    return pltpu.CompilerParams(dimension_semantics=sem, vmem_limit_bytes=VMEM_LIMIT)


def _rope_kernel(pos_ref, invf_ref, cos_ref, sin_ref):
    ang = pos_ref[...].astype(F32) * invf_ref[...]
    cos_ref[...] = jnp.cos(ang)
    sin_ref[...] = jnp.sin(ang)


def _rope_tables(positions):
    n = positions.size
    tm = 2048
    half = DIL_HD // 2
    inv_freq = jnp.power(jnp.float32(ROPE_THETA), -jnp.arange(0, DIL_HD, 2, dtype=F32) / DIL_HD)
    invf = jnp.tile(inv_freq, LANES // half).reshape(1, LANES)
    return pl.pallas_call(
        _rope_kernel,
        grid=(n // tm,),
        in_specs=[pl.BlockSpec((tm, 1), lambda i: (i, 0)),
                  pl.BlockSpec((1, LANES), lambda i: (0, 0))],
        out_specs=[pl.BlockSpec((tm, LANES), lambda i: (i, 0)),
                   pl.BlockSpec((tm, LANES), lambda i: (i, 0))],
        out_shape=[jax.ShapeDtypeStruct((n, LANES), F32)] * 2,
        compiler_params=_cp("parallel"),
        name="rope_tables",
    )(positions.reshape(n, 1), invf)


def _proj_kernel(x_ref, w_ref, o_ref, xb_ref):
    @pl.when(pl.program_id(1) == 0)
    def _():
        xb_ref[...] = x_ref[...].astype(BF16)

    o_ref[...] = jnp.dot(xb_ref[...], w_ref[...], preferred_element_type=F32).astype(o_ref.dtype)


def _proj(x, w, out_dtype, tn, n):
    k = x.shape[1]
    ncol = w.shape[1]
    tm = min(PROJ_MAIN_TM, n)
    return pl.pallas_call(
        _proj_kernel,
        grid=(n // tm, ncol // tn),
        in_specs=[pl.BlockSpec((tm, k), lambda i, j: (i, 0)),
                  pl.BlockSpec((k, tn), lambda i, j: (0, j))],
        out_specs=pl.BlockSpec((tm, tn), lambda i, j: (i, j)),
        out_shape=jax.ShapeDtypeStruct((n, ncol), out_dtype),
        scratch_shapes=[pltpu.VMEM((tm, k), BF16)],
        compiler_params=_cp("parallel", "arbitrary"),
        name="proj",
    )(x, w)


def _proj_dil_kernel(x_ref, w_ref, cos_ref, sin_ref, o_ref, *slab_refs, dil, tm):
    acc_ref = slab_refs[0] if dil > 1 else None
    acc = jnp.dot(x_ref[...].astype(BF16), w_ref[...], preferred_element_type=F32)
    slabs = DIL_GROUP_W // LANES

    def emit(which, slab, val):
        if dil == 1:
            o_ref[which, 0, 0, :, slab * LANES:(slab + 1) * LANES] = val.astype(BF16)
        else:
            acc_ref[which * slabs + slab] = val

    def col(which, slab):
        start = which * DIL_GROUP_W + slab * LANES
        return acc[:, start:start + LANES]

    for which in range(2):
        scale = DIL_HD ** -0.5 if which == 0 else 1.0
        cos_t = cos_ref[...] * scale
        sin_t = sin_ref[...] * scale
        for quad in range(slabs // 2):
            x1, x2 = col(which, 2 * quad), col(which, 2 * quad + 1)
            emit(which, 2 * quad, x1 * cos_t - x2 * sin_t)
            emit(which, 2 * quad + 1, x2 * cos_t + x1 * sin_t)
    for slab in range(slabs):
        emit(2, slab, col(2, slab))

    if dil > 1:
        rows = tm // dil
        step = 4 if dil > 4 else dil
        outer = dil // step
        part = tm // step
        for s in range(3 * slabs):
            which, slab = divmod(s, slabs)
            if outer > 1:
                tmp_ref = slab_refs[1]
                for lo in range(step):
                    tmp_ref[s, lo * part:(lo + 1) * part, :] = acc_ref[s, pl.ds(lo, part, stride=step), :]
            for r in range(dil):
                hi, lo = divmod(r, step)
                if outer > 1:
                    val = tmp_ref[s, pl.ds(lo * part + hi, rows, stride=outer), :]
                else:
                    val = acc_ref[s, pl.ds(r, rows, stride=dil), :]
                o_ref[which, 0, r, :, slab * LANES:(slab + 1) * LANES] = val.astype(BF16)


def _proj_dil(x, w, cos_t, sin_t, batch, seq, dil):
    n, k = batch * seq, x.shape[1]
    tm = min(PROJ_TM, seq)
    tiles_per_seq = seq // tm
    sub = seq // dil
    kern = functools.partial(_proj_dil_kernel, dil=dil, tm=tm)
    n_slabs = 3 * DIL_GROUP_W // LANES
    return pl.pallas_call(
        kern,
        grid=(n // tm,),
        in_specs=[pl.BlockSpec((tm, k), lambda i: (i, 0)),
                  pl.BlockSpec((k, 3 * DIL_GROUP_W), lambda i: (0, 0)),
                  pl.BlockSpec((tm, LANES), lambda i: (i, 0)),
                  pl.BlockSpec((tm, LANES), lambda i: (i, 0))],
        out_specs=pl.BlockSpec((3, 1, dil, tm // dil, DIL_GROUP_W),
                               lambda i: (0, i // tiles_per_seq, 0, i % tiles_per_seq, 0)),
        out_shape=jax.ShapeDtypeStruct((3, batch, dil, sub, DIL_GROUP_W), BF16),
        scratch_shapes=[pltpu.VMEM((n_slabs, tm, LANES), F32)] * ((dil > 1) + (dil > 4)),
        compiler_params=_cp("parallel"),
        name=f"proj_dil{dil}",
    )(x, w, cos_t, sin_t)


def _split_dot(mat_bf16, val_f32):
    hi = val_f32.astype(BF16)
    lo = (val_f32 - hi.astype(F32)).astype(BF16)
    return (jnp.dot(mat_bf16, hi, preferred_element_type=F32)
            + jnp.dot(mat_bf16, lo, preferred_element_type=F32))


def _gla_kernel(q_ref, k_ref, v_ref, r_ref, a_ref, wup_ref, balpha_ref, gnorm_ref, tri_ref,
                o_ref, state_ref, sprev_ref):
    t = pl.program_id(1)
    n_chunks = GLA_T // GLA_CHUNK

    @pl.when(t == 0)
    def _():
        state_ref[...] = jnp.zeros_like(state_ref)

    z = jnp.dot(a_ref[0].astype(BF16), wup_ref[...], preferred_element_type=F32) + balpha_ref[...]
    log_a = (jnp.minimum(z, 0.0) - jnp.log1p(jnp.exp(-jnp.abs(z)))) / GLA_TAU
    tri = tri_ref[...]
    b = jnp.concatenate([_split_dot(tri, log_a[blk * GLA_BLOCK:(blk + 1) * GLA_BLOCK])
                         for blk in range(GLA_T // GLA_BLOCK)], axis=0)
    b_end = jnp.concatenate(
        [jnp.broadcast_to(b[(n + 1) * GLA_CHUNK - 1:(n + 1) * GLA_CHUNK], (GLA_CHUNK, GLA_QK_W))
         for n in range(n_chunks)], axis=0)
    c = b_end - b
    eb = jnp.exp(b)
    kf = k_ref[0].astype(F32)
    qt = (q_ref[0].astype(F32) * eb * (GLA_DK ** -0.5)).astype(BF16)
    kt = (kf * jnp.exp(-b)).astype(BF16)
    ks = (kf * jnp.exp(c)).astype(BF16)
    causal = tri > 0

    for h in range(GLA_HEADS):
        ck = slice(h * GLA_DK, (h + 1) * GLA_DK)
        cv = slice(h * GLA_DV, (h + 1) * GLA_DV)
        qt_h, kt_h, ks_h = qt[:, ck], kt[:, ck], ks[:, ck]
        v_h = v_ref[0, :, cv]
        intra = []
        for blk in range(GLA_T // GLA_BLOCK):
            rb = slice(blk * GLA_BLOCK, (blk + 1) * GLA_BLOCK)
            att = lax.dot_general(qt_h[rb], kt_h[rb], (((1,), (1,)), ((), ())), preferred_element_type=F32)
            att = jnp.where(causal, att, 0.0).astype(BF16)
            intra.append(jnp.dot(att, v_h[rb], preferred_element_type=F32))
        o_h = jnp.concatenate(intra, axis=0)

        state = state_ref[h]
        for n in range(n_chunks):
            rs = slice(n * GLA_CHUNK, (n + 1) * GLA_CHUNK)
            contrib = lax.dot_general(v_h[rs], ks_h[rs], (((0,), (0,)), ((), ())),
                                      preferred_element_type=F32)
            sprev_ref[h, n] = state.astype(BF16)
            decay = eb[(n + 1) * GLA_CHUNK - 1:(n + 1) * GLA_CHUNK, ck]
            state = state * decay + contrib
        state_ref[h] = state

        inter = [lax.dot_general(qt_h[n * GLA_CHUNK:(n + 1) * GLA_CHUNK], sprev_ref[h, n],
                                 (((1,), (1,)), ((), ())), preferred_element_type=F32)
                 for n in range(n_chunks)]
        o_h = o_h + jnp.concatenate(inter, axis=0)

        o_h = o_h * lax.rsqrt(jnp.mean(jnp.square(o_h), axis=-1, keepdims=True) + RMS_EPS)
        o_h = o_h * gnorm_ref[:, cv]
        r_h = r_ref[0, :, cv].astype(F32)
        o_ref[0, :, cv] = (o_h * (r_h * jax.nn.sigmoid(r_h))).astype(o_ref.dtype)


def _gla(h_main, a_proj, wup, balpha, gnorm, tri, batch, seq):
    h3 = h_main.reshape(batch, seq, MAIN_W)
    a3 = a_proj.reshape(batch, seq, LANES)
    n_chunks = GLA_T // GLA_CHUNK
    return pl.pallas_call(
        _gla_kernel,
        grid=(batch, seq // GLA_T),
        in_specs=[pl.BlockSpec((1, GLA_T, GLA_QK_W), lambda b, t: (b, t, 0)),
                  pl.BlockSpec((1, GLA_T, GLA_QK_W), lambda b, t: (b, t, 1)),
                  pl.BlockSpec((1, GLA_T, GLA_V_W), lambda b, t: (b, t, 1)),
                  pl.BlockSpec((1, GLA_T, GLA_V_W), lambda b, t: (b, t, 2)),
                  pl.BlockSpec((1, GLA_T, LANES), lambda b, t: (b, t, 0)),
                  pl.BlockSpec((LANES, GLA_QK_W), lambda b, t: (0, 0)),
                  pl.BlockSpec((1, GLA_QK_W), lambda b, t: (0, 0)),
                  pl.BlockSpec((1, GLA_V_W), lambda b, t: (0, 0)),
                  pl.BlockSpec((GLA_BLOCK, GLA_BLOCK), lambda b, t: (0, 0))],
        out_specs=pl.BlockSpec((1, GLA_T, GLA_V_W), lambda b, t: (b, t, 0)),
        out_shape=jax.ShapeDtypeStruct((batch, seq, GLA_V_W), BF16),
        scratch_shapes=[pltpu.VMEM((GLA_HEADS, GLA_DV, GLA_DK), F32),
                        pltpu.VMEM((GLA_HEADS, n_chunks, GLA_DV, GLA_DK), BF16)],
        compiler_params=_cp("parallel", "arbitrary"),
        name="gla",
    )(h3, h3, h3, h3, a3, wup, balpha, gnorm, tri)


def _gla_masks():
    i = jnp.arange(GLA_BLOCK)[:, None]
    j = jnp.arange(GLA_BLOCK)[None, :]
    same = (i // GLA_CHUNK) == (j // GLA_CHUNK)
    return (same & (j <= i)).astype(BF16)


def _dil_attn_kernel(q_ref, kc_ref, kp_ref, vc_ref, vp_ref, o_ref, lse_ref, *, tq):
    n = pl.program_id(2)
    st = DIL_STEPS
    qi = lax.broadcasted_iota(I32, (st, 2 * st), 0)
    kj = lax.broadcasted_iota(I32, (st, 2 * st), 1)
    in_prev = kj < st
    band_bias = jnp.where(jnp.where(in_prev, kj - qi, qi + st - kj) >= 0, 0.0, NEG_INF)
    prev_cols = jnp.where(in_prev, 1.0, 0.0)
    lane = lax.broadcasted_iota(I32, (st, LANES), 1)
    low = lane < DIL_HD
    ones = jnp.ones((2 * st, LANES), BF16)
    nt = (((1,), (1,)), ((), ()))

    no_prev = jnp.where(n == 0, NEG_INF, 0.0)
    first_bias = band_bias + prev_cols * no_prev

    def window(cur_ref, prev_ref, i, cs):
        if i == 0:
            return jnp.concatenate([prev_ref[0, 0, 0, :, cs], cur_ref[0, 0, 0, 0:st, cs]], axis=0)
        return cur_ref[0, 0, 0, (i - 1) * st:(i + 1) * st, cs]

    quad_w = 2 * LANES
    quad_lane = lax.broadcasted_iota(I32, (st, quad_w), 1)
    head_in_quad = (quad_lane % LANES) // (DIL_HD // 2)

    for i in range(tq // st):
        r0 = i * st
        bias = first_bias if i == 0 else band_bias
        bias4 = jnp.concatenate([bias] * 4, axis=0)
        stats = jnp.zeros((st, LANES), F32)
        outs = []
        for quad in range(DIL_HEADS // 4):
            qs = slice(quad * quad_w, (quad + 1) * quad_w)
            qq = q_ref[0, 0, 0, r0:r0 + st, qs]
            zero = jnp.zeros_like(qq)
            lhs = jnp.concatenate([jnp.where(head_in_quad == u, qq, zero) for u in range(4)], axis=0)
            s = lax.dot_general(lhs, window(kc_ref, kp_ref, i, qs), nt,
                                preferred_element_type=F32) + bias4
            m = jnp.max(s, axis=-1, keepdims=True)
            pexp = jnp.exp(s - m).astype(BF16)
            m_rep = jnp.broadcast_to(m, (4 * st, LANES))
            for pair in range(2):
                p = 2 * quad + pair
                cs = slice(p * LANES, (p + 1) * LANES)
                v_ext = jnp.concatenate([window(vc_ref, vp_ref, i, cs), ones], axis=1)
                o = jnp.dot(pexp[2 * pair * st:(2 * pair + 2) * st], v_ext,
                            preferred_element_type=F32)
                outs.append(jnp.where(low, o[0:st, 0:LANES], o[st:2 * st, 0:LANES]))
                l_rep = o[:, LANES:2 * LANES]
                for half in range(2):
                    rs = slice(half * st, (half + 1) * st)
                    ms = slice((2 * pair + half) * st, (2 * pair + half + 1) * st)
                    stats = jnp.where(lane == 2 * p + half, m_rep[ms], stats)
                    stats = jnp.where(lane == DIL_HEADS + 2 * p + half, l_rep[rs], stats)
        o_ref[0, 0, r0:r0 + st, :] = jnp.concatenate(outs, axis=1).astype(o_ref.dtype)
        lse_ref[0, 0, r0:r0 + st, :] = stats


def _dil_attn(qkv, batch, seq, dil):
    sub = seq // dil
    tq = min(ATTN_TQ, sub)
    per = tq // DIL_STEPS

    def cur(which):
        return pl.BlockSpec((1, 1, 1, tq, DIL_GROUP_W), lambda b, r, n: (which, b, r, n, 0))

    def prev(which):
        return pl.BlockSpec((1, 1, 1, DIL_STEPS, DIL_GROUP_W),
                            lambda b, r, n: (which, b, r, jnp.maximum(n * per - 1, 0), 0))

    return pl.pallas_call(
        functools.partial(_dil_attn_kernel, tq=tq),
        grid=(batch, dil, sub // tq),
        in_specs=[cur(0), cur(1), prev(1), cur(2), prev(2)],
        out_specs=[pl.BlockSpec((1, 1, tq, DIL_GROUP_W), lambda b, r, n: (b, r, n, 0)),
                   pl.BlockSpec((1, 1, tq, LANES), lambda b, r, n: (b, r, n, 0))],
        out_shape=[jax.ShapeDtypeStruct((batch, dil, sub, DIL_GROUP_W), BF16),
                   jax.ShapeDtypeStruct((batch, dil, sub, LANES), F32)],
        compiler_params=_cp("parallel", "parallel", "arbitrary"),
        name=f"dil_attn{dil}",
    )(qkv, qkv, qkv, qkv, qkv)


def _store_token_tiles(ref, val):
    rows = val.shape[0]
    for c in range(TOKEN_ROWS):
        ref[pl.ds(c, rows, stride=TOKEN_ROWS), :] = val[:, c * LANES:(c + 1) * LANES]


def _load_token_tiles(ref, rows):
    return jnp.concatenate([ref[pl.ds(c, rows, stride=TOKEN_ROWS), :] for c in range(TOKEN_ROWS)], axis=1)


def _layer_norm(x, g, b):
    mu = jnp.mean(x, axis=-1, keepdims=True)
    var = jnp.mean(jnp.square(x - mu), axis=-1, keepdims=True)
    return (x - mu) * lax.rsqrt(var + LN_EPS) * g + b


def _mix_kernel(og_ref, g1_ref, g2_ref, o0_ref, o1_ref, o2_ref, l0_ref, l1_ref, l2_ref, x_ref,
                wgo_ref, wao_ref, wout_ref, lng_ref, lnb_ref, wrt_ref, br_ref, expand_ref,
                x1_ref, bucket_ref, osc_ref, lsc_ref):
    tm = x_ref.shape[0]

    def interleaved(o_ref, l_ref, dil, slot):
        if dil == 1:
            return o_ref[0, 0].astype(F32), l_ref[0, 0]
        rows = tm // dil
        chunks = DIL_GROUP_W // LANES
        for r in range(dil):
            for c in range(chunks):
                osc_ref[slot, c, pl.ds(r, rows, stride=dil), :] = (
                    o_ref[0, r, :, c * LANES:(c + 1) * LANES].astype(F32))
            lsc_ref[slot, pl.ds(r, rows, stride=dil), :] = l_ref[0, r]
        return jnp.concatenate([osc_ref[slot, c] for c in range(chunks)], axis=1), lsc_ref[slot]

    groups = [interleaved(o_ref, l_ref, dil, slot)
              for slot, (o_ref, l_ref, (_, dil)) in enumerate(
                  zip((o0_ref, o1_ref, o2_ref), (l0_ref, l1_ref, l2_ref), DIL_PATTERN))]
    stats = [g[1] for g in groups]
    m_max = jnp.maximum(jnp.maximum(stats[0], stats[1]), stats[2])
    es = [jnp.exp(s - m_max) for s in stats]
    ls = [pltpu.roll(s, LANES - DIL_HEADS, 1) for s in stats]
    inv = 1.0 / (es[0] * ls[0] + es[1] * ls[1] + es[2] * ls[2])
    head_lane = lax.broadcasted_iota(I32, (tm, LANES), 1) < DIL_HEADS
    att = jnp.zeros((tm, DIL_GROUP_W), F32)
    for (num_g, _), e in zip(groups, es):
        coef = jnp.where(head_lane, e * inv, 0.0)
        att = att + _split_dot_rhs(coef, expand_ref[...]) * num_g

    y_att = jnp.dot(att.astype(BF16), wao_ref[...], preferred_element_type=F32)
    y_gla = jnp.dot(og_ref[...], wgo_ref[...], preferred_element_type=F32)
    mixed = (jax.nn.sigmoid(g1_ref[...].astype(F32)) * y_gla
             + jax.nn.sigmoid(g2_ref[...].astype(F32)) * y_att)
    mix = jnp.dot(mixed.astype(BF16), wout_ref[...], preferred_element_type=F32)
    x1 = _layer_norm(DN_ALPHA * x_ref[...] + mix, lng_ref[...], lnb_ref[...])
    _store_token_tiles(x1_ref, x1)

    logits = lax.dot_general(wrt_ref[...], x1.astype(BF16), (((1,), (1,)), ((), ())),
                             preferred_element_type=F32) + br_ref[:, 0:1]
    e = jnp.exp(logits - jnp.max(logits, axis=0, keepdims=True))
    probs = e / jnp.sum(e, axis=0, keepdims=True)
    best = None
    for g in range(N_GROUPS):
        for pi, (ea, eb_) in enumerate(PAIRS):
            cand = probs[4 * g + ea:4 * g + ea + 1] + probs[4 * g + eb_:4 * g + eb_ + 1]
            idx = g * len(PAIRS) + pi
            if best is None:
                best, bidx = cand, jnp.zeros(cand.shape, I32)
            else:
                better = cand > best
                best = jnp.where(better, cand, best)
                bidx = jnp.where(better, idx, bidx)
    bucket_ref[0] = bidx


def _split_dot_rhs(val_f32, mat_bf16):
    hi = val_f32.astype(BF16)
    lo = (val_f32 - hi.astype(F32)).astype(BF16)
    return (jnp.dot(hi, mat_bf16, preferred_element_type=F32)
            + jnp.dot(lo, mat_bf16, preferred_element_type=F32))


def _mix(og, h_main, attn, x, wgo, wao, wout, lng, lnb, wrt, br, expand, batch, seq):
    n = batch * seq
    tm = min(MIX_TM, seq)
    tps = seq // tm
    row = lambda i: (i, 0)
    const = lambda i: (0, 0)
    o_specs, l_specs, o_args, l_args = [], [], [], []
    for (o_g, lse_g), (_, dil) in zip(attn, DIL_PATTERN):
        o_specs.append(pl.BlockSpec((1, dil, tm // dil, DIL_GROUP_W), lambda i: (i // tps, 0, i % tps, 0)))
        l_specs.append(pl.BlockSpec((1, dil, tm // dil, LANES), lambda i: (i // tps, 0, i % tps, 0)))
        o_args.append(o_g)
        l_args.append(lse_g)
    gate_blk = GATE_COL // D_MODEL
    return pl.pallas_call(
        _mix_kernel,
        grid=(n // tm,),
        in_specs=[pl.BlockSpec((tm, GLA_V_W), row),
                  pl.BlockSpec((tm, D_MODEL), lambda i: (i, gate_blk)),
                  pl.BlockSpec((tm, D_MODEL), lambda i: (i, gate_blk + 1)),
                  *o_specs, *l_specs,
                  pl.BlockSpec((tm, D_MODEL), row),
                  pl.BlockSpec((GLA_V_W, D_MODEL), const),
                  pl.BlockSpec((DIL_GROUP_W, D_MODEL), const),
                  pl.BlockSpec((D_MODEL, D_MODEL), const),
                  pl.BlockSpec((1, D_MODEL), const),
                  pl.BlockSpec((1, D_MODEL), const),
                  pl.BlockSpec((N_EXPERTS, D_MODEL), const),
                  pl.BlockSpec((N_EXPERTS, LANES), const),
                  pl.BlockSpec((LANES, DIL_GROUP_W), const)],
        out_specs=[pl.BlockSpec((tm * TOKEN_ROWS, LANES), row),
                   pl.BlockSpec((1, 1, tm), lambda i: (i, 0, 0))],
        out_shape=[jax.ShapeDtypeStruct((n * TOKEN_ROWS, LANES), F32),
                   jax.ShapeDtypeStruct((n // tm, 1, tm), I32)],
        scratch_shapes=[pltpu.VMEM((N_DIL, DIL_GROUP_W // LANES, tm, LANES), F32),
                        pltpu.VMEM((N_DIL, tm, LANES), F32)],
        compiler_params=_cp("parallel"),
        name="mix",
    )(og, h_main, h_main, *o_args, *l_args, x, wgo, wao, wout, lng, lnb, wrt, br, expand)


def _rank_kernel(bucket_ref, upper_ref, rank_ref, counts_ref, carry_ref):
    @pl.when(pl.program_id(0) == 0)
    def _():
        carry_ref[...] = jnp.zeros_like(carry_ref)

    b = bucket_ref[0]
    ts = b.shape[1]
    onehot = lax.broadcasted_iota(I32, (BUCKET_ROWS, ts), 0) == b
    oh_f = jnp.where(onehot, 1.0, 0.0)
    before = jnp.dot(oh_f.astype(BF16), upper_ref[...], preferred_element_type=F32)
    carry = carry_ref[...]
    rank = jnp.sum(oh_f * (before + carry[:, 0:1]), axis=0, keepdims=True)
    rank_ref[0] = rank.astype(I32)
    carry = carry + jnp.sum(oh_f, axis=1, keepdims=True)
    carry_ref[...] = carry
    counts_ref[...] = carry.astype(I32)


def _bucket_ranks(bucket, n):
    ts = min(SORT_TS, n)
    b3 = bucket.reshape(n // ts, 1, ts)
    j = jnp.arange(ts)
    upper = (j[:, None] < j[None, :]).astype(BF16)
    rank, counts = pl.pallas_call(
        _rank_kernel,
        grid=(n // ts,),
        in_specs=[pl.BlockSpec((1, 1, ts), lambda i: (i, 0, 0)),
                  pl.BlockSpec((ts, ts), lambda i: (0, 0))],
        out_specs=[pl.BlockSpec((1, 1, ts), lambda i: (i, 0, 0)),
                   pl.BlockSpec((BUCKET_ROWS, LANES), lambda i: (0, 0))],
        out_shape=[jax.ShapeDtypeStruct((n // ts, 1, ts), I32),
                   jax.ShapeDtypeStruct((BUCKET_ROWS, LANES), I32)],
        scratch_shapes=[pltpu.VMEM((BUCKET_ROWS, LANES), F32)],
        compiler_params=_cp("arbitrary"),
        name="bucket_rank",
    )(b3, upper)
    return rank.reshape(n), counts[:N_BUCKETS, 0]


def _sort_plan(bucket, rank, counts, n):
    tiles = (counts + MOE_T - 1) // MOE_T
    tile_end = jnp.cumsum(tiles)
    offsets = (tile_end - tiles) * MOE_T
    dest = offsets[bucket] + rank
    n_tiles = n // MOE_T + N_BUCKETS
    tile_ids = jnp.arange(n_tiles, dtype=I32)
    total = tile_end[-1]
    last = jnp.minimum(tile_ids, total - 1)
    tb = jnp.sum((last[:, None] >= tile_end[None, :]).astype(I32), axis=1)
    tb = jnp.minimum(tb, N_BUCKETS - 1)
    n_valid = jnp.clip(counts[tb] - (tile_ids - (tile_end - tiles)[tb]) * MOE_T, 0, MOE_T)
    n_valid = jnp.where(tile_ids < total, n_valid, 0)
    pair = jnp.array(PAIRS, dtype=I32)
    e_lo = (tb // len(PAIRS)) * EXPERTS_PER_GROUP + pair[tb % len(PAIRS), 0]
    e_hi = (tb // len(PAIRS)) * EXPERTS_PER_GROUP + pair[tb % len(PAIRS), 1]
    pad_lo = jnp.concatenate([offsets + counts, total[None] * MOE_T])
    pad_hi = jnp.concatenate([tile_end * MOE_T, jnp.full((1,), n_tiles * MOE_T, I32)])
    pad_ranges = jnp.stack([pad_lo, pad_hi]).astype(I32)
    return dest.astype(I32), tb, e_lo, e_hi, n_valid.astype(I32), pad_ranges


def _invert_kernel(pad_ref, dest_ref, src_ref, table_ref, sem, *, chunk):
    i = pl.program_id(0)

    @pl.when(i == 0)
    def _():
        def clear(j, carry):
            table_ref[j] = 0
            return carry
        for b in range(N_BUCKETS + 1):
            lax.fori_loop(pad_ref[0, b], pad_ref[1, b], clear, 0)

    def put(j, carry):
        table_ref[dest_ref[0, 0, j]] = i * chunk + j
        return carry
    lax.fori_loop(0, chunk, put, 0, unroll=8)

    @pl.when(i == pl.num_programs(0) - 1)
    def _():
        copy = pltpu.make_async_copy(table_ref, src_ref, sem)
        copy.start()
        copy.wait()


def _invert(dest, pad_ranges, n_slots):
    n = dest.shape[0]
    chunk = min(2048, n)
    return pl.pallas_call(
        functools.partial(_invert_kernel, chunk=chunk),
        grid=(n // chunk,),
        in_specs=[pl.BlockSpec(memory_space=pltpu.SMEM),
                  pl.BlockSpec((1, 1, chunk), lambda i: (i, 0, 0), memory_space=pltpu.SMEM)],
        out_specs=pl.BlockSpec(memory_space=pl.ANY),
        out_shape=jax.ShapeDtypeStruct((n_slots,), I32),
        scratch_shapes=[pltpu.SMEM((n_slots,), I32), pltpu.SemaphoreType.DMA(())],
        compiler_params=_cp("arbitrary"),
        name="invert_perm",
    )(pad_ranges, dest.reshape(n // chunk, 1, chunk))


def _moe_kernel(tb_ref, elo_ref, ehi_ref, nvalid_ref, prv_ref, src_ref, nxt_ref, x_hbm, wr2_ref, br2_ref,
                wg_lo_ref, wu_lo_ref, wd_lo_ref, wg_hi_ref, wu_hi_ref, wd_hi_ref, lng_ref, lnb_ref,
                out_hbm, xbuf_ref, ybuf_ref, gsem, ssem, *, n_tokens):
    del tb_ref, elo_ref, ehi_ref
    t = pl.program_id(0)
    last = pl.num_programs(0) - 1
    slot = t % 2
    other = 1 - slot
    tile = MOE_T
    n_chunks = 8
    per_chunk = tile // n_chunks

    def token_rows(tok):
        if isinstance(tok, int):
            return pl.ds(tok * TOKEN_ROWS, TOKEN_ROWS)
        return pl.ds(pl.multiple_of(tok * TOKEN_ROWS, TOKEN_ROWS), TOKEN_ROWS)

    def gather_copy(idx_ref, r, sl):
        return pltpu.make_async_copy(x_hbm.at[token_rows(idx_ref[0, 0, r])],
                                     xbuf_ref.at[sl, token_rows(r)], gsem.at[sl])

    def scatter_copy(dst, r, sl):
        return pltpu.make_async_copy(ybuf_ref.at[sl, token_rows(r)], out_hbm.at[token_rows(dst)], ssem.at[sl])

    def scatter_rows(idx_ref, n_valid, rows, sl):
        for r in rows:
            dst = jnp.where(r < n_valid, idx_ref[0, 0, r], n_tokens + sl * tile + r)
            scatter_copy(dst, r, sl).start(priority=r % 2)

    def wait_scatter(sl):
        for r in range(tile):
            scatter_copy(0, r, sl).wait()

    prev_valid = jnp.where(t > 0, nvalid_ref[jnp.maximum(t - 1, 0)], 0)

    def issue(chunk):
        rows = range(chunk * per_chunk, (chunk + 1) * per_chunk)
        for r in rows:
            gather_copy(nxt_ref, r, other).start(priority=r % 2)
        scatter_rows(prv_ref, prev_valid, rows, other)

    @pl.when(t == 0)
    def _():
        ybuf_ref[...] = jnp.zeros_like(ybuf_ref)
        for r in range(tile):
            gather_copy(src_ref, r, 0).start()
        scatter_rows(src_ref, 0, range(tile), 0)

    for r in range(tile):
        gather_copy(src_ref, r, slot).wait()

    @pl.when(nvalid_ref[t] == 0)
    def _():
        wait_scatter(slot)
        for chunk in range(n_chunks):
            issue(chunk)

    @pl.when(nvalid_ref[t] != 0)
    def _():
        x = _load_token_tiles(xbuf_ref.at[slot], tile)
        xb = x.astype(BF16)
        issue(0)
        issue(1)
        l2 = jnp.dot(xb, wr2_ref[0], preferred_element_type=F32) + br2_ref[0]
        g_lo = jax.nn.sigmoid(l2[:, 0:1] - l2[:, 1:2])
        g_hi = jax.nn.sigmoid(l2[:, 1:2] - l2[:, 0:1])
        issue(2)
        issue(3)

        def expert(wg_ref, wu_ref, wd_ref, chunks):
            a = jnp.dot(xb, wg_ref[0], preferred_element_type=F32)
            for chunk in chunks[:2]:
                issue(chunk)
            u = jnp.dot(xb, wu_ref[0], preferred_element_type=F32)
            for chunk in chunks[2:]:
                issue(chunk)
            hidden = (a * jax.nn.sigmoid(a) * u).astype(BF16)
            return jnp.dot(hidden, wd_ref[0], preferred_element_type=F32)

        y = g_lo * expert(wg_lo_ref, wu_lo_ref, wd_lo_ref, (4, 5, 6, 7))
        y = y + g_hi * expert(wg_hi_ref, wu_hi_ref, wd_hi_ref, ())
        y = _layer_norm(DN_ALPHA * x + y, lng_ref[...], lnb_ref[...])
        wait_scatter(slot)
        _store_token_tiles(ybuf_ref.at[slot], y)

    @pl.when(t == last)
    def _():
        for r in range(tile):
            gather_copy(nxt_ref, r, other).wait()
        scatter_rows(src_ref, nvalid_ref[t], range(tile), slot)
        for sl in range(2):
            wait_scatter(sl)


def _moe(x1, src, plan, wr2, br2, wg, wu, wd, lng, lnb):
    _, tb, e_lo, e_hi, n_valid, _ = plan
    n = x1.shape[0] // TOKEN_ROWS
    n_tiles = src.shape[0] // MOE_T
    src3 = src.reshape(n_tiles, 1, MOE_T)
    lo = lambda t, tb, elo, ehi, v: (elo[t], 0, 0)
    hi = lambda t, tb, elo, ehi, v: (ehi[t], 0, 0)
    const = lambda t, tb, elo, ehi, v: (0, 0)
    wblk = (1, D_MODEL, D_FF)
    idx_blk = (1, 1, MOE_T)
    grid_spec = pltpu.PrefetchScalarGridSpec(
        num_scalar_prefetch=4,
        grid=(n_tiles,),
        in_specs=[pl.BlockSpec(idx_blk, lambda t, tb, elo, ehi, v: (jnp.maximum(t - 1, 0), 0, 0),
                               memory_space=pltpu.SMEM),
                  pl.BlockSpec(idx_blk, lambda t, tb, elo, ehi, v: (t, 0, 0), memory_space=pltpu.SMEM),
                  pl.BlockSpec(idx_blk, lambda t, tb, elo, ehi, v: (jnp.minimum(t + 1, n_tiles - 1), 0, 0),
                               memory_space=pltpu.SMEM),
                  pl.BlockSpec(memory_space=pl.ANY),
                  pl.BlockSpec((1, D_MODEL, LANES), lambda t, tb, elo, ehi, v: (tb[t], 0, 0)),
                  pl.BlockSpec((1, 1, LANES), lambda t, tb, elo, ehi, v: (tb[t], 0, 0)),
                  pl.BlockSpec(wblk, lo), pl.BlockSpec(wblk, lo), pl.BlockSpec((1, D_FF, D_MODEL), lo),
                  pl.BlockSpec(wblk, hi), pl.BlockSpec(wblk, hi), pl.BlockSpec((1, D_FF, D_MODEL), hi),
                  pl.BlockSpec((1, D_MODEL), const), pl.BlockSpec((1, D_MODEL), const)],
        out_specs=pl.BlockSpec(memory_space=pl.ANY),
        scratch_shapes=[pltpu.VMEM((2, MOE_T * TOKEN_ROWS, LANES), F32),
                        pltpu.VMEM((2, MOE_T * TOKEN_ROWS, LANES), F32),
                        pltpu.SemaphoreType.DMA((2,)), pltpu.SemaphoreType.DMA((2,))],
    )
    return pl.pallas_call(
        functools.partial(_moe_kernel, n_tokens=n),
        grid_spec=grid_spec,
        out_shape=jax.ShapeDtypeStruct(((n + 2 * MOE_T) * TOKEN_ROWS, LANES), F32),
        compiler_params=_cp("arbitrary"),
        name="moe",
    )(tb, e_lo, e_hi, n_valid, src3, src3, src3, x1, wr2, br2, wg, wu, wd, wg, wu, wd, lng, lnb)


def _untile_kernel(xt_ref, x_ref, xb_ref):
    x = _load_token_tiles(xt_ref, x_ref.shape[0])
    x_ref[...] = x
    xb_ref[...] = x.astype(BF16)


def _untile(xt, n):
    tm = min(PROJ_TM, n)
    return pl.pallas_call(
        _untile_kernel,
        grid=(n // tm,),
        in_specs=[pl.BlockSpec((tm * TOKEN_ROWS, LANES), lambda i: (i, 0))],
        out_specs=[pl.BlockSpec((tm, D_MODEL), lambda i: (i, 0))] * 2,
        out_shape=[jax.ShapeDtypeStruct((n, D_MODEL), F32), jax.ShapeDtypeStruct((n, D_MODEL), BF16)],
        compiler_params=_cp("parallel"),
        name="untile",
    )(xt)


def _split_w_in(w_in_l):
    sizes = (GLA_QK_W, GLA_QK_W, GLA_V_W, GLA_V_W, GLA_RANK,
             N_DIL * DIL_GROUP_W, N_DIL * DIL_GROUP_W, N_DIL * DIL_GROUP_W, 2 * D_MODEL)
    starts = [sum(sizes[:i]) for i in range(len(sizes))]
    blk = [w_in_l[:, s:s + z] for s, z in zip(starts, sizes)]
    q_g, k_g, v_g, r_g, a_g, q_d, k_d, v_d, gates = blk
    w_main = jnp.concatenate([q_g, k_g, v_g, r_g, gates], axis=1).astype(BF16)
    w_a = jnp.pad(a_g, ((0, 0), (0, LANES - GLA_RANK))).astype(BF16)
    quad = jnp.asarray(_quad_layout())
    w_dil = []
    for g in range(N_DIL):
        gs = slice(g * DIL_GROUP_W, (g + 1) * DIL_GROUP_W)
        w_dil.append(jnp.concatenate([q_d[:, gs][:, quad], k_d[:, gs][:, quad], v_d[:, gs]],
                                     axis=1).astype(BF16))
    return w_main, w_a, w_dil


def _quad_layout():
    half = DIL_HD // 2
    cols = []
    for quad in range(DIL_HEADS // 4):
        for part in range(2):
            for u in range(4):
                start = (4 * quad + u) * DIL_HD + part * half
                cols.extend(range(start, start + half))
    return cols


def _router_pair_tables(w_router, b_router):
    cols_w, cols_b = [], []
    for g in range(N_GROUPS):
        for ea, eb_ in PAIRS:
            lo, hi = EXPERTS_PER_GROUP * g + ea, EXPERTS_PER_GROUP * g + eb_
            w2 = jnp.stack([w_router[:, lo], w_router[:, hi]], axis=1)
            cols_w.append(jnp.pad(w2, ((0, 0), (0, LANES - 2))))
            cols_b.append(jnp.pad(jnp.stack([b_router[lo], b_router[hi]]), (0, LANES - 2)))
    return jnp.stack(cols_w).astype(BF16), jnp.stack(cols_b).reshape(N_BUCKETS, 1, LANES).astype(F32)


def kernel(x, positions, w_in, w_alpha_up, b_alpha, gla_norm, w_gla_o, w_att_o, w_out, ln1_g, ln1_b,
           w_router, b_router, w_e_gate, w_e_up, w_e_down, ln2_g, ln2_b):
    batch, seq, d_model = x.shape
    depth = w_in.shape[0]
    n = batch * seq
    assert d_model == D_MODEL and seq % (DIL_PATTERN[-1][1] * DIL_STEPS) == 0

    cos_t, sin_t = _rope_tables(positions)
    tri = _gla_masks()
    head_of_col = jnp.arange(DIL_GROUP_W) // DIL_HD
    expand = (jnp.arange(LANES)[:, None] == head_of_col[None, :]).astype(BF16)
    wrt = w_router.T.astype(BF16)
    br = jnp.broadcast_to(b_router.astype(F32)[:, None], (N_EXPERTS, LANES))
    wr2, br2 = _router_pair_tables(w_router, b_router)
    n_slots = n + N_BUCKETS * MOE_T

    xf = xin = x.reshape(n, D_MODEL)
    for i in range(depth):
        w_main, w_a, w_dil = _split_w_in(w_in[i])
        h_main = _proj(xin, w_main, BF16, PROJ_TN, n)
        a_proj = _proj(xin, w_a, F32, LANES, n)
        wup = jnp.pad(w_alpha_up[i], ((0, LANES - GLA_RANK), (0, 0))).astype(BF16)
        og = _gla(h_main, a_proj, wup, b_alpha[i].reshape(1, -1).astype(F32),
                  gla_norm[i].reshape(1, -1).astype(F32), tri, batch, seq)
        attn = []
        for g, (_, dil) in enumerate(DIL_PATTERN):
            qkv = _proj_dil(xin, w_dil[g], cos_t, sin_t, batch, seq, dil)
            attn.append(_dil_attn(qkv, batch, seq, dil))
        x1, bucket = _mix(og.reshape(n, GLA_V_W), h_main, attn, xf,
                          w_gla_o[i].astype(BF16), w_att_o[i].astype(BF16), w_out[i].astype(BF16),
                          ln1_g[i].reshape(1, -1), ln1_b[i].reshape(1, -1), wrt, br, expand, batch, seq)
        bucket = bucket.reshape(n)
        rank, counts = _bucket_ranks(bucket, n)
        plan = _sort_plan(bucket, rank, counts, n)
        src = _invert(plan[0], plan[5], n_slots)
        xt = _moe(x1, src, plan, wr2, br2, w_e_gate[i].astype(BF16), w_e_up[i].astype(BF16),
                  w_e_down[i].astype(BF16), ln2_g[i].reshape(1, -1), ln2_b[i].reshape(1, -1))
        xf, xin = _untile(xt, n)
    return xf.reshape(batch, seq, D_MODEL)
```

```python
import functools

import jax
import jax.numpy as jnp
from jax import lax
from jax.experimental import pallas as pl
from jax.experimental.pallas import tpu as pltpu

F32 = jnp.float32
BF16 = jnp.bfloat16
I32 = jnp.int32

D_MODEL = 1024
N_LAYERS = 4
GLA_HEADS = 4
GLA_DK = 128
GLA_DV = 256
GLA_RANK = 16
GLA_TAU = 16.0
GLA_CHUNK = 64
DIL_PATTERN = ((128, 1), (512, 4), (2048, 16))
DIL_HEADS = 8
DIL_HD = 64
DIL_STEPS = 128
ROPE_THETA = 10000.0
N_EXPERTS = 16
N_GROUPS = 4
EXPERTS_PER_GROUP = 4
D_FF = 1024
LN_EPS = 1e-5
RMS_EPS = 1e-6
DN_ALPHA = (2 * N_LAYERS) ** 0.25
NEG_INF = -1e30

GLA_QK_W = GLA_HEADS * GLA_DK
GLA_V_W = GLA_HEADS * GLA_DV
DIL_GROUP_W = DIL_HEADS * DIL_HD
N_DIL = len(DIL_PATTERN)
MAIN_W = 2 * GLA_QK_W + 2 * GLA_V_W + 2 * D_MODEL
GATE_COL = 2 * GLA_QK_W + 2 * GLA_V_W

LANES = 128
TOKEN_ROWS = D_MODEL // LANES
PAIRS = ((0, 1), (0, 2), (0, 3), (1, 2), (1, 3), (2, 3))
N_BUCKETS = N_GROUPS * len(PAIRS)
BUCKET_ROWS = 32

PROJ_TM = 1024
PROJ_MAIN_TM = 2048
PROJ_TN = 1024
GLA_T = 512
GLA_BLOCK = 2 * GLA_CHUNK
MIX_TM = 512
SORT_TS = 1024
MOE_T = 256
ATTN_TQ = 512
VMEM_LIMIT = 56 * 1024 * 1024


def _cp(*sem):
    return pltpu.CompilerParams(dimension_semantics=sem, vmem_limit_bytes=VMEM_LIMIT)


def _rope_kernel(pos_ref, invf_ref, cos_ref, sin_ref):
    ang = pos_ref[...].astype(F32) * invf_ref[...]
    cos_ref[...] = jnp.cos(ang)
    sin_ref[...] = jnp.sin(ang)


def _rope_tables(positions):
    n = positions.size
    tm = 2048
    half = DIL_HD // 2
    inv_freq = jnp.power(jnp.float32(ROPE_THETA), -jnp.arange(0, DIL_HD, 2, dtype=F32) / DIL_HD)
    invf = jnp.tile(inv_freq, LANES // half).reshape(1, LANES)
    return pl.pallas_call(
        _rope_kernel,
        grid=(n // tm,),
        in_specs=[pl.BlockSpec((tm, 1), lambda i: (i, 0)),
                  pl.BlockSpec((1, LANES), lambda i: (0, 0))],
        out_specs=[pl.BlockSpec((tm, LANES), lambda i: (i, 0)),
                   pl.BlockSpec((tm, LANES), lambda i: (i, 0))],
        out_shape=[jax.ShapeDtypeStruct((n, LANES), F32)] * 2,
        compiler_params=_cp("parallel"),
        name="rope_tables",
    )(positions.reshape(n, 1), invf)


def _proj_kernel(x_ref, w_ref, o_ref, xb_ref):
    @pl.when(pl.program_id(1) == 0)
    def _():
        xb_ref[...] = x_ref[...].astype(BF16)

    o_ref[...] = jnp.dot(xb_ref[...], w_ref[...], preferred_element_type=F32).astype(o_ref.dtype)


def _proj(x, w, out_dtype, tn, n):
    k = x.shape[1]
    ncol = w.shape[1]
    tm = min(PROJ_MAIN_TM, n)
    return pl.pallas_call(
        _proj_kernel,
        grid=(n // tm, ncol // tn),
        in_specs=[pl.BlockSpec((tm, k), lambda i, j: (i, 0)),
                  pl.BlockSpec((k, tn), lambda i, j: (0, j))],
        out_specs=pl.BlockSpec((tm, tn), lambda i, j: (i, j)),
        out_shape=jax.ShapeDtypeStruct((n, ncol), out_dtype),
        scratch_shapes=[pltpu.VMEM((tm, k), BF16)],
        compiler_params=_cp("parallel", "arbitrary"),
        name="proj",
    )(x, w)


def _proj_dil_kernel(x_ref, w_ref, cos_ref, sin_ref, o_ref, *slab_refs, dil, tm):
    acc_ref = slab_refs[0] if dil > 1 else None
    acc = jnp.dot(x_ref[...].astype(BF16), w_ref[...], preferred_element_type=F32)
    slabs = DIL_GROUP_W // LANES

    def emit(which, slab, val):
        if dil == 1:
            o_ref[which, 0, 0, :, slab * LANES:(slab + 1) * LANES] = val.astype(BF16)
        else:
            acc_ref[which * slabs + slab] = val

    def col(which, slab):
        start = which * DIL_GROUP_W + slab * LANES
        return acc[:, start:start + LANES]

    for which in range(2):
        scale = DIL_HD ** -0.5 if which == 0 else 1.0
        cos_t = cos_ref[...] * scale
        sin_t = sin_ref[...] * scale
        for quad in range(slabs // 2):
            x1, x2 = col(which, 2 * quad), col(which, 2 * quad + 1)
            emit(which, 2 * quad, x1 * cos_t - x2 * sin_t)
            emit(which, 2 * quad + 1, x2 * cos_t + x1 * sin_t)
    for slab in range(slabs):
        emit(2, slab, col(2, slab))

    if dil > 1:
        rows = tm // dil
        step = 4 if dil > 4 else dil
        outer = dil // step
        part = tm // step
        for s in range(3 * slabs):
            which, slab = divmod(s, slabs)
            if outer > 1:
                tmp_ref = slab_refs[1]
                for lo in range(step):
                    tmp_ref[s, lo * part:(lo + 1) * part, :] = acc_ref[s, pl.ds(lo, part, stride=step), :]
            for r in range(dil):
                hi, lo = divmod(r, step)
                if outer > 1:
                    val = tmp_ref[s, pl.ds(lo * part + hi, rows, stride=outer), :]
                else:
                    val = acc_ref[s, pl.ds(r, rows, stride=dil), :]
                o_ref[which, 0, r, :, slab * LANES:(slab + 1) * LANES] = val.astype(BF16)


def _proj_dil(x, w, cos_t, sin_t, batch, seq, dil):
    n, k = batch * seq, x.shape[1]
    tm = min(PROJ_TM, seq)
    tiles_per_seq = seq // tm
    sub = seq // dil
    kern = functools.partial(_proj_dil_kernel, dil=dil, tm=tm)
    n_slabs = 3 * DIL_GROUP_W // LANES
    return pl.pallas_call(
        kern,
        grid=(n // tm,),
        in_specs=[pl.BlockSpec((tm, k), lambda i: (i, 0)),
                  pl.BlockSpec((k, 3 * DIL_GROUP_W), lambda i: (0, 0)),
                  pl.BlockSpec((tm, LANES), lambda i: (i, 0)),
                  pl.BlockSpec((tm, LANES), lambda i: (i, 0))],
        out_specs=pl.BlockSpec((3, 1, dil, tm // dil, DIL_GROUP_W),
                               lambda i: (0, i // tiles_per_seq, 0, i % tiles_per_seq, 0)),
        out_shape=jax.ShapeDtypeStruct((3, batch, dil, sub, DIL_GROUP_W), BF16),
        scratch_shapes=[pltpu.VMEM((n_slabs, tm, LANES), F32)] * ((dil > 1) + (dil > 4)),
        compiler_params=_cp("parallel"),
        name=f"proj_dil{dil}",
    )(x, w, cos_t, sin_t)


def _split_dot(mat_bf16, val_f32):
    hi = val_f32.astype(BF16)
    lo = (val_f32 - hi.astype(F32)).astype(BF16)
    return (jnp.dot(mat_bf16, hi, preferred_element_type=F32)
            + jnp.dot(mat_bf16, lo, preferred_element_type=F32))


def _gla_kernel(q_ref, k_ref, v_ref, r_ref, x_ref, wa_ref, wup_ref, balpha_ref, gnorm_ref, tri_ref,
                o_ref, state_ref, sprev_ref):
    t = pl.program_id(1)
    n_chunks = GLA_T // GLA_CHUNK

    @pl.when(t == 0)
    def _():
        state_ref[...] = jnp.zeros_like(state_ref)

    a = jnp.dot(x_ref[0].astype(BF16), wa_ref[...], preferred_element_type=F32)
    z = jnp.dot(a.astype(BF16), wup_ref[...], preferred_element_type=F32) + balpha_ref[...]
    log_a = (jnp.minimum(z, 0.0) - jnp.log1p(jnp.exp(-jnp.abs(z)))) / GLA_TAU
    tri = tri_ref[...]
    b = jnp.concatenate([_split_dot(tri, log_a[blk * GLA_BLOCK:(blk + 1) * GLA_BLOCK])
                         for blk in range(GLA_T // GLA_BLOCK)], axis=0)
    b_end = jnp.concatenate(
        [jnp.broadcast_to(b[(n + 1) * GLA_CHUNK - 1:(n + 1) * GLA_CHUNK], (GLA_CHUNK, GLA_QK_W))
         for n in range(n_chunks)], axis=0)
    c = b_end - b
    eb = jnp.exp(b)
    kf = k_ref[0].astype(F32)
    qt = (q_ref[0].astype(F32) * eb * (GLA_DK ** -0.5)).astype(BF16)
    kt = (kf * jnp.exp(-b)).astype(BF16)
    ks = (kf * jnp.exp(c)).astype(BF16)
    causal = tri > 0

    for h in range(GLA_HEADS):
        ck = slice(h * GLA_DK, (h + 1) * GLA_DK)
        cv = slice(h * GLA_DV, (h + 1) * GLA_DV)
        qt_h, kt_h, ks_h = qt[:, ck], kt[:, ck], ks[:, ck]
        v_h = v_ref[0, :, cv]
        intra = []
        for blk in range(GLA_T // GLA_BLOCK):
            rb = slice(blk * GLA_BLOCK, (blk + 1) * GLA_BLOCK)
            att = lax.dot_general(qt_h[rb], kt_h[rb], (((1,), (1,)), ((), ())), preferred_element_type=F32)
            att = jnp.where(causal, att, 0.0).astype(BF16)
            intra.append(jnp.dot(att, v_h[rb], preferred_element_type=F32))
        o_h = jnp.concatenate(intra, axis=0)

        state = state_ref[h]
        for n in range(n_chunks):
            rs = slice(n * GLA_CHUNK, (n + 1) * GLA_CHUNK)
            contrib = lax.dot_general(v_h[rs], ks_h[rs], (((0,), (0,)), ((), ())),
                                      preferred_element_type=F32)
            sprev_ref[h, n] = state.astype(BF16)
            decay = eb[(n + 1) * GLA_CHUNK - 1:(n + 1) * GLA_CHUNK, ck]
            state = state * decay + contrib
        state_ref[h] = state

        inter = [lax.dot_general(qt_h[n * GLA_CHUNK:(n + 1) * GLA_CHUNK], sprev_ref[h, n],
                                 (((1,), (1,)), ((), ())), preferred_element_type=F32)
                 for n in range(n_chunks)]
        o_h = o_h + jnp.concatenate(inter, axis=0)

        o_h = o_h * lax.rsqrt(jnp.mean(jnp.square(o_h), axis=-1, keepdims=True) + RMS_EPS)
        o_h = o_h * gnorm_ref[:, cv]
        r_h = r_ref[0, :, cv].astype(F32)
        o_ref[0, :, cv] = (o_h * (r_h * jax.nn.sigmoid(r_h))).astype(o_ref.dtype)


def _gla(h_main, x, w_a, wup, balpha, gnorm, tri, batch, seq):
    h3 = h_main.reshape(batch, seq, MAIN_W)
    x3 = x.reshape(batch, seq, D_MODEL)
    n_chunks = GLA_T // GLA_CHUNK
    return pl.pallas_call(
        _gla_kernel,
        grid=(batch, seq // GLA_T),
        in_specs=[pl.BlockSpec((1, GLA_T, GLA_QK_W), lambda b, t: (b, t, 0)),
                  pl.BlockSpec((1, GLA_T, GLA_QK_W), lambda b, t: (b, t, 1)),
                  pl.BlockSpec((1, GLA_T, GLA_V_W), lambda b, t: (b, t, 1)),
                  pl.BlockSpec((1, GLA_T, GLA_V_W), lambda b, t: (b, t, 2)),
                  pl.BlockSpec((1, GLA_T, D_MODEL), lambda b, t: (b, t, 0)),
                  pl.BlockSpec((D_MODEL, LANES), lambda b, t: (0, 0)),
                  pl.BlockSpec((LANES, GLA_QK_W), lambda b, t: (0, 0)),
                  pl.BlockSpec((1, GLA_QK_W), lambda b, t: (0, 0)),
                  pl.BlockSpec((1, GLA_V_W), lambda b, t: (0, 0)),
                  pl.BlockSpec((GLA_BLOCK, GLA_BLOCK), lambda b, t: (0, 0))],
        out_specs=pl.BlockSpec((1, GLA_T, GLA_V_W), lambda b, t: (b, t, 0)),
        out_shape=jax.ShapeDtypeStruct((batch, seq, GLA_V_W), BF16),
        scratch_shapes=[pltpu.VMEM((GLA_HEADS, GLA_DV, GLA_DK), F32),
                        pltpu.VMEM((GLA_HEADS, n_chunks, GLA_DV, GLA_DK), BF16)],
        compiler_params=_cp("parallel", "arbitrary"),
        name="gla",
    )(h3, h3, h3, h3, x3, w_a, wup, balpha, gnorm, tri)


def _gla_masks():
    i = jnp.arange(GLA_BLOCK)[:, None]
    j = jnp.arange(GLA_BLOCK)[None, :]
    same = (i // GLA_CHUNK) == (j // GLA_CHUNK)
    return (same & (j <= i)).astype(BF16)


def _dil_attn_kernel(q_ref, kc_ref, kp_ref, vc_ref, vp_ref, o_ref, lse_ref, *, tq):
    n = pl.program_id(2)
    st = DIL_STEPS
    qi = lax.broadcasted_iota(I32, (st, 2 * st), 0)
    kj = lax.broadcasted_iota(I32, (st, 2 * st), 1)
    in_prev = kj < st
    band_bias = jnp.where(jnp.where(in_prev, kj - qi, qi + st - kj) >= 0, 0.0, NEG_INF)
    prev_cols = jnp.where(in_prev, 1.0, 0.0)
    lane = lax.broadcasted_iota(I32, (st, LANES), 1)
    low = lane < DIL_HD
    ones = jnp.ones((2 * st, LANES), BF16)
    nt = (((1,), (1,)), ((), ()))

    no_prev = jnp.where(n == 0, NEG_INF, 0.0)
    first_bias = band_bias + prev_cols * no_prev

    def window(cur_ref, prev_ref, i, cs):
        if i == 0:
            return jnp.concatenate([prev_ref[0, 0, 0, :, cs], cur_ref[0, 0, 0, 0:st, cs]], axis=0)
        return cur_ref[0, 0, 0, (i - 1) * st:(i + 1) * st, cs]

    quad_w = 2 * LANES
    quad_lane = lax.broadcasted_iota(I32, (st, quad_w), 1)
    head_in_quad = (quad_lane % LANES) // (DIL_HD // 2)

    for i in range(tq // st):
        r0 = i * st
        bias = first_bias if i == 0 else band_bias
        bias4 = jnp.concatenate([bias] * 4, axis=0)
        stats = jnp.zeros((st, LANES), F32)
        outs = []
        for quad in range(DIL_HEADS // 4):
            qs = slice(quad * quad_w, (quad + 1) * quad_w)
            qq = q_ref[0, 0, 0, r0:r0 + st, qs]
            zero = jnp.zeros_like(qq)
            lhs = jnp.concatenate([jnp.where(head_in_quad == u, qq, zero) for u in range(4)], axis=0)
            s = lax.dot_general(lhs, window(kc_ref, kp_ref, i, qs), nt,
                                preferred_element_type=F32) + bias4
            m = jnp.max(s, axis=-1, keepdims=True)
            pexp = jnp.exp(s - m).astype(BF16)
            m_rep = jnp.broadcast_to(m, (4 * st, LANES))
            for pair in range(2):
                p = 2 * quad + pair
                cs = slice(p * LANES, (p + 1) * LANES)
                v_ext = jnp.concatenate([window(vc_ref, vp_ref, i, cs), ones], axis=1)
                o = jnp.dot(pexp[2 * pair * st:(2 * pair + 2) * st], v_ext,
                            preferred_element_type=F32)
                outs.append(jnp.where(low, o[0:st, 0:LANES], o[st:2 * st, 0:LANES]))
                l_rep = o[:, LANES:2 * LANES]
                for half in range(2):
                    rs = slice(half * st, (half + 1) * st)
                    ms = slice((2 * pair + half) * st, (2 * pair + half + 1) * st)
                    stats = jnp.where(lane == 2 * p + half, m_rep[ms], stats)
                    stats = jnp.where(lane == DIL_HEADS + 2 * p + half, l_rep[rs], stats)
        o_ref[0, 0, r0:r0 + st, :] = jnp.concatenate(outs, axis=1).astype(o_ref.dtype)
        lse_ref[0, 0, r0:r0 + st, :] = stats


def _dil_attn(qkv, batch, seq, dil):
    sub = seq // dil
    tq = min(ATTN_TQ, sub)
    per = tq // DIL_STEPS

    def cur(which):
        return pl.BlockSpec((1, 1, 1, tq, DIL_GROUP_W), lambda b, r, n: (which, b, r, n, 0))

    def prev(which):
        return pl.BlockSpec((1, 1, 1, DIL_STEPS, DIL_GROUP_W),
                            lambda b, r, n: (which, b, r, jnp.maximum(n * per - 1, 0), 0))

    return pl.pallas_call(
        functools.partial(_dil_attn_kernel, tq=tq),
        grid=(batch, dil, sub // tq),
        in_specs=[cur(0), cur(1), prev(1), cur(2), prev(2)],
        out_specs=[pl.BlockSpec((1, 1, tq, DIL_GROUP_W), lambda b, r, n: (b, r, n, 0)),
                   pl.BlockSpec((1, 1, tq, LANES), lambda b, r, n: (b, r, n, 0))],
        out_shape=[jax.ShapeDtypeStruct((batch, dil, sub, DIL_GROUP_W), BF16),
                   jax.ShapeDtypeStruct((batch, dil, sub, LANES), F32)],
        compiler_params=_cp("parallel", "parallel", "arbitrary"),
        name=f"dil_attn{dil}",
    )(qkv, qkv, qkv, qkv, qkv)


def _store_token_tiles(ref, val):
    rows = val.shape[0]
    for c in range(TOKEN_ROWS):
        ref[pl.ds(c, rows, stride=TOKEN_ROWS), :] = val[:, c * LANES:(c + 1) * LANES]


def _load_token_tiles(ref, rows):
    return jnp.concatenate([ref[pl.ds(c, rows, stride=TOKEN_ROWS), :] for c in range(TOKEN_ROWS)], axis=1)


def _layer_norm(x, g, b):
    mu = jnp.mean(x, axis=-1, keepdims=True)
    var = jnp.mean(jnp.square(x - mu), axis=-1, keepdims=True)
    return (x - mu) * lax.rsqrt(var + LN_EPS) * g + b


def _mix_kernel(og_ref, g1_ref, g2_ref, o0_ref, o1_ref, o2_ref, l0_ref, l1_ref, l2_ref, x_ref,
                wgo_ref, wao_ref, wout_ref, lng_ref, lnb_ref, wrt_ref, br_ref, expand_ref,
                x1_ref, bucket_ref, osc_ref, lsc_ref):
    tm = x_ref.shape[0]

    def interleaved(o_ref, l_ref, dil, slot):
        if dil == 1:
            return o_ref[0, 0].astype(F32), l_ref[0, 0]
        rows = tm // dil
        chunks = DIL_GROUP_W // LANES
        for r in range(dil):
            for c in range(chunks):
                osc_ref[slot, c, pl.ds(r, rows, stride=dil), :] = (
                    o_ref[0, r, :, c * LANES:(c + 1) * LANES].astype(F32))
            lsc_ref[slot, pl.ds(r, rows, stride=dil), :] = l_ref[0, r]
        return jnp.concatenate([osc_ref[slot, c] for c in range(chunks)], axis=1), lsc_ref[slot]

    groups = [interleaved(o_ref, l_ref, dil, slot)
              for slot, (o_ref, l_ref, (_, dil)) in enumerate(
                  zip((o0_ref, o1_ref, o2_ref), (l0_ref, l1_ref, l2_ref), DIL_PATTERN))]
    stats = [g[1] for g in groups]
    m_max = jnp.maximum(jnp.maximum(stats[0], stats[1]), stats[2])
    es = [jnp.exp(s - m_max) for s in stats]
    ls = [pltpu.roll(s, LANES - DIL_HEADS, 1) for s in stats]
    inv = 1.0 / (es[0] * ls[0] + es[1] * ls[1] + es[2] * ls[2])
    head_lane = lax.broadcasted_iota(I32, (tm, LANES), 1) < DIL_HEADS
    att = jnp.zeros((tm, DIL_GROUP_W), F32)
    for (num_g, _), e in zip(groups, es):
        coef = jnp.where(head_lane, e * inv, 0.0)
        att = att + _split_dot_rhs(coef, expand_ref[...]) * num_g

    y_att = jnp.dot(att.astype(BF16), wao_ref[...], preferred_element_type=F32)
    y_gla = jnp.dot(og_ref[...], wgo_ref[...], preferred_element_type=F32)
    mixed = (jax.nn.sigmoid(g1_ref[...].astype(F32)) * y_gla
             + jax.nn.sigmoid(g2_ref[...].astype(F32)) * y_att)
    mix = jnp.dot(mixed.astype(BF16), wout_ref[...], preferred_element_type=F32)
    x1 = _layer_norm(DN_ALPHA * x_ref[...] + mix, lng_ref[...], lnb_ref[...])
    _store_token_tiles(x1_ref, x1)

    logits = lax.dot_general(wrt_ref[...], x1.astype(BF16), (((1,), (1,)), ((), ())),
                             preferred_element_type=F32) + br_ref[:, 0:1]
    e = jnp.exp(logits - jnp.max(logits, axis=0, keepdims=True))
    probs = e / jnp.sum(e, axis=0, keepdims=True)
    best = None
    for g in range(N_GROUPS):
        for pi, (ea, eb_) in enumerate(PAIRS):
            cand = probs[4 * g + ea:4 * g + ea + 1] + probs[4 * g + eb_:4 * g + eb_ + 1]
            idx = g * len(PAIRS) + pi
            if best is None:
                best, bidx = cand, jnp.zeros(cand.shape, I32)
            else:
                better = cand > best
                best = jnp.where(better, cand, best)
                bidx = jnp.where(better, idx, bidx)
    bucket_ref[0] = bidx


def _split_dot_rhs(val_f32, mat_bf16):
    hi = val_f32.astype(BF16)
    lo = (val_f32 - hi.astype(F32)).astype(BF16)
    return (jnp.dot(hi, mat_bf16, preferred_element_type=F32)
            + jnp.dot(lo, mat_bf16, preferred_element_type=F32))


def _mix(og, h_main, attn, x, wgo, wao, wout, lng, lnb, wrt, br, expand, batch, seq):
    n = batch * seq
    tm = min(MIX_TM, seq)
    tps = seq // tm
    row = lambda i: (i, 0)
    const = lambda i: (0, 0)
    o_specs, l_specs, o_args, l_args = [], [], [], []
    for (o_g, lse_g), (_, dil) in zip(attn, DIL_PATTERN):
        o_specs.append(pl.BlockSpec((1, dil, tm // dil, DIL_GROUP_W), lambda i: (i // tps, 0, i % tps, 0)))
        l_specs.append(pl.BlockSpec((1, dil, tm // dil, LANES), lambda i: (i // tps, 0, i % tps, 0)))
        o_args.append(o_g)
        l_args.append(lse_g)
    gate_blk = GATE_COL // D_MODEL
    return pl.pallas_call(
        _mix_kernel,
        grid=(n // tm,),
        in_specs=[pl.BlockSpec((tm, GLA_V_W), row),
                  pl.BlockSpec((tm, D_MODEL), lambda i: (i, gate_blk)),
                  pl.BlockSpec((tm, D_MODEL), lambda i: (i, gate_blk + 1)),
                  *o_specs, *l_specs,
                  pl.BlockSpec((tm, D_MODEL), row),
                  pl.BlockSpec((GLA_V_W, D_MODEL), const),
                  pl.BlockSpec((DIL_GROUP_W, D_MODEL), const),
                  pl.BlockSpec((D_MODEL, D_MODEL), const),
                  pl.BlockSpec((1, D_MODEL), const),
                  pl.BlockSpec((1, D_MODEL), const),
                  pl.BlockSpec((N_EXPERTS, D_MODEL), const),
                  pl.BlockSpec((N_EXPERTS, LANES), const),
                  pl.BlockSpec((LANES, DIL_GROUP_W), const)],
        out_specs=[pl.BlockSpec((tm * TOKEN_ROWS, LANES), row),
                   pl.BlockSpec((1, 1, tm), lambda i: (i, 0, 0))],
        out_shape=[jax.ShapeDtypeStruct((n * TOKEN_ROWS, LANES), F32),
                   jax.ShapeDtypeStruct((n // tm, 1, tm), I32)],
        scratch_shapes=[pltpu.VMEM((N_DIL, DIL_GROUP_W // LANES, tm, LANES), F32),
                        pltpu.VMEM((N_DIL, tm, LANES), F32)],
        compiler_params=_cp("parallel"),
        name="mix",
    )(og, h_main, h_main, *o_args, *l_args, x, wgo, wao, wout, lng, lnb, wrt, br, expand)


def _rank_kernel(bucket_ref, upper_ref, rank_ref, counts_ref, carry_ref):
    @pl.when(pl.program_id(0) == 0)
    def _():
        carry_ref[...] = jnp.zeros_like(carry_ref)

    b = bucket_ref[0]
    ts = b.shape[1]
    onehot = lax.broadcasted_iota(I32, (BUCKET_ROWS, ts), 0) == b
    oh_f = jnp.where(onehot, 1.0, 0.0)
    before = jnp.dot(oh_f.astype(BF16), upper_ref[...], preferred_element_type=F32)
    carry = carry_ref[...]
    rank = jnp.sum(oh_f * (before + carry[:, 0:1]), axis=0, keepdims=True)
    rank_ref[0] = rank.astype(I32)
    carry = carry + jnp.sum(oh_f, axis=1, keepdims=True)
    carry_ref[...] = carry
    counts_ref[...] = carry.astype(I32)


def _bucket_ranks(bucket, n):
    ts = min(SORT_TS, n)
    b3 = bucket.reshape(n // ts, 1, ts)
    j = jnp.arange(ts)
    upper = (j[:, None] < j[None, :]).astype(BF16)
    rank, counts = pl.pallas_call(
        _rank_kernel,
        grid=(n // ts,),
        in_specs=[pl.BlockSpec((1, 1, ts), lambda i: (i, 0, 0)),
                  pl.BlockSpec((ts, ts), lambda i: (0, 0))],
        out_specs=[pl.BlockSpec((1, 1, ts), lambda i: (i, 0, 0)),
                   pl.BlockSpec((BUCKET_ROWS, LANES), lambda i: (0, 0))],
        out_shape=[jax.ShapeDtypeStruct((n // ts, 1, ts), I32),
                   jax.ShapeDtypeStruct((BUCKET_ROWS, LANES), I32)],
        scratch_shapes=[pltpu.VMEM((BUCKET_ROWS, LANES), F32)],
        compiler_params=_cp("arbitrary"),
        name="bucket_rank",
    )(b3, upper)
    return rank.reshape(n), counts[:N_BUCKETS, 0]


def _sort_plan(bucket, rank, counts, n):
    tiles = (counts + MOE_T - 1) // MOE_T
    tile_end = jnp.cumsum(tiles)
    offsets = (tile_end - tiles) * MOE_T
    dest = offsets[bucket] + rank
    n_tiles = n // MOE_T + N_BUCKETS
    tile_ids = jnp.arange(n_tiles, dtype=I32)
    total = tile_end[-1]
    last = jnp.minimum(tile_ids, total - 1)
    tb = jnp.sum((last[:, None] >= tile_end[None, :]).astype(I32), axis=1)
    tb = jnp.minimum(tb, N_BUCKETS - 1)
    n_valid = jnp.clip(counts[tb] - (tile_ids - (tile_end - tiles)[tb]) * MOE_T, 0, MOE_T)
    n_valid = jnp.where(tile_ids < total, n_valid, 0)
    pair = jnp.array(PAIRS, dtype=I32)
    e_lo = (tb // len(PAIRS)) * EXPERTS_PER_GROUP + pair[tb % len(PAIRS), 0]
    e_hi = (tb // len(PAIRS)) * EXPERTS_PER_GROUP + pair[tb % len(PAIRS), 1]
    pad_lo = jnp.concatenate([offsets + counts, total[None] * MOE_T])
    pad_hi = jnp.concatenate([tile_end * MOE_T, jnp.full((1,), n_tiles * MOE_T, I32)])
    pad_ranges = jnp.stack([pad_lo, pad_hi]).astype(I32)
    return dest.astype(I32), tb, e_lo, e_hi, n_valid.astype(I32), pad_ranges


def _invert_kernel(pad_ref, dest_ref, src_ref, table_ref, sem, *, chunk):
    i = pl.program_id(0)

    @pl.when(i == 0)
    def _():
        def clear(j, carry):
            table_ref[j] = 0
            return carry
        for b in range(N_BUCKETS + 1):
            lax.fori_loop(pad_ref[0, b], pad_ref[1, b], clear, 0)

    def put(j, carry):
        table_ref[dest_ref[0, 0, j]] = i * chunk + j
        return carry
    lax.fori_loop(0, chunk, put, 0, unroll=32)

    @pl.when(i == pl.num_programs(0) - 1)
    def _():
        copy = pltpu.make_async_copy(table_ref, src_ref, sem)
        copy.start()
        copy.wait()


def _invert(dest, pad_ranges, n_slots):
    n = dest.shape[0]
    chunk = min(2048, n)
    return pl.pallas_call(
        functools.partial(_invert_kernel, chunk=chunk),
        grid=(n // chunk,),
        in_specs=[pl.BlockSpec(memory_space=pltpu.SMEM),
                  pl.BlockSpec((1, 1, chunk), lambda i: (i, 0, 0), memory_space=pltpu.SMEM)],
        out_specs=pl.BlockSpec(memory_space=pl.ANY),
        out_shape=jax.ShapeDtypeStruct((n_slots,), I32),
        scratch_shapes=[pltpu.SMEM((n_slots,), I32), pltpu.SemaphoreType.DMA(())],
        compiler_params=_cp("arbitrary"),
        name="invert_perm",
    )(pad_ranges, dest.reshape(n // chunk, 1, chunk))


def _moe_kernel(tb_ref, elo_ref, ehi_ref, nvalid_ref, prv_ref, src_ref, nxt_ref, x_hbm, wr2_ref, br2_ref,
                wg_lo_ref, wu_lo_ref, wd_lo_ref, wg_hi_ref, wu_hi_ref, wd_hi_ref, lng_ref, lnb_ref,
                out_hbm, xbuf_ref, ybuf_ref, gsem, ssem, *, n_tokens):
    del tb_ref, elo_ref, ehi_ref
    t = pl.program_id(0)
    last = pl.num_programs(0) - 1
    slot = t % 2
    other = 1 - slot
    tile = MOE_T
    n_chunks = 8
    per_chunk = tile // n_chunks

    def token_rows(tok):
        if isinstance(tok, int):
            return pl.ds(tok * TOKEN_ROWS, TOKEN_ROWS)
        return pl.ds(pl.multiple_of(tok * TOKEN_ROWS, TOKEN_ROWS), TOKEN_ROWS)

    def gather_copy(idx_ref, r, sl):
        return pltpu.make_async_copy(x_hbm.at[token_rows(idx_ref[0, 0, r])],
                                     xbuf_ref.at[sl, token_rows(r)], gsem.at[sl])

    def scatter_copy(dst, r, sl):
        return pltpu.make_async_copy(ybuf_ref.at[sl, token_rows(r)], out_hbm.at[token_rows(dst)], ssem.at[sl])

    def scatter_rows(idx_ref, n_valid, rows, sl):
        for r in rows:
            dst = jnp.where(r < n_valid, idx_ref[0, 0, r], n_tokens + sl * tile + r)
            scatter_copy(dst, r, sl).start(priority=r % 2)

    def wait_scatter(sl):
        for r in range(tile):
            scatter_copy(0, r, sl).wait()

    prev_valid = jnp.where(t > 0, nvalid_ref[jnp.maximum(t - 1, 0)], 0)

    def issue(chunk):
        rows = range(chunk * per_chunk, (chunk + 1) * per_chunk)
        for r in rows:
            gather_copy(nxt_ref, r, other).start(priority=r % 2)
        scatter_rows(prv_ref, prev_valid, rows, other)

    @pl.when(t == 0)
    def _():
        ybuf_ref[...] = jnp.zeros_like(ybuf_ref)
        for r in range(tile):
            gather_copy(src_ref, r, 0).start()
        scatter_rows(src_ref, 0, range(tile), 0)

    for r in range(tile):
        gather_copy(src_ref, r, slot).wait()

    @pl.when(nvalid_ref[t] == 0)
    def _():
        wait_scatter(slot)
        for chunk in range(n_chunks):
            issue(chunk)

    @pl.when(nvalid_ref[t] != 0)
    def _():
        x = _load_token_tiles(xbuf_ref.at[slot], tile)
        xb = x.astype(BF16)
        issue(0)
        issue(1)
        l2 = jnp.dot(xb, wr2_ref[0], preferred_element_type=F32) + br2_ref[0]
        g_lo = jax.nn.sigmoid(l2[:, 0:1] - l2[:, 1:2])
        g_hi = jax.nn.sigmoid(l2[:, 1:2] - l2[:, 0:1])
        issue(2)
        issue(3)

        def expert(wg_ref, wu_ref, wd_ref, chunks):
            a = jnp.dot(xb, wg_ref[0], preferred_element_type=F32)
            for chunk in chunks[:2]:
                issue(chunk)
            u = jnp.dot(xb, wu_ref[0], preferred_element_type=F32)
            for chunk in chunks[2:]:
                issue(chunk)
            hidden = (a * jax.nn.sigmoid(a) * u).astype(BF16)
            return jnp.dot(hidden, wd_ref[0], preferred_element_type=F32)

        y = g_lo * expert(wg_lo_ref, wu_lo_ref, wd_lo_ref, (4, 5, 6, 7))
        y = y + g_hi * expert(wg_hi_ref, wu_hi_ref, wd_hi_ref, ())
        y = _layer_norm(DN_ALPHA * x + y, lng_ref[...], lnb_ref[...])
        wait_scatter(slot)
        _store_token_tiles(ybuf_ref.at[slot], y)

    @pl.when(t == last)
    def _():
        for r in range(tile):
            gather_copy(nxt_ref, r, other).wait()
        scatter_rows(src_ref, nvalid_ref[t], range(tile), slot)
        for sl in range(2):
            wait_scatter(sl)


def _moe(x1, src, plan, wr2, br2, wg, wu, wd, lng, lnb):
    _, tb, e_lo, e_hi, n_valid, _ = plan
    n = x1.shape[0] // TOKEN_ROWS
    n_tiles = src.shape[0] // MOE_T
    src3 = src.reshape(n_tiles, 1, MOE_T)
    lo = lambda t, tb, elo, ehi, v: (elo[t], 0, 0)
    hi = lambda t, tb, elo, ehi, v: (ehi[t], 0, 0)
    const = lambda t, tb, elo, ehi, v: (0, 0)
    wblk = (1, D_MODEL, D_FF)
    idx_blk = (1, 1, MOE_T)
    grid_spec = pltpu.PrefetchScalarGridSpec(
        num_scalar_prefetch=4,
        grid=(n_tiles,),
        in_specs=[pl.BlockSpec(idx_blk, lambda t, tb, elo, ehi, v: (jnp.maximum(t - 1, 0), 0, 0),
                               memory_space=pltpu.SMEM),
                  pl.BlockSpec(idx_blk, lambda t, tb, elo, ehi, v: (t, 0, 0), memory_space=pltpu.SMEM),
                  pl.BlockSpec(idx_blk, lambda t, tb, elo, ehi, v: (jnp.minimum(t + 1, n_tiles - 1), 0, 0),
                               memory_space=pltpu.SMEM),
                  pl.BlockSpec(memory_space=pl.ANY),
                  pl.BlockSpec((1, D_MODEL, LANES), lambda t, tb, elo, ehi, v: (tb[t], 0, 0)),
                  pl.BlockSpec((1, 1, LANES), lambda t, tb, elo, ehi, v: (tb[t], 0, 0)),
                  pl.BlockSpec(wblk, lo), pl.BlockSpec(wblk, lo), pl.BlockSpec((1, D_FF, D_MODEL), lo),
                  pl.BlockSpec(wblk, hi), pl.BlockSpec(wblk, hi), pl.BlockSpec((1, D_FF, D_MODEL), hi),
                  pl.BlockSpec((1, D_MODEL), const), pl.BlockSpec((1, D_MODEL), const)],
        out_specs=pl.BlockSpec(memory_space=pl.ANY),
        scratch_shapes=[pltpu.VMEM((2, MOE_T * TOKEN_ROWS, LANES), F32),
                        pltpu.VMEM((2, MOE_T * TOKEN_ROWS, LANES), F32),
                        pltpu.SemaphoreType.DMA((2,)), pltpu.SemaphoreType.DMA((2,))],
    )
    return pl.pallas_call(
        functools.partial(_moe_kernel, n_tokens=n),
        grid_spec=grid_spec,
        out_shape=jax.ShapeDtypeStruct(((n + 2 * MOE_T) * TOKEN_ROWS, LANES), F32),
        compiler_params=_cp("arbitrary"),
        name="moe",
    )(tb, e_lo, e_hi, n_valid, src3, src3, src3, x1, wr2, br2, wg, wu, wd, wg, wu, wd, lng, lnb)


def _untile_kernel(xt_ref, x_ref, xb_ref):
    x = _load_token_tiles(xt_ref, x_ref.shape[0])
    x_ref[...] = x
    xb_ref[...] = x.astype(BF16)


def _untile(xt, n):
    tm = min(PROJ_TM, n)
    return pl.pallas_call(
        _untile_kernel,
        grid=(n // tm,),
        in_specs=[pl.BlockSpec((tm * TOKEN_ROWS, LANES), lambda i: (i, 0))],
        out_specs=[pl.BlockSpec((tm, D_MODEL), lambda i: (i, 0))] * 2,
        out_shape=[jax.ShapeDtypeStruct((n, D_MODEL), F32), jax.ShapeDtypeStruct((n, D_MODEL), BF16)],
        compiler_params=_cp("parallel"),
        name="untile",
    )(xt)


def _split_w_in(w_in_l):
    sizes = (GLA_QK_W, GLA_QK_W, GLA_V_W, GLA_V_W, GLA_RANK,
             N_DIL * DIL_GROUP_W, N_DIL * DIL_GROUP_W, N_DIL * DIL_GROUP_W, 2 * D_MODEL)
    starts = [sum(sizes[:i]) for i in range(len(sizes))]
    blk = [w_in_l[:, s:s + z] for s, z in zip(starts, sizes)]
    q_g, k_g, v_g, r_g, a_g, q_d, k_d, v_d, gates = blk
    w_main = jnp.concatenate([q_g, k_g, v_g, r_g, gates], axis=1).astype(BF16)
    w_a = jnp.pad(a_g, ((0, 0), (0, LANES - GLA_RANK))).astype(BF16)
    quad = jnp.asarray(_quad_layout())
    w_dil = []
    for g in range(N_DIL):
        gs = slice(g * DIL_GROUP_W, (g + 1) * DIL_GROUP_W)
        w_dil.append(jnp.concatenate([q_d[:, gs][:, quad], k_d[:, gs][:, quad], v_d[:, gs]],
                                     axis=1).astype(BF16))
    return w_main, w_a, w_dil


def _quad_layout():
    half = DIL_HD // 2
    cols = []
    for quad in range(DIL_HEADS // 4):
        for part in range(2):
            for u in range(4):
                start = (4 * quad + u) * DIL_HD + part * half
                cols.extend(range(start, start + half))
    return cols


def _router_pair_tables(w_router, b_router):
    cols_w, cols_b = [], []
    for g in range(N_GROUPS):
        for ea, eb_ in PAIRS:
            lo, hi = EXPERTS_PER_GROUP * g + ea, EXPERTS_PER_GROUP * g + eb_
            w2 = jnp.stack([w_router[:, lo], w_router[:, hi]], axis=1)
            cols_w.append(jnp.pad(w2, ((0, 0), (0, LANES - 2))))
            cols_b.append(jnp.pad(jnp.stack([b_router[lo], b_router[hi]]), (0, LANES - 2)))
    return jnp.stack(cols_w).astype(BF16), jnp.stack(cols_b).reshape(N_BUCKETS, 1, LANES).astype(F32)


def kernel(x, positions, w_in, w_alpha_up, b_alpha, gla_norm, w_gla_o, w_att_o, w_out, ln1_g, ln1_b,
           w_router, b_router, w_e_gate, w_e_up, w_e_down, ln2_g, ln2_b):
    batch, seq, d_model = x.shape
    depth = w_in.shape[0]
    n = batch * seq
    assert d_model == D_MODEL and seq % (DIL_PATTERN[-1][1] * DIL_STEPS) == 0

    cos_t, sin_t = _rope_tables(positions)
    tri = _gla_masks()
    head_of_col = jnp.arange(DIL_GROUP_W) // DIL_HD
    expand = (jnp.arange(LANES)[:, None] == head_of_col[None, :]).astype(BF16)
    wrt = w_router.T.astype(BF16)
    br = jnp.broadcast_to(b_router.astype(F32)[:, None], (N_EXPERTS, LANES))
    wr2, br2 = _router_pair_tables(w_router, b_router)
    n_slots = n + N_BUCKETS * MOE_T

    xf = xin = x.reshape(n, D_MODEL)
    for i in range(depth):
        w_main, w_a, w_dil = _split_w_in(w_in[i])
        h_main = _proj(xin, w_main, BF16, PROJ_TN, n)
        wup = jnp.pad(w_alpha_up[i], ((0, LANES - GLA_RANK), (0, 0))).astype(BF16)
        og = _gla(h_main, xin, w_a, wup, b_alpha[i].reshape(1, -1).astype(F32),
                  gla_norm[i].reshape(1, -1).astype(F32), tri, batch, seq)
        attn = []
        for g, (_, dil) in enumerate(DIL_PATTERN):
            qkv = _proj_dil(xin, w_dil[g], cos_t, sin_t, batch, seq, dil)
            attn.append(_dil_attn(qkv, batch, seq, dil))
        x1, bucket = _mix(og.reshape(n, GLA_V_W), h_main, attn, xf,
                          w_gla_o[i].astype(BF16), w_att_o[i].astype(BF16), w_out[i].astype(BF16),
                          ln1_g[i].reshape(1, -1), ln1_b[i].reshape(1, -1), wrt, br, expand, batch, seq)
        bucket = bucket.reshape(n)
        rank, counts = _bucket_ranks(bucket, n)
        plan = _sort_plan(bucket, rank, counts, n)
        src = _invert(plan[0], plan[5], n_slots)
        xt = _moe(x1, src, plan, wr2, br2, w_e_gate[i].astype(BF16), w_e_up[i].astype(BF16),
                  w_e_down[i].astype(BF16), ln2_g[i].reshape(1, -1), ln2_b[i].reshape(1, -1))
        xf, xin = _untile(xt, n)
    return xf.reshape(batch, seq, D_MODEL)
```

```python
import functools

import jax
import jax.numpy as jnp
from jax import lax
from jax.experimental import pallas as pl
from jax.experimental.pallas import tpu as pltpu

F32 = jnp.float32
BF16 = jnp.bfloat16
I32 = jnp.int32

D_MODEL = 1024
N_LAYERS = 4
GLA_HEADS = 4
GLA_DK = 128
GLA_DV = 256
GLA_RANK = 16
GLA_TAU = 16.0
GLA_CHUNK = 64
DIL_PATTERN = ((128, 1), (512, 4), (2048, 16))
DIL_HEADS = 8
DIL_HD = 64
DIL_STEPS = 128
ROPE_THETA = 10000.0
N_EXPERTS = 16
N_GROUPS = 4
EXPERTS_PER_GROUP = 4
D_FF = 1024
LN_EPS = 1e-5
RMS_EPS = 1e-6
DN_ALPHA = (2 * N_LAYERS) ** 0.25
NEG_INF = -1e30

GLA_QK_W = GLA_HEADS * GLA_DK
GLA_V_W = GLA_HEADS * GLA_DV
DIL_GROUP_W = DIL_HEADS * DIL_HD
N_DIL = len(DIL_PATTERN)
MAIN_W = 2 * GLA_QK_W + 2 * GLA_V_W + 2 * D_MODEL
GATE_COL = 2 * GLA_QK_W + 2 * GLA_V_W

LANES = 128
TOKEN_ROWS = D_MODEL // LANES
PAIRS = ((0, 1), (0, 2), (0, 3), (1, 2), (1, 3), (2, 3))
N_BUCKETS = N_GROUPS * len(PAIRS)
BUCKET_ROWS = 32

PROJ_TM = 1024
PROJ_MAIN_TM = 2048
PROJ_TN = 1024
GLA_T = 512
GLA_BLOCK = 2 * GLA_CHUNK
MIX_TM = 512
SORT_TS = 1024
MOE_T = 256
ATTN_TQ = 512
VMEM_LIMIT = 56 * 1024 * 1024


def _cp(*sem):
    return pltpu.CompilerParams(dimension_semantics=sem, vmem_limit_bytes=VMEM_LIMIT)


def _rope_kernel(pos_ref, invf_ref, cos_ref, sin_ref):
    ang = pos_ref[...].astype(F32) * invf_ref[...]
    cos_ref[...] = jnp.cos(ang)
    sin_ref[...] = jnp.sin(ang)


def _rope_tables(positions):
    n = positions.size
    tm = 2048
    half = DIL_HD // 2
    inv_freq = jnp.power(jnp.float32(ROPE_THETA), -jnp.arange(0, DIL_HD, 2, dtype=F32) / DIL_HD)
    invf = jnp.tile(inv_freq, LANES // half).reshape(1, LANES)
    return pl.pallas_call(
        _rope_kernel,
        grid=(n // tm,),
        in_specs=[pl.BlockSpec((tm, 1), lambda i: (i, 0)),
                  pl.BlockSpec((1, LANES), lambda i: (0, 0))],
        out_specs=[pl.BlockSpec((tm, LANES), lambda i: (i, 0)),
                   pl.BlockSpec((tm, LANES), lambda i: (i, 0))],
        out_shape=[jax.ShapeDtypeStruct((n, LANES), F32)] * 2,
        compiler_params=_cp("parallel"),
        name="rope_tables",
    )(positions.reshape(n, 1), invf)


def _proj_kernel(x_ref, w_ref, o_ref, xb_ref):
    @pl.when(pl.program_id(1) == 0)
    def _():
        xb_ref[...] = x_ref[...].astype(BF16)

    o_ref[...] = jnp.dot(xb_ref[...], w_ref[...], preferred_element_type=F32).astype(o_ref.dtype)


def _proj(x, w, out_dtype, tn, n):
    k = x.shape[1]
    ncol = w.shape[1]
    tm = min(PROJ_MAIN_TM, n)
    return pl.pallas_call(
        _proj_kernel,
        grid=(n // tm, ncol // tn),
        in_specs=[pl.BlockSpec((tm, k), lambda i, j: (i, 0)),
                  pl.BlockSpec((k, tn), lambda i, j: (0, j))],
        out_specs=pl.BlockSpec((tm, tn), lambda i, j: (i, j)),
        out_shape=jax.ShapeDtypeStruct((n, ncol), out_dtype),
        scratch_shapes=[pltpu.VMEM((tm, k), BF16)],
        compiler_params=_cp("parallel", "arbitrary"),
        name="proj",
    )(x, w)


def _proj_dil_kernel(x_ref, w_ref, cos_ref, sin_ref, o_ref, *slab_refs, dil, tm):
    acc_ref = slab_refs[0] if dil > 1 else None
    acc = jnp.dot(x_ref[...].astype(BF16), w_ref[...], preferred_element_type=F32)
    slabs = DIL_GROUP_W // LANES

    def emit(which, slab, val):
        if dil == 1:
            o_ref[which, 0, 0, :, slab * LANES:(slab + 1) * LANES] = val.astype(BF16)
        else:
            acc_ref[which * slabs + slab] = val

    def col(which, slab):
        start = which * DIL_GROUP_W + slab * LANES
        return acc[:, start:start + LANES]

    for which in range(2):
        scale = DIL_HD ** -0.5 if which == 0 else 1.0
        cos_t = cos_ref[...] * scale
        sin_t = sin_ref[...] * scale
        for quad in range(slabs // 2):
            x1, x2 = col(which, 2 * quad), col(which, 2 * quad + 1)
            emit(which, 2 * quad, x1 * cos_t - x2 * sin_t)
            emit(which, 2 * quad + 1, x2 * cos_t + x1 * sin_t)
    for slab in range(slabs):
        emit(2, slab, col(2, slab))

    if dil > 1:
        rows = tm // dil
        step = 4 if dil > 4 else dil
        outer = dil // step
        part = tm // step
        for s in range(3 * slabs):
            which, slab = divmod(s, slabs)
            if outer > 1:
                tmp_ref = slab_refs[1]
                for lo in range(step):
                    tmp_ref[s, lo * part:(lo + 1) * part, :] = acc_ref[s, pl.ds(lo, part, stride=step), :]
            for r in range(dil):
                hi, lo = divmod(r, step)
                if outer > 1:
                    val = tmp_ref[s, pl.ds(lo * part + hi, rows, stride=outer), :]
                else:
                    val = acc_ref[s, pl.ds(r, rows, stride=dil), :]
                o_ref[which, 0, r, :, slab * LANES:(slab + 1) * LANES] = val.astype(BF16)


def _proj_dil(x, w, cos_t, sin_t, batch, seq, dil):
    n, k = batch * seq, x.shape[1]
    tm = min(PROJ_TM, seq)
    tiles_per_seq = seq // tm
    sub = seq // dil
    kern = functools.partial(_proj_dil_kernel, dil=dil, tm=tm)
    n_slabs = 3 * DIL_GROUP_W // LANES
    return pl.pallas_call(
        kern,
        grid=(n // tm,),
        in_specs=[pl.BlockSpec((tm, k), lambda i: (i, 0)),
                  pl.BlockSpec((k, 3 * DIL_GROUP_W), lambda i: (0, 0)),
                  pl.BlockSpec((tm, LANES), lambda i: (i, 0)),
                  pl.BlockSpec((tm, LANES), lambda i: (i, 0))],
        out_specs=pl.BlockSpec((3, 1, dil, tm // dil, DIL_GROUP_W),
                               lambda i: (0, i // tiles_per_seq, 0, i % tiles_per_seq, 0)),
        out_shape=jax.ShapeDtypeStruct((3, batch, dil, sub, DIL_GROUP_W), BF16),
        scratch_shapes=[pltpu.VMEM((n_slabs, tm, LANES), F32)] * ((dil > 1) + (dil > 4)),
        compiler_params=_cp("parallel"),
        name=f"proj_dil{dil}",
    )(x, w, cos_t, sin_t)


def _split_dot(mat_bf16, val_f32):
    hi = val_f32.astype(BF16)
    lo = (val_f32 - hi.astype(F32)).astype(BF16)
    return (jnp.dot(mat_bf16, hi, preferred_element_type=F32)
            + jnp.dot(mat_bf16, lo, preferred_element_type=F32))


def _gla_kernel(q_ref, k_ref, v_ref, r_ref, x_ref, wa_ref, wup_ref, balpha_ref, gnorm_ref, tri_ref,
                o_ref, state_ref, sprev_ref):
    t = pl.program_id(1)
    n_chunks = GLA_T // GLA_CHUNK

    @pl.when(t == 0)
    def _():
        state_ref[...] = jnp.zeros_like(state_ref)

    a = jnp.dot(x_ref[0].astype(BF16), wa_ref[...], preferred_element_type=F32)
    z = jnp.dot(a.astype(BF16), wup_ref[...], preferred_element_type=F32) + balpha_ref[...]
    log_a = (jnp.minimum(z, 0.0) - jnp.log1p(jnp.exp(-jnp.abs(z)))) / GLA_TAU
    tri = tri_ref[...]
    b = jnp.concatenate([_split_dot(tri, log_a[blk * GLA_BLOCK:(blk + 1) * GLA_BLOCK])
                         for blk in range(GLA_T // GLA_BLOCK)], axis=0)
    b_end = jnp.concatenate(
        [jnp.broadcast_to(b[(n + 1) * GLA_CHUNK - 1:(n + 1) * GLA_CHUNK], (GLA_CHUNK, GLA_QK_W))
         for n in range(n_chunks)], axis=0)
    c = b_end - b
    eb = jnp.exp(b)
    kf = k_ref[0].astype(F32)
    qt = (q_ref[0].astype(F32) * eb * (GLA_DK ** -0.5)).astype(BF16)
    kt = (kf * jnp.exp(-b)).astype(BF16)
    ks = (kf * jnp.exp(c)).astype(BF16)
    causal = tri > 0

    for h in range(GLA_HEADS):
        ck = slice(h * GLA_DK, (h + 1) * GLA_DK)
        cv = slice(h * GLA_DV, (h + 1) * GLA_DV)
        qt_h, kt_h, ks_h = qt[:, ck], kt[:, ck], ks[:, ck]
        v_h = v_ref[0, :, cv]
        intra = []
        for blk in range(GLA_T // GLA_BLOCK):
            rb = slice(blk * GLA_BLOCK, (blk + 1) * GLA_BLOCK)
            att = lax.dot_general(qt_h[rb], kt_h[rb], (((1,), (1,)), ((), ())), preferred_element_type=F32)
            att = jnp.where(causal, att, 0.0).astype(BF16)
            intra.append(jnp.dot(att, v_h[rb], preferred_element_type=F32))
        o_h = jnp.concatenate(intra, axis=0)

        state = state_ref[h]
        for n in range(n_chunks):
            rs = slice(n * GLA_CHUNK, (n + 1) * GLA_CHUNK)
            contrib = lax.dot_general(v_h[rs], ks_h[rs], (((0,), (0,)), ((), ())),
                                      preferred_element_type=F32)
            sprev_ref[h, n] = state.astype(BF16)
            decay = eb[(n + 1) * GLA_CHUNK - 1:(n + 1) * GLA_CHUNK, ck]
            state = state * decay + contrib
        state_ref[h] = state

        inter = [lax.dot_general(qt_h[n * GLA_CHUNK:(n + 1) * GLA_CHUNK], sprev_ref[h, n],
                                 (((1,), (1,)), ((), ())), preferred_element_type=F32)
                 for n in range(n_chunks)]
        o_h = o_h + jnp.concatenate(inter, axis=0)

        o_h = o_h * lax.rsqrt(jnp.mean(jnp.square(o_h), axis=-1, keepdims=True) + RMS_EPS)
        o_h = o_h * gnorm_ref[:, cv]
        r_h = r_ref[0, :, cv].astype(F32)
        o_ref[0, :, cv] = (o_h * (r_h * jax.nn.sigmoid(r_h))).astype(o_ref.dtype)


def _gla(h_main, x, w_a, wup, balpha, gnorm, tri, batch, seq):
    h3 = h_main.reshape(batch, seq, MAIN_W)
    x3 = x.reshape(batch, seq, D_MODEL)
    n_chunks = GLA_T // GLA_CHUNK
    return pl.pallas_call(
        _gla_kernel,
        grid=(batch, seq // GLA_T),
        in_specs=[pl.BlockSpec((1, GLA_T, GLA_QK_W), lambda b, t: (b, t, 0)),
                  pl.BlockSpec((1, GLA_T, GLA_QK_W), lambda b, t: (b, t, 1)),
                  pl.BlockSpec((1, GLA_T, GLA_V_W), lambda b, t: (b, t, 1)),
                  pl.BlockSpec((1, GLA_T, GLA_V_W), lambda b, t: (b, t, 2)),
                  pl.BlockSpec((1, GLA_T, D_MODEL), lambda b, t: (b, t, 0)),
                  pl.BlockSpec((D_MODEL, LANES), lambda b, t: (0, 0)),
                  pl.BlockSpec((LANES, GLA_QK_W), lambda b, t: (0, 0)),
                  pl.BlockSpec((1, GLA_QK_W), lambda b, t: (0, 0)),
                  pl.BlockSpec((1, GLA_V_W), lambda b, t: (0, 0)),
                  pl.BlockSpec((GLA_BLOCK, GLA_BLOCK), lambda b, t: (0, 0))],
        out_specs=pl.BlockSpec((1, GLA_T, GLA_V_W), lambda b, t: (b, t, 0)),
        out_shape=jax.ShapeDtypeStruct((batch, seq, GLA_V_W), BF16),
        scratch_shapes=[pltpu.VMEM((GLA_HEADS, GLA_DV, GLA_DK), F32),
                        pltpu.VMEM((GLA_HEADS, n_chunks, GLA_DV, GLA_DK), BF16)],
        compiler_params=_cp("parallel", "arbitrary"),
        name="gla",
    )(h3, h3, h3, h3, x3, w_a, wup, balpha, gnorm, tri)


def _gla_masks():
    i = jnp.arange(GLA_BLOCK)[:, None]
    j = jnp.arange(GLA_BLOCK)[None, :]
    same = (i // GLA_CHUNK) == (j // GLA_CHUNK)
    return (same & (j <= i)).astype(BF16)


def _dil_attn_kernel(q_ref, kc_ref, kp_ref, vc_ref, vp_ref, o_ref, lse_ref, *, tq, residues):
    n = pl.program_id(2)
    st = DIL_STEPS
    qi = lax.broadcasted_iota(I32, (st, 2 * st), 0)
    kj = lax.broadcasted_iota(I32, (st, 2 * st), 1)
    in_prev = kj < st
    band_bias = jnp.where(jnp.where(in_prev, kj - qi, qi + st - kj) >= 0, 0.0, NEG_INF)
    prev_cols = jnp.where(in_prev, 1.0, 0.0)
    lane = lax.broadcasted_iota(I32, (st, LANES), 1)
    low = lane < DIL_HD
    ones = jnp.ones((2 * st, LANES), BF16)
    nt = (((1,), (1,)), ((), ()))

    no_prev = jnp.where(n == 0, NEG_INF, 0.0)
    first_bias = band_bias + prev_cols * no_prev

    def window(cur_ref, prev_ref, rr, i, cs):
        if i == 0:
            return jnp.concatenate([prev_ref[0, 0, rr, :, cs], cur_ref[0, 0, rr, 0:st, cs]], axis=0)
        return cur_ref[0, 0, rr, (i - 1) * st:(i + 1) * st, cs]

    quad_w = 2 * LANES
    quad_lane = lax.broadcasted_iota(I32, (st, quad_w), 1)
    head_in_quad = (quad_lane % LANES) // (DIL_HD // 2)

    for rr, i in [(rr, i) for rr in range(residues) for i in range(tq // st)]:
        r0 = i * st
        bias = first_bias if i == 0 else band_bias
        bias4 = jnp.concatenate([bias] * 4, axis=0)
        stats = jnp.zeros((st, LANES), F32)
        outs = []
        for quad in range(DIL_HEADS // 4):
            qs = slice(quad * quad_w, (quad + 1) * quad_w)
            qq = q_ref[0, 0, rr, r0:r0 + st, qs]
            zero = jnp.zeros_like(qq)
            lhs = jnp.concatenate([jnp.where(head_in_quad == u, qq, zero) for u in range(4)], axis=0)
            s = lax.dot_general(lhs, window(kc_ref, kp_ref, rr, i, qs), nt,
                                preferred_element_type=F32) + bias4
            m = jnp.max(s, axis=-1, keepdims=True)
            pexp = jnp.exp(s - m).astype(BF16)
            m_rep = jnp.broadcast_to(m, (4 * st, LANES))
            for pair in range(2):
                p = 2 * quad + pair
                cs = slice(p * LANES, (p + 1) * LANES)
                v_ext = jnp.concatenate([window(vc_ref, vp_ref, rr, i, cs), ones], axis=1)
                o = jnp.dot(pexp[2 * pair * st:(2 * pair + 2) * st], v_ext,
                            preferred_element_type=F32)
                outs.append(jnp.where(low, o[0:st, 0:LANES], o[st:2 * st, 0:LANES]))
                l_rep = o[:, LANES:2 * LANES]
                for half in range(2):
                    rs = slice(half * st, (half + 1) * st)
                    ms = slice((2 * pair + half) * st, (2 * pair + half + 1) * st)
                    stats = jnp.where(lane == 2 * p + half, m_rep[ms], stats)
                    stats = jnp.where(lane == DIL_HEADS + 2 * p + half, l_rep[rs], stats)
        o_ref[0, rr, r0:r0 + st, :] = jnp.concatenate(outs, axis=1).astype(o_ref.dtype)
        lse_ref[0, rr, r0:r0 + st, :] = stats


def _dil_attn(qkv, batch, seq, dil):
    sub = seq // dil
    tq = min(ATTN_TQ, sub)
    per = tq // DIL_STEPS
    rb = min(dil, ATTN_TQ // tq)

    def cur(which):
        return pl.BlockSpec((1, 1, rb, tq, DIL_GROUP_W), lambda b, r, n: (which, b, r, n, 0))

    def prev(which):
        return pl.BlockSpec((1, 1, rb, DIL_STEPS, DIL_GROUP_W),
                            lambda b, r, n: (which, b, r, jnp.maximum(n * per - 1, 0), 0))

    return pl.pallas_call(
        functools.partial(_dil_attn_kernel, tq=tq, residues=rb),
        grid=(batch, dil // rb, sub // tq),
        in_specs=[cur(0), cur(1), prev(1), cur(2), prev(2)],
        out_specs=[pl.BlockSpec((1, rb, tq, DIL_GROUP_W), lambda b, r, n: (b, r, n, 0)),
                   pl.BlockSpec((1, rb, tq, LANES), lambda b, r, n: (b, r, n, 0))],
        out_shape=[jax.ShapeDtypeStruct((batch, dil, sub, DIL_GROUP_W), BF16),
                   jax.ShapeDtypeStruct((batch, dil, sub, LANES), F32)],
        compiler_params=_cp("parallel", "parallel", "arbitrary"),
        name=f"dil_attn{dil}",
    )(qkv, qkv, qkv, qkv, qkv)


def _store_token_tiles(ref, val):
    rows = val.shape[0]
    for c in range(TOKEN_ROWS):
        ref[pl.ds(c, rows, stride=TOKEN_ROWS), :] = val[:, c * LANES:(c + 1) * LANES]


def _load_token_tiles(ref, rows):
    return jnp.concatenate([ref[pl.ds(c, rows, stride=TOKEN_ROWS), :] for c in range(TOKEN_ROWS)], axis=1)


def _layer_norm(x, g, b):
    mu = jnp.mean(x, axis=-1, keepdims=True)
    var = jnp.mean(jnp.square(x - mu), axis=-1, keepdims=True)
    return (x - mu) * lax.rsqrt(var + LN_EPS) * g + b


def _mix_kernel(og_ref, g1_ref, g2_ref, o0_ref, o1_ref, o2_ref, l0_ref, l1_ref, l2_ref, x_ref,
                wgo_ref, wao_ref, wout_ref, lng_ref, lnb_ref, wrt_ref, br_ref, expand_ref,
                x1_ref, bucket_ref, osc_ref, lsc_ref):
    tm = x_ref.shape[0]

    def interleaved(o_ref, l_ref, dil, slot):
        if dil == 1:
            return o_ref[0, 0].astype(F32), l_ref[0, 0]
        rows = tm // dil
        chunks = DIL_GROUP_W // LANES
        for r in range(dil):
            for c in range(chunks):
                osc_ref[slot, c, pl.ds(r, rows, stride=dil), :] = (
                    o_ref[0, r, :, c * LANES:(c + 1) * LANES].astype(F32))
            lsc_ref[slot, pl.ds(r, rows, stride=dil), :] = l_ref[0, r]
        return jnp.concatenate([osc_ref[slot, c] for c in range(chunks)], axis=1), lsc_ref[slot]

    groups = [interleaved(o_ref, l_ref, dil, slot)
              for slot, (o_ref, l_ref, (_, dil)) in enumerate(
                  zip((o0_ref, o1_ref, o2_ref), (l0_ref, l1_ref, l2_ref), DIL_PATTERN))]
    stats = [g[1] for g in groups]
    m_max = jnp.maximum(jnp.maximum(stats[0], stats[1]), stats[2])
    es = [jnp.exp(s - m_max) for s in stats]
    ls = [pltpu.roll(s, LANES - DIL_HEADS, 1) for s in stats]
    inv = 1.0 / (es[0] * ls[0] + es[1] * ls[1] + es[2] * ls[2])
    head_lane = lax.broadcasted_iota(I32, (tm, LANES), 1) < DIL_HEADS
    att = jnp.zeros((tm, DIL_GROUP_W), F32)
    for (num_g, _), e in zip(groups, es):
        coef = jnp.where(head_lane, e * inv, 0.0)
        att = att + _split_dot_rhs(coef, expand_ref[...]) * num_g

    y_att = jnp.dot(att.astype(BF16), wao_ref[...], preferred_element_type=F32)
    y_gla = jnp.dot(og_ref[...], wgo_ref[...], preferred_element_type=F32)
    mixed = (jax.nn.sigmoid(g1_ref[...].astype(F32)) * y_gla
             + jax.nn.sigmoid(g2_ref[...].astype(F32)) * y_att)
    mix = jnp.dot(mixed.astype(BF16), wout_ref[...], preferred_element_type=F32)
    x1 = _layer_norm(DN_ALPHA * x_ref[...] + mix, lng_ref[...], lnb_ref[...])
    _store_token_tiles(x1_ref, x1)

    logits = lax.dot_general(wrt_ref[...], x1.astype(BF16), (((1,), (1,)), ((), ())),
                             preferred_element_type=F32) + br_ref[:, 0:1]
    e = jnp.exp(logits - jnp.max(logits, axis=0, keepdims=True))
    probs = e / jnp.sum(e, axis=0, keepdims=True)
    best = None
    for g in range(N_GROUPS):
        for pi, (ea, eb_) in enumerate(PAIRS):
            cand = probs[4 * g + ea:4 * g + ea + 1] + probs[4 * g + eb_:4 * g + eb_ + 1]
            idx = g * len(PAIRS) + pi
            if best is None:
                best, bidx = cand, jnp.zeros(cand.shape, I32)
            else:
                better = cand > best
                best = jnp.where(better, cand, best)
                bidx = jnp.where(better, idx, bidx)
    bucket_ref[0] = bidx


def _split_dot_rhs(val_f32, mat_bf16):
    hi = val_f32.astype(BF16)
    lo = (val_f32 - hi.astype(F32)).astype(BF16)
    return (jnp.dot(hi, mat_bf16, preferred_element_type=F32)
            + jnp.dot(lo, mat_bf16, preferred_element_type=F32))


def _mix(og, h_main, attn, x, wgo, wao, wout, lng, lnb, wrt, br, expand, batch, seq):
    n = batch * seq
    tm = min(MIX_TM, seq)
    tps = seq // tm
    row = lambda i: (i, 0)
    const = lambda i: (0, 0)
    o_specs, l_specs, o_args, l_args = [], [], [], []
    for (o_g, lse_g), (_, dil) in zip(attn, DIL_PATTERN):
        o_specs.append(pl.BlockSpec((1, dil, tm // dil, DIL_GROUP_W), lambda i: (i // tps, 0, i % tps, 0)))
        l_specs.append(pl.BlockSpec((1, dil, tm // dil, LANES), lambda i: (i // tps, 0, i % tps, 0)))
        o_args.append(o_g)
        l_args.append(lse_g)
    gate_blk = GATE_COL // D_MODEL
    return pl.pallas_call(
        _mix_kernel,
        grid=(n // tm,),
        in_specs=[pl.BlockSpec((tm, GLA_V_W), row),
                  pl.BlockSpec((tm, D_MODEL), lambda i: (i, gate_blk)),
                  pl.BlockSpec((tm, D_MODEL), lambda i: (i, gate_blk + 1)),
                  *o_specs, *l_specs,
                  pl.BlockSpec((tm, D_MODEL), row),
                  pl.BlockSpec((GLA_V_W, D_MODEL), const),
                  pl.BlockSpec((DIL_GROUP_W, D_MODEL), const),
                  pl.BlockSpec((D_MODEL, D_MODEL), const),
                  pl.BlockSpec((1, D_MODEL), const),
                  pl.BlockSpec((1, D_MODEL), const),
                  pl.BlockSpec((N_EXPERTS, D_MODEL), const),
                  pl.BlockSpec((N_EXPERTS, LANES), const),
                  pl.BlockSpec((LANES, DIL_GROUP_W), const)],
        out_specs=[pl.BlockSpec((tm * TOKEN_ROWS, LANES), row),
                   pl.BlockSpec((1, 1, tm), lambda i: (i, 0, 0))],
        out_shape=[jax.ShapeDtypeStruct((n * TOKEN_ROWS, LANES), F32),
                   jax.ShapeDtypeStruct((n // tm, 1, tm), I32)],
        scratch_shapes=[pltpu.VMEM((N_DIL, DIL_GROUP_W // LANES, tm, LANES), F32),
                        pltpu.VMEM((N_DIL, tm, LANES), F32)],
        compiler_params=_cp("parallel"),
        name="mix",
    )(og, h_main, h_main, *o_args, *l_args, x, wgo, wao, wout, lng, lnb, wrt, br, expand)


def _rank_kernel(bucket_ref, upper_ref, rank_ref, counts_ref, carry_ref):
    @pl.when(pl.program_id(0) == 0)
    def _():
        carry_ref[...] = jnp.zeros_like(carry_ref)

    b = bucket_ref[0]
    ts = b.shape[1]
    onehot = lax.broadcasted_iota(I32, (BUCKET_ROWS, ts), 0) == b
    oh_f = jnp.where(onehot, 1.0, 0.0)
    before = jnp.dot(oh_f.astype(BF16), upper_ref[...], preferred_element_type=F32)
    carry = carry_ref[...]
    rank = jnp.sum(oh_f * (before + carry[:, 0:1]), axis=0, keepdims=True)
    rank_ref[0] = rank.astype(I32)
    carry = carry + jnp.sum(oh_f, axis=1, keepdims=True)
    carry_ref[...] = carry
    counts_ref[...] = carry.astype(I32)


def _bucket_ranks(bucket, n):
    ts = min(SORT_TS, n)
    b3 = bucket.reshape(n // ts, 1, ts)
    j = jnp.arange(ts)
    upper = (j[:, None] < j[None, :]).astype(BF16)
    rank, counts = pl.pallas_call(
        _rank_kernel,
        grid=(n // ts,),
        in_specs=[pl.BlockSpec((1, 1, ts), lambda i: (i, 0, 0)),
                  pl.BlockSpec((ts, ts), lambda i: (0, 0))],
        out_specs=[pl.BlockSpec((1, 1, ts), lambda i: (i, 0, 0)),
                   pl.BlockSpec((BUCKET_ROWS, LANES), lambda i: (0, 0))],
        out_shape=[jax.ShapeDtypeStruct((n // ts, 1, ts), I32),
                   jax.ShapeDtypeStruct((BUCKET_ROWS, LANES), I32)],
        scratch_shapes=[pltpu.VMEM((BUCKET_ROWS, LANES), F32)],
        compiler_params=_cp("arbitrary"),
        name="bucket_rank",
    )(b3, upper)
    return rank.reshape(n), counts[:N_BUCKETS, 0]


def _sort_plan(bucket, rank, counts, n):
    tiles = (counts + MOE_T - 1) // MOE_T
    tile_end = jnp.cumsum(tiles)
    offsets = (tile_end - tiles) * MOE_T
    dest = offsets[bucket] + rank
    n_tiles = n // MOE_T + N_BUCKETS
    tile_ids = jnp.arange(n_tiles, dtype=I32)
    total = tile_end[-1]
    last = jnp.minimum(tile_ids, total - 1)
    tb = jnp.sum((last[:, None] >= tile_end[None, :]).astype(I32), axis=1)
    tb = jnp.minimum(tb, N_BUCKETS - 1)
    n_valid = jnp.clip(counts[tb] - (tile_ids - (tile_end - tiles)[tb]) * MOE_T, 0, MOE_T)
    n_valid = jnp.where(tile_ids < total, n_valid, 0)
    pair = jnp.array(PAIRS, dtype=I32)
    e_lo = (tb // len(PAIRS)) * EXPERTS_PER_GROUP + pair[tb % len(PAIRS), 0]
    e_hi = (tb // len(PAIRS)) * EXPERTS_PER_GROUP + pair[tb % len(PAIRS), 1]
    pad_lo = jnp.concatenate([offsets + counts, total[None] * MOE_T])
    pad_hi = jnp.concatenate([tile_end * MOE_T, jnp.full((1,), n_tiles * MOE_T, I32)])
    pad_ranges = jnp.stack([pad_lo, pad_hi]).astype(I32)
    return dest.astype(I32), tb, e_lo, e_hi, n_valid.astype(I32), pad_ranges


def _invert_kernel(pad_ref, dest_ref, src_ref, table_ref, sem, *, chunk):
    i = pl.program_id(0)

    @pl.when(i == 0)
    def _():
        def clear(j, carry):
            table_ref[j] = 0
            return carry
        for b in range(N_BUCKETS + 1):
            lax.fori_loop(pad_ref[0, b], pad_ref[1, b], clear, 0)

    def put(j, carry):
        table_ref[dest_ref[0, 0, j]] = i * chunk + j
        return carry
    lax.fori_loop(0, chunk, put, 0, unroll=32)

    @pl.when(i == pl.num_programs(0) - 1)
    def _():
        copy = pltpu.make_async_copy(table_ref, src_ref, sem)
        copy.start()
        copy.wait()


def _invert(dest, pad_ranges, n_slots):
    n = dest.shape[0]
    chunk = min(2048, n)
    return pl.pallas_call(
        functools.partial(_invert_kernel, chunk=chunk),
        grid=(n // chunk,),
        in_specs=[pl.BlockSpec(memory_space=pltpu.SMEM),
                  pl.BlockSpec((1, 1, chunk), lambda i: (i, 0, 0), memory_space=pltpu.SMEM)],
        out_specs=pl.BlockSpec(memory_space=pl.ANY),
        out_shape=jax.ShapeDtypeStruct((n_slots,), I32),
        scratch_shapes=[pltpu.SMEM((n_slots,), I32), pltpu.SemaphoreType.DMA(())],
        compiler_params=_cp("arbitrary"),
        name="invert_perm",
    )(pad_ranges, dest.reshape(n // chunk, 1, chunk))


def _moe_kernel(tb_ref, elo_ref, ehi_ref, nvalid_ref, prv_ref, src_ref, nxt_ref, x_hbm, wr2_ref, br2_ref,
                wg_lo_ref, wu_lo_ref, wd_lo_ref, wg_hi_ref, wu_hi_ref, wd_hi_ref, lng_ref, lnb_ref,
                out_hbm, xbuf_ref, ybuf_ref, gsem, ssem, *, n_tokens):
    del tb_ref, elo_ref, ehi_ref
    t = pl.program_id(0)
    last = pl.num_programs(0) - 1
    slot = t % 2
    other = 1 - slot
    tile = MOE_T
    n_chunks = 8
    per_chunk = tile // n_chunks

    def token_rows(tok):
        if isinstance(tok, int):
            return pl.ds(tok * TOKEN_ROWS, TOKEN_ROWS)
        return pl.ds(pl.multiple_of(tok * TOKEN_ROWS, TOKEN_ROWS), TOKEN_ROWS)

    def gather_copy(idx_ref, r, sl):
        return pltpu.make_async_copy(x_hbm.at[token_rows(idx_ref[0, 0, r])],
                                     xbuf_ref.at[sl, token_rows(r)], gsem.at[sl])

    def scatter_copy(dst, r, sl):
        return pltpu.make_async_copy(ybuf_ref.at[sl, token_rows(r)], out_hbm.at[token_rows(dst)], ssem.at[sl])

    def scatter_rows(idx_ref, n_valid, rows, sl):
        for r in rows:
            dst = jnp.where(r < n_valid, idx_ref[0, 0, r], n_tokens + sl * tile + r)
            scatter_copy(dst, r, sl).start(priority=r % 2)

    def wait_scatter(sl):
        for r in range(tile):
            scatter_copy(0, r, sl).wait()

    prev_valid = jnp.where(t > 0, nvalid_ref[jnp.maximum(t - 1, 0)], 0)

    def issue(chunk):
        rows = range(chunk * per_chunk, (chunk + 1) * per_chunk)
        for r in rows:
            gather_copy(nxt_ref, r, other).start(priority=r % 2)
        scatter_rows(prv_ref, prev_valid, rows, other)

    @pl.when(t == 0)
    def _():
        ybuf_ref[...] = jnp.zeros_like(ybuf_ref)
        for r in range(tile):
            gather_copy(src_ref, r, 0).start()
        scatter_rows(src_ref, 0, range(tile), 0)

    for r in range(tile):
        gather_copy(src_ref, r, slot).wait()

    @pl.when(nvalid_ref[t] == 0)
    def _():
        wait_scatter(slot)
        for chunk in range(n_chunks):
            issue(chunk)

    @pl.when(nvalid_ref[t] != 0)
    def _():
        x = _load_token_tiles(xbuf_ref.at[slot], tile)
        xb = x.astype(BF16)
        issue(0)
        issue(1)
        l2 = jnp.dot(xb, wr2_ref[0], preferred_element_type=F32) + br2_ref[0]
        g_lo = jax.nn.sigmoid(l2[:, 0:1] - l2[:, 1:2])
        g_hi = jax.nn.sigmoid(l2[:, 1:2] - l2[:, 0:1])
        issue(2)
        issue(3)

        def expert(wg_ref, wu_ref, wd_ref, chunks):
            a = jnp.dot(xb, wg_ref[0], preferred_element_type=F32)
            for chunk in chunks[:2]:
                issue(chunk)
            u = jnp.dot(xb, wu_ref[0], preferred_element_type=F32)
            for chunk in chunks[2:]:
                issue(chunk)
            hidden = (a * jax.nn.sigmoid(a) * u).astype(BF16)
            return jnp.dot(hidden, wd_ref[0], preferred_element_type=F32)

        y = g_lo * expert(wg_lo_ref, wu_lo_ref, wd_lo_ref, (4, 5, 6, 7))
        y = y + g_hi * expert(wg_hi_ref, wu_hi_ref, wd_hi_ref, ())
        y = _layer_norm(DN_ALPHA * x + y, lng_ref[...], lnb_ref[...])
        wait_scatter(slot)
        _store_token_tiles(ybuf_ref.at[slot], y)

    @pl.when(t == last)
    def _():
        for r in range(tile):
            gather_copy(nxt_ref, r, other).wait()
        scatter_rows(src_ref, nvalid_ref[t], range(tile), slot)
        for sl in range(2):
            wait_scatter(sl)


def _moe(x1, src, plan, wr2, br2, wg, wu, wd, lng, lnb):
    _, tb, e_lo, e_hi, n_valid, _ = plan
    n = x1.shape[0] // TOKEN_ROWS
    n_tiles = src.shape[0] // MOE_T
    src3 = src.reshape(n_tiles, 1, MOE_T)
    lo = lambda t, tb, elo, ehi, v: (elo[t], 0, 0)
    hi = lambda t, tb, elo, ehi, v: (ehi[t], 0, 0)
    const = lambda t, tb, elo, ehi, v: (0, 0)
    wblk = (1, D_MODEL, D_FF)
    idx_blk = (1, 1, MOE_T)
    grid_spec = pltpu.PrefetchScalarGridSpec(
        num_scalar_prefetch=4,
        grid=(n_tiles,),
        in_specs=[pl.BlockSpec(idx_blk, lambda t, tb, elo, ehi, v: (jnp.maximum(t - 1, 0), 0, 0),
                               memory_space=pltpu.SMEM),
                  pl.BlockSpec(idx_blk, lambda t, tb, elo, ehi, v: (t, 0, 0), memory_space=pltpu.SMEM),
                  pl.BlockSpec(idx_blk, lambda t, tb, elo, ehi, v: (jnp.minimum(t + 1, n_tiles - 1), 0, 0),
                               memory_space=pltpu.SMEM),
                  pl.BlockSpec(memory_space=pl.ANY),
                  pl.BlockSpec((1, D_MODEL, LANES), lambda t, tb, elo, ehi, v: (tb[t], 0, 0)),
                  pl.BlockSpec((1, 1, LANES), lambda t, tb, elo, ehi, v: (tb[t], 0, 0)),
                  pl.BlockSpec(wblk, lo), pl.BlockSpec(wblk, lo), pl.BlockSpec((1, D_FF, D_MODEL), lo),
                  pl.BlockSpec(wblk, hi), pl.BlockSpec(wblk, hi), pl.BlockSpec((1, D_FF, D_MODEL), hi),
                  pl.BlockSpec((1, D_MODEL), const), pl.BlockSpec((1, D_MODEL), const)],
        out_specs=pl.BlockSpec(memory_space=pl.ANY),
        scratch_shapes=[pltpu.VMEM((2, MOE_T * TOKEN_ROWS, LANES), F32),
                        pltpu.VMEM((2, MOE_T * TOKEN_ROWS, LANES), F32),
                        pltpu.SemaphoreType.DMA((2,)), pltpu.SemaphoreType.DMA((2,))],
    )
    return pl.pallas_call(
        functools.partial(_moe_kernel, n_tokens=n),
        grid_spec=grid_spec,
        out_shape=jax.ShapeDtypeStruct(((n + 2 * MOE_T) * TOKEN_ROWS, LANES), F32),
        compiler_params=_cp("arbitrary"),
        name="moe",
    )(tb, e_lo, e_hi, n_valid, src3, src3, src3, x1, wr2, br2, wg, wu, wd, wg, wu, wd, lng, lnb)


def _untile_kernel(xt_ref, x_ref, xb_ref):
    x = _load_token_tiles(xt_ref, x_ref.shape[0])
    x_ref[...] = x
    xb_ref[...] = x.astype(BF16)


def _untile(xt, n):
    tm = min(PROJ_TM, n)
    return pl.pallas_call(
        _untile_kernel,
        grid=(n // tm,),
        in_specs=[pl.BlockSpec((tm * TOKEN_ROWS, LANES), lambda i: (i, 0))],
        out_specs=[pl.BlockSpec((tm, D_MODEL), lambda i: (i, 0))] * 2,
        out_shape=[jax.ShapeDtypeStruct((n, D_MODEL), F32), jax.ShapeDtypeStruct((n, D_MODEL), BF16)],
        compiler_params=_cp("parallel"),
        name="untile",
    )(xt)


def _split_w_in(w_in_l):
    sizes = (GLA_QK_W, GLA_QK_W, GLA_V_W, GLA_V_W, GLA_RANK,
             N_DIL * DIL_GROUP_W, N_DIL * DIL_GROUP_W, N_DIL * DIL_GROUP_W, 2 * D_MODEL)
    starts = [sum(sizes[:i]) for i in range(len(sizes))]
    blk = [w_in_l[:, s:s + z] for s, z in zip(starts, sizes)]
    q_g, k_g, v_g, r_g, a_g, q_d, k_d, v_d, gates = blk
    w_main = jnp.concatenate([q_g, k_g, v_g, r_g, gates], axis=1).astype(BF16)
    w_a = jnp.pad(a_g, ((0, 0), (0, LANES - GLA_RANK))).astype(BF16)
    quad = jnp.asarray(_quad_layout())
    w_dil = []
    for g in range(N_DIL):
        gs = slice(g * DIL_GROUP_W, (g + 1) * DIL_GROUP_W)
        w_dil.append(jnp.concatenate([q_d[:, gs][:, quad], k_d[:, gs][:, quad], v_d[:, gs]],
                                     axis=1).astype(BF16))
    return w_main, w_a, w_dil


def _quad_layout():
    half = DIL_HD // 2
    cols = []
    for quad in range(DIL_HEADS // 4):
        for part in range(2):
            for u in range(4):
                start = (4 * quad + u) * DIL_HD + part * half
                cols.extend(range(start, start + half))
    return cols


def _router_pair_tables(w_router, b_router):
    cols_w, cols_b = [], []
    for g in range(N_GROUPS):
        for ea, eb_ in PAIRS:
            lo, hi = EXPERTS_PER_GROUP * g + ea, EXPERTS_PER_GROUP * g + eb_
            w2 = jnp.stack([w_router[:, lo], w_router[:, hi]], axis=1)
            cols_w.append(jnp.pad(w2, ((0, 0), (0, LANES - 2))))
            cols_b.append(jnp.pad(jnp.stack([b_router[lo], b_router[hi]]), (0, LANES - 2)))
    return jnp.stack(cols_w).astype(BF16), jnp.stack(cols_b).reshape(N_BUCKETS, 1, LANES).astype(F32)


def kernel(x, positions, w_in, w_alpha_up, b_alpha, gla_norm, w_gla_o, w_att_o, w_out, ln1_g, ln1_b,
           w_router, b_router, w_e_gate, w_e_up, w_e_down, ln2_g, ln2_b):
    batch, seq, d_model = x.shape
    depth = w_in.shape[0]
    n = batch * seq
    assert d_model == D_MODEL and seq % (DIL_PATTERN[-1][1] * DIL_STEPS) == 0

    cos_t, sin_t = _rope_tables(positions)
    tri = _gla_masks()
    head_of_col = jnp.arange(DIL_GROUP_W) // DIL_HD
    expand = (jnp.arange(LANES)[:, None] == head_of_col[None, :]).astype(BF16)
    wrt = w_router.T.astype(BF16)
    br = jnp.broadcast_to(b_router.astype(F32)[:, None], (N_EXPERTS, LANES))
    wr2, br2 = _router_pair_tables(w_router, b_router)
    n_slots = n + N_BUCKETS * MOE_T

    xf = xin = x.reshape(n, D_MODEL)
    for i in range(depth):
        w_main, w_a, w_dil = _split_w_in(w_in[i])
        h_main = _proj(xin, w_main, BF16, PROJ_TN, n)
        wup = jnp.pad(w_alpha_up[i], ((0, LANES - GLA_RANK), (0, 0))).astype(BF16)
        og = _gla(h_main, xin, w_a, wup, b_alpha[i].reshape(1, -1).astype(F32),
                  gla_norm[i].reshape(1, -1).astype(F32), tri, batch, seq)
        attn = []
        for g, (_, dil) in enumerate(DIL_PATTERN):
            qkv = _proj_dil(xin, w_dil[g], cos_t, sin_t, batch, seq, dil)
            attn.append(_dil_attn(qkv, batch, seq, dil))
        x1, bucket = _mix(og.reshape(n, GLA_V_W), h_main, attn, xf,
                          w_gla_o[i].astype(BF16), w_att_o[i].astype(BF16), w_out[i].astype(BF16),
                          ln1_g[i].reshape(1, -1), ln1_b[i].reshape(1, -1), wrt, br, expand, batch, seq)
        bucket = bucket.reshape(n)
        rank, counts = _bucket_ranks(bucket, n)
        plan = _sort_plan(bucket, rank, counts, n)
        src = _invert(plan[0], plan[5], n_slots)
        xt = _moe(x1, src, plan, wr2, br2, w_e_gate[i].astype(BF16), w_e_up[i].astype(BF16),
                  w_e_down[i].astype(BF16), ln2_g[i].reshape(1, -1), ln2_b[i].reshape(1, -1))
        xf, xin = _untile(xt, n)
    return xf.reshape(batch, seq, D_MODEL)
```
